```python
import math
import jax, jax.numpy as jnp
from jax import lax
import numpy as np

D_MODEL = 1024
BATCH = 2
SEQ = 8192
DEPTH = 1

EPS = 1e-6
DA_HEADS = 4
DA_HEAD_DIM = D_MODEL // 16
DA_V_DIM = 2 * DA_HEAD_DIM
DA_QK = DA_HEADS * 2 * DA_HEAD_DIM
DA_WIDTH = DA_HEADS * DA_V_DIM
Q_BLOCK = 128
GDN_HEADS = 4
GDN_HEAD_DIM = D_MODEL // 8
GDN_WIDTH = GDN_HEADS * GDN_HEAD_DIM
GDN_CONV = 4
GDN_CHUNK = 64
D_MIX = DA_WIDTH + GDN_WIDTH
SPLITS = [
    DA_QK,
    2 * DA_QK,
    2 * DA_QK + DA_WIDTH,
    2 * DA_QK + DA_WIDTH + GDN_WIDTH,
    2 * DA_QK + DA_WIDTH + 2 * GDN_WIDTH,
    2 * DA_QK + DA_WIDTH + 3 * GDN_WIDTH,
    2 * DA_QK + DA_WIDTH + 3 * GDN_WIDTH + GDN_HEADS,
    2 * DA_QK + DA_WIDTH + 3 * GDN_WIDTH + 2 * GDN_HEADS,
]
D_IN_PROJ = SPLITS[-1] + GDN_WIDTH
D_FF = ((8 * D_MODEL // 3 + 127) // 128) * 128
FFN_CONV = 3

kernel_name = "hybrid_diffattn_gdn_convffn"


def rms_norm(x, g):
    x32 = x.astype(jnp.float32)
    y = x32 * lax.rsqrt(jnp.mean(x32 * x32, axis=-1, keepdims=True) + EPS)
    return (y * g.astype(jnp.float32)).astype(x.dtype)


def l2_normalize(x):
    x32 = x.astype(jnp.float32)
    return x32 * lax.rsqrt(jnp.sum(x32 * x32, axis=-1, keepdims=True) + EPS)


def causal_depthwise_conv(x, w):
    K = w.shape[0]
    S = x.shape[1]
    xp = jnp.pad(x, ((0, 0), (K - 1, 0), (0, 0)))
    y = xp[:, 0:S] * w[0]
    for j in range(1, K):
        y = y + xp[:, j:j + S] * w[j]
    return y


def alibi_slopes(n_heads):
    return jnp.exp2(-8.0 * jnp.arange(1, n_heads + 1, dtype=jnp.float32) / n_heads)


def diff_attention(q, k, v, lam):
    B, S = q.shape[0], q.shape[1]
    nblk = S // Q_BLOCK
    qb = (q * DA_HEAD_DIM ** -0.5).reshape(B, nblk, Q_BLOCK, DA_HEADS, 2, DA_HEAD_DIM)
    qb = jnp.moveaxis(qb, 1, 0)
    slopes = alibi_slopes(DA_HEADS)
    key_pos = jnp.arange(S)

    def block(args):
        qi, idx = args
        q_pos = idx * Q_BLOCK + jnp.arange(Q_BLOCK)
        dist = q_pos[:, None] - key_pos[None, :]
        s = jnp.einsum('bqhcd,bkhcd->bhcqk', qi, k).astype(jnp.float32)
        s = s - slopes[None, :, None, None, None] * dist.astype(jnp.float32)
        s = jnp.where(dist >= 0, s, -jnp.inf)
        p = jax.nn.softmax(s, axis=-1)
        a = p[:, :, 0] - lam * p[:, :, 1]
        return jnp.einsum('bhqk,bkhd->bqhd', a.astype(v.dtype), v)

    out = lax.map(block, (qb, jnp.arange(nblk)))
    return jnp.moveaxis(out, 0, 1).reshape(B, S, DA_HEADS, DA_V_DIM)


def gated_delta_rule(q, k, v, g, beta):
    B, S, H, Dk = q.shape
    Dv = v.shape[-1]
    C = GDN_CHUNK
    N = S // C
    q = q * Dk ** -0.5

    def to_chunks(t):
        t = jnp.swapaxes(t, 1, 2)
        return t.reshape(B, H, N, C, *t.shape[3:])

    qc, kc, vc = to_chunks(q), to_chunks(k), to_chunks(v)
    gc = jnp.cumsum(to_chunks(g), axis=-1)
    bc = to_chunks(beta)
    tril = jnp.tril(jnp.ones((C, C), dtype=bool))
    strict = jnp.tril(jnp.ones((C, C), dtype=bool), -1)
    decay = jnp.exp(jnp.where(tril, gc[..., :, None] - gc[..., None, :], -jnp.inf))
    kb = kc * bc[..., None]
    vb = vc * bc[..., None]
    L = jnp.where(strict, jnp.einsum('bhnid,bhnjd->bhnij', kb, kc) * decay, 0.0)
    eye = jnp.eye(C, dtype=jnp.float32)
    T = lax.linalg.triangular_solve(eye + L, jnp.broadcast_to(eye, L.shape),
                                    left_side=True, lower=True, unit_diagonal=True)
    u = jnp.einsum('bhnij,bhnjd->bhnid', T, vb)
    w = jnp.einsum('bhnij,bhnjd->bhnid', T, kb * jnp.exp(gc)[..., None])
    qk = jnp.where(tril, jnp.einsum('bhnid,bhnjd->bhnij', qc, kc) * decay, 0.0)
    q_dec = qc * jnp.exp(gc)[..., None]
    k_dec = kc * jnp.exp(gc[..., -1:] - gc)[..., None]
    g_last = jnp.exp(gc[..., -1])

    def step(state, xs):
        qk_i, q_dec_i, k_dec_i, u_i, w_i, gl_i = xs
        v_new = u_i - jnp.einsum('bhck,bhkv->bhcv', w_i, state)
        o = jnp.einsum('bhck,bhkv->bhcv', q_dec_i, state) + jnp.einsum('bhij,bhjv->bhiv', qk_i, v_new)
        state = state * gl_i[..., None, None] + jnp.einsum('bhck,bhcv->bhkv', k_dec_i, v_new)
        return state, o

    xs = tuple(jnp.moveaxis(t, 2, 0) for t in (qk, q_dec, k_dec, u, w, g_last))
    state0 = jnp.zeros((B, H, Dk, Dv), dtype=jnp.float32)
    _, o = lax.scan(step, state0, xs)
    return jnp.transpose(o, (1, 0, 3, 2, 4)).reshape(B, S, H, Dv)


def setup_inputs(seed: int = 0) -> dict:
    key = jax.random.key(seed)
    ks = jax.random.split(key, 20)
    f32 = jnp.float32

    def nrm(k, shape, scale):
        return jax.random.normal(k, shape, dtype=f32) * scale

    def gain(k, n):
        return 1.0 + 0.02 * jax.random.normal(k, (DEPTH, n), dtype=f32)

    dt = jnp.exp(jax.random.uniform(ks[9], (DEPTH, GDN_HEADS), dtype=f32,
                                    minval=math.log(1e-3), maxval=math.log(1e-1)))
    return {
        "x": jax.random.normal(ks[0], (BATCH, SEQ, D_MODEL), dtype=f32),
        "attn_norm_g": gain(ks[1], D_MODEL),
        "w_in": nrm(ks[2], (DEPTH, D_MODEL, D_IN_PROJ), D_MODEL ** -0.5),
        "da_lambda_q1": nrm(ks[3], (DEPTH, DA_HEAD_DIM), 0.1),
        "da_lambda_k1": nrm(ks[4], (DEPTH, DA_HEAD_DIM), 0.1),
        "da_lambda_q2": nrm(ks[5], (DEPTH, DA_HEAD_DIM), 0.1),
        "da_lambda_k2": nrm(ks[6], (DEPTH, DA_HEAD_DIM), 0.1),
        "da_subln_g": gain(ks[7], DA_V_DIM),
        "gdn_conv_w": nrm(ks[8], (DEPTH, GDN_CONV, 3 * GDN_WIDTH), GDN_CONV ** -0.5),
        "gdn_a_log": jnp.log(jax.random.uniform(ks[10], (DEPTH, GDN_HEADS), dtype=f32, minval=1.0, maxval=16.0)),
        "gdn_dt_bias": dt + jnp.log(-jnp.expm1(-dt)),
        "gdn_norm_g": gain(ks[11], GDN_HEAD_DIM),
        "w_out": nrm(ks[12], (DEPTH, D_MIX, D_MODEL), D_MIX ** -0.5),
        "ffn_norm_g": gain(ks[13], D_MODEL),
        "w_up": nrm(ks[14], (DEPTH, D_MODEL, 2 * D_FF), D_MODEL ** -0.5),
        "ffn_conv_w": nrm(ks[15], (DEPTH, FFN_CONV, 2 * D_FF), FFN_CONV ** -0.5),
        "ffn_conv_b": nrm(ks[16], (DEPTH, 2 * D_FF), 0.02),
        "w_down": nrm(ks[17], (DEPTH, D_FF, D_MODEL), D_FF ** -0.5),
        "final_norm_g": 1.0 + 0.02 * jax.random.normal(ks[18], (D_MODEL,), dtype=f32),
    }


def reference(x, attn_norm_g, w_in, da_lambda_q1, da_lambda_k1, da_lambda_q2, da_lambda_k2,
              da_subln_g, gdn_conv_w, gdn_a_log, gdn_dt_bias, gdn_norm_g, w_out,
              ffn_norm_g, w_up, ffn_conv_w, ffn_conv_b, w_down, final_norm_g):
    B, S, _ = x.shape
    f32 = jnp.float32
    for l in range(DEPTH):
        lam_init = 0.8 - 0.6 * math.exp(-0.3 * l)
        h = rms_norm(x, attn_norm_g[l])
        proj = h @ w_in[l]
        dq, dk, dv, gq, gk, gv, ga, gb, gz = jnp.split(proj, SPLITS, axis=-1)

        lam = (jnp.exp(jnp.sum(da_lambda_q1[l].astype(f32) * da_lambda_k1[l].astype(f32)))
               - jnp.exp(jnp.sum(da_lambda_q2[l].astype(f32) * da_lambda_k2[l].astype(f32)))
               + lam_init)
        qa = dq.reshape(B, S, DA_HEADS, 2, DA_HEAD_DIM)
        ka = dk.reshape(B, S, DA_HEADS, 2, DA_HEAD_DIM)
        va = dv.reshape(B, S, DA_HEADS, DA_V_DIM)
        oa = diff_attention(qa, ka, va, lam)
        oa = (rms_norm(oa, da_subln_g[l]) * (1.0 - lam_init)).reshape(B, S, DA_WIDTH)

        qkv = jax.nn.silu(causal_depthwise_conv(jnp.concatenate([gq, gk, gv], axis=-1), gdn_conv_w[l]))
        qg, kg, vg = jnp.split(qkv, [GDN_WIDTH, 2 * GDN_WIDTH], axis=-1)
        qg = l2_normalize(qg.reshape(B, S, GDN_HEADS, GDN_HEAD_DIM))
        kg = l2_normalize(kg.reshape(B, S, GDN_HEADS, GDN_HEAD_DIM))
        vg = vg.reshape(B, S, GDN_HEADS, GDN_HEAD_DIM).astype(f32)
        beta = jax.nn.sigmoid(gb.astype(f32))
        g = -jnp.exp(gdn_a_log[l].astype(f32)) * jax.nn.softplus(ga.astype(f32) + gdn_dt_bias[l].astype(f32))
        og = gated_delta_rule(qg, kg, vg, g, beta).astype(x.dtype)
        z = gz.reshape(B, S, GDN_HEADS, GDN_HEAD_DIM)
        og = (rms_norm(og, gdn_norm_g[l]) * jax.nn.silu(z)).reshape(B, S, GDN_WIDTH)

        x = x + jnp.concatenate([oa, og], axis=-1) @ w_out[l]

        h = rms_norm(x, ffn_norm_g[l])
        up = causal_depthwise_conv(h @ w_up[l], ffn_conv_w[l]) + ffn_conv_b[l]
        gate, val = jnp.split(up, [D_FF], axis=-1)
        x = x + (jax.nn.silu(gate) * val) @ w_down[l]
    return rms_norm(x, final_norm_g)
```

```python
import functools
import math

import jax
import jax.numpy as jnp
from jax import lax
from jax.experimental import pallas as pl
from jax.experimental.pallas import tpu as pltpu

F32 = jnp.float32
BF16 = jnp.bfloat16

EPS = 1e-6
LOG2E = 1.4426950408889634
NEG_BIG = -1e30

D_MODEL = 1024
DA_HEADS = 4
DA_HEAD_DIM = 64
DA_V_DIM = 128
DA_WIDTH = DA_HEADS * DA_V_DIM
GDN_HEADS = 4
GDN_HEAD_DIM = 128
GDN_WIDTH = GDN_HEADS * GDN_HEAD_DIM
GDN_CONV = 4
GDN_CHUNK = 64
D_FF = 2816
FFN_CONV = 3
LAM_INIT = 0.8 - 0.6 * math.exp(-0.3 * 0)

LANES = 128
SUBLANES = 8
VMEM_LIMIT = 56 * 1024 * 1024

W1_COLS = 3 * DA_WIDTH + 3 * GDN_WIDTH + GDN_WIDTH + LANES


def _mm(a, b):
    return jnp.dot(a.astype(BF16), b.astype(BF16), preferred_element_type=F32)


def _mm_nt(a, b):
    return lax.dot_general(a.astype(BF16), b.astype(BF16), (((1,), (1,)), ((), ())),
                           preferred_element_type=F32)


def _mm_tn(a, b):
    return lax.dot_general(a.astype(BF16), b.astype(BF16), (((0,), (0,)), ((), ())),
                           preferred_element_type=F32)


def _mm_f32(a, b):
    return jnp.dot(a, b, preferred_element_type=F32, precision=lax.Precision.HIGHEST)


def _silu(x):
    return x * (1.0 / (1.0 + jnp.exp(-x)))


def _rms(x, g):
    return x * lax.rsqrt(jnp.mean(x * x, axis=-1, keepdims=True) + EPS) * g


def _in_proj_kernel(x_ref, g_ref, w_ref, qk_ref, va_ref, graw_ref, gz_ref, gab_ref):
    hb = _rms(x_ref[...], g_ref[...]).astype(BF16)

    def proj(lo, width):
        return jnp.dot(hb, w_ref[:, lo:lo + width], preferred_element_type=F32)

    qk_ref[:, 0:DA_WIDTH] = (proj(0, DA_WIDTH) * (DA_HEAD_DIM ** -0.5 * LOG2E)).astype(BF16)
    qk_ref[:, DA_WIDTH:2 * DA_WIDTH] = proj(DA_WIDTH, DA_WIDTH).astype(BF16)
    v = proj(2 * DA_WIDTH, DA_WIDTH)
    tm = v.shape[0]
    ones_col = jnp.where(lax.broadcasted_iota(jnp.int32, (tm, LANES), 1) == 0, 1.0, 0.0).astype(BF16)
    for h in range(DA_HEADS):
        va_ref[:, 2 * h * LANES:(2 * h + 1) * LANES] = v[:, h * DA_V_DIM:(h + 1) * DA_V_DIM].astype(BF16)
        va_ref[:, (2 * h + 1) * LANES:(2 * h + 2) * LANES] = ones_col
    base = 3 * DA_WIDTH
    for c in range(3):
        graw_ref[:, c * GDN_WIDTH:(c + 1) * GDN_WIDTH] = proj(base + c * GDN_WIDTH, GDN_WIDTH)
    gz_ref[...] = proj(base + 3 * GDN_WIDTH, GDN_WIDTH)
    gab_ref[...] = proj(base + 4 * GDN_WIDTH, LANES)


def _in_proj(x2, g, w1, tm):
    m = x2.shape[0]
    row = lambda i: (i, 0)
    fixed = lambda i: (0, 0)
    return pl.pallas_call(
        _in_proj_kernel,
        grid=(m // tm,),
        in_specs=[pl.BlockSpec((tm, D_MODEL), row),
                  pl.BlockSpec((1, D_MODEL), fixed),
                  pl.BlockSpec((D_MODEL, W1_COLS), fixed)],
        out_specs=[pl.BlockSpec((tm, 2 * DA_WIDTH), row),
                   pl.BlockSpec((tm, 2 * DA_WIDTH), row),
                   pl.BlockSpec((tm, 3 * GDN_WIDTH), row),
                   pl.BlockSpec((tm, GDN_WIDTH), row),
                   pl.BlockSpec((tm, LANES), row)],
        out_shape=[jax.ShapeDtypeStruct((m, 2 * DA_WIDTH), BF16),
                   jax.ShapeDtypeStruct((m, 2 * DA_WIDTH), BF16),
                   jax.ShapeDtypeStruct((m, 3 * GDN_WIDTH), F32),
                   jax.ShapeDtypeStruct((m, GDN_WIDTH), F32),
                   jax.ShapeDtypeStruct((m, LANES), F32)],
        compiler_params=pltpu.CompilerParams(dimension_semantics=("arbitrary",),
                                             vmem_limit_bytes=VMEM_LIMIT),
        name="in_proj",
    )(x2, g, w1)


def _attn_kernel(lam_ref, sg_ref, q_ref, k_ref, v_ref, o_ref, acc1_ref, acc2_ref, *, t):
    h = pl.program_id(1)
    qi = pl.program_id(2)

    q = q_ref[0]
    lane = lax.broadcasted_iota(jnp.int32, (t, LANES), 1)
    zero = jnp.zeros_like(q)
    q1 = jnp.where(lane < DA_HEAD_DIM, q, zero)
    q2 = jnp.where(lane >= DA_HEAD_DIM, q, zero)

    hf = (h + 1).astype(F32)
    slope_row = jnp.exp2(jnp.full((1, t), -8.0 / DA_HEADS, F32) * hf) * LOG2E
    col_f = lax.broadcasted_iota(jnp.int32, (1, t), 1).astype(F32)
    row_i = lax.broadcasted_iota(jnp.int32, (t, t), 0)
    col_i = lax.broadcasted_iota(jnp.int32, (t, t), 1)

    acc1_ref[...] = jnp.zeros_like(acc1_ref)
    acc2_ref[...] = jnp.zeros_like(acc2_ref)

    def step(ki, m1, m2, masked):
        start = pl.multiple_of(ki * t, t)
        k = k_ref[0, pl.ds(start, t), :]
        v = v_ref[0, pl.ds(start, t), :]
        cb = (((ki - qi) * t).astype(F32) + col_f) * slope_row

        def comp(qm, m, acc_ref):
            s = lax.dot_general(qm, k, (((1,), (1,)), ((), ())), preferred_element_type=F32) + cb
            if masked:
                s = jnp.where(col_i <= row_i, s, NEG_BIG)
            m_new = jnp.maximum(m, jnp.max(s, axis=-1, keepdims=True))
            p = jnp.exp2(s - m_new)
            alpha = jnp.exp2(m - m_new)
            acc_ref[...] = alpha * acc_ref[...] + jnp.dot(p.astype(BF16), v, preferred_element_type=F32)
            return m_new

        return comp(q1, m1, acc1_ref), comp(q2, m2, acc2_ref)

    m0 = jnp.full((t, 1), NEG_BIG, F32)
    m1, m2 = lax.fori_loop(0, qi, lambda ki, c: step(ki, c[0], c[1], False), (m0, m0))
    step(qi, m1, m2, True)

    lam_p = lam_ref[...]
    lam = (jnp.exp(jnp.sum(lam_p[0:1] * lam_p[1:2], axis=-1, keepdims=True))
           - jnp.exp(jnp.sum(lam_p[2:3] * lam_p[3:4], axis=-1, keepdims=True)) + LAM_INIT)
    a1 = acc1_ref[...]
    a2 = acc2_ref[...]
    o = (a1[:, :DA_V_DIM] / a1[:, DA_V_DIM:DA_V_DIM + 1]
         - lam * (a2[:, :DA_V_DIM] / a2[:, DA_V_DIM:DA_V_DIM + 1]))
    o_ref[0] = (_rms(o, sg_ref[...]) * (1.0 - LAM_INIT)).astype(BF16)


def _diff_attn(lam_p, subln_g, qk3, va3, t):
    b, s, _ = qk3.shape
    return pl.pallas_call(
        functools.partial(_attn_kernel, t=t),
        grid=(b, DA_HEADS, s // t),
        in_specs=[pl.BlockSpec((4, DA_HEAD_DIM), lambda bi, h, qi: (0, 0)),
                  pl.BlockSpec((1, DA_V_DIM), lambda bi, h, qi: (0, 0)),
                  pl.BlockSpec((1, t, LANES), lambda bi, h, qi: (bi, qi, h)),
                  pl.BlockSpec((1, s, LANES), lambda bi, h, qi: (bi, 0, DA_HEADS + h)),
                  pl.BlockSpec((1, s, 2 * LANES), lambda bi, h, qi: (bi, 0, h))],
        out_specs=pl.BlockSpec((1, t, DA_V_DIM), lambda bi, h, qi: (bi, qi, h)),
        out_shape=jax.ShapeDtypeStruct((b, s, DA_WIDTH), BF16),
        scratch_shapes=[pltpu.VMEM((t, 2 * LANES), F32), pltpu.VMEM((t, 2 * LANES), F32)],
        compiler_params=pltpu.CompilerParams(dimension_semantics=("arbitrary", "arbitrary", "arbitrary"),
                                             vmem_limit_bytes=VMEM_LIMIT),
        name="diff_attn",
    )(lam_p, subln_g, qk3, qk3, va3)


def _gdn_kernel(graw_ref, gab_ref, gz_ref, cw_ref, alog_ref, dtb_ref, ng_ref, og_ref,
                cbuf_ref, state_ref, o_ref, *, nb, r):
    i = pl.program_id(0)
    c = GDN_CHUNK
    nchunk = r // c
    halo = SUBLANES

    @pl.when(i == 0)
    def _():
        cbuf_ref[:, 0:halo, :] = jnp.zeros((nb, halo, 3 * GDN_WIDTH), F32)
        state_ref[...] = jnp.zeros_like(state_ref)

    cbuf_ref[:, halo:halo + r, :] = graw_ref[...]

    ri = lax.broadcasted_iota(jnp.int32, (r, r), 0)
    ci = lax.broadcasted_iota(jnp.int32, (r, r), 1)
    rx = ri ^ ci
    same_chunk = rx < c
    lower = ci <= ri
    strict = ci < ri
    eye = jnp.where(ci == ri, 1.0, 0.0).astype(F32)
    tri_ones = jnp.where(same_chunk, jnp.where(lower, 1.0, 0.0), 0.0).astype(F32)
    blk_ones = jnp.where(same_chunk, 1.0, 0.0).astype(F32)

    cw = cw_ref[...]
    per_head = []
    for b in range(nb):
        xc = cw[0:1] * cbuf_ref[b, halo - 3:halo - 3 + r, :]
        for j in range(1, GDN_CONV):
            xc = xc + cw[j:j + 1] * cbuf_ref[b, halo - 3 + j:halo - 3 + j + r, :]
        xc = _silu(xc)

        gab = gab_ref[b]
        sp_in = gab + dtb_ref[...]
        softplus = jnp.maximum(sp_in, 0.0) + jnp.log(1.0 + jnp.exp(-jnp.abs(sp_in)))
        g_all = -jnp.exp(alog_ref[...]) * softplus
        beta_all = 1.0 / (1.0 + jnp.exp(-gab))
        gc_all = _mm_f32(tri_ones, g_all)
        gl_all = _mm_f32(blk_ones, g_all)
        gc_t = jnp.transpose(gc_all)

        for h in range(GDN_HEADS):
            qh = xc[:, h * GDN_HEAD_DIM:(h + 1) * GDN_HEAD_DIM]
            kh = xc[:, GDN_WIDTH + h * GDN_HEAD_DIM:GDN_WIDTH + (h + 1) * GDN_HEAD_DIM]
            vh = xc[:, 2 * GDN_WIDTH + h * GDN_HEAD_DIM:2 * GDN_WIDTH + (h + 1) * GDN_HEAD_DIM]
            qh = qh * lax.rsqrt(jnp.sum(qh * qh, axis=-1, keepdims=True) + EPS) * (GDN_HEAD_DIM ** -0.5)
            kh = kh * lax.rsqrt(jnp.sum(kh * kh, axis=-1, keepdims=True) + EPS)
            beta = beta_all[:, GDN_HEADS + h:GDN_HEADS + h + 1]
            gc = gc_all[:, h:h + 1]
            gl = gl_all[:, h:h + 1]
            gc_row = gc_t[h:h + 1, :]

            decay = jnp.exp(jnp.where(same_chunk, jnp.where(lower, gc - gc_row, -jnp.inf), -jnp.inf))
            kb = kh * beta
            vb = vh * beta
            kbf = kh.astype(BF16)
            lmat = jnp.where(strict, _mm_nt(kb, kbf) * decay, 0.0)
            qk = _mm_nt(qh, kbf) * decay

            x1 = jnp.where(rx < 16, -lmat, 0.0)
            p = eye + x1
            x2 = _mm(x1, x1)
            p = p + _mm(x2, p)
            x4 = _mm(x2, x2)
            p = p + _mm(x4, p)
            x8 = _mm(x4, x4)
            p = p + _mm(x8, p)
            b32 = jnp.where(rx < 32, jnp.where(rx >= 16, lmat, 0.0), 0.0)
            p = p - _mm(p, _mm(b32, p))
            b64 = jnp.where(rx >= 32, lmat, 0.0)
            tmat = p - _mm(p, _mm(b64, p))

            eg = jnp.exp(gc)
            uw = _mm(tmat, jnp.concatenate([vb, kb * eg], axis=1))
            per_head.append(dict(
                u=uw[:, :GDN_HEAD_DIM], w=uw[:, GDN_HEAD_DIM:], qk=qk, qdec=qh * eg,
                kdec=kh * jnp.exp(gl - gc), egl=jnp.exp(gl)))

    for n in range(nchunk):
        rows = slice(n * c, (n + 1) * c)
        for idx, d in enumerate(per_head):
            b, h = divmod(idx, GDN_HEADS)
            st = state_ref[idx]
            ws_qs = _mm(jnp.concatenate([d["w"][rows], d["qdec"][rows]], axis=0), st)
            v_new = d["u"][rows] - ws_qs[:c]
            o = ws_qs[c:] + _mm(d["qk"][rows, n * c:(n + 1) * c], v_new)
            state_ref[idx] = st * d["egl"][n * c:n * c + 1, :] + _mm_tn(d["kdec"][rows], v_new)
            o_ref[b, rows, h * GDN_HEAD_DIM:(h + 1) * GDN_HEAD_DIM] = o

    for b in range(nb):
        for h in range(GDN_HEADS):
            cols = slice(h * GDN_HEAD_DIM, (h + 1) * GDN_HEAD_DIM)
            o = _rms(o_ref[b, :, cols], ng_ref[...])
            og_ref[b, :, cols] = (o * _silu(gz_ref[b, :, cols])).astype(BF16)

    cbuf_ref[:, 0:halo, :] = cbuf_ref[:, r:r + halo, :]


def _gdn(graw3, gab3, gz3, conv_w, alog_row, dtb_row, norm_g, r):
    nb, s, _ = graw3.shape
    blk = lambda i: (0, i, 0)
    fixed = lambda i: (0, 0)
    return pl.pallas_call(
        functools.partial(_gdn_kernel, nb=nb, r=r),
        grid=(s // r,),
        in_specs=[pl.BlockSpec((nb, r, 3 * GDN_WIDTH), blk),
                  pl.BlockSpec((nb, r, LANES), blk),
                  pl.BlockSpec((nb, r, GDN_WIDTH), blk),
                  pl.BlockSpec((GDN_CONV, 3 * GDN_WIDTH), fixed),
                  pl.BlockSpec((1, LANES), fixed),
                  pl.BlockSpec((1, LANES), fixed),
                  pl.BlockSpec((1, GDN_HEAD_DIM), fixed)],
        out_specs=pl.BlockSpec((nb, r, GDN_WIDTH), blk),
        out_shape=jax.ShapeDtypeStruct((nb, s, GDN_WIDTH), BF16),
        scratch_shapes=[pltpu.VMEM((nb, r + 2 * SUBLANES, 3 * GDN_WIDTH), F32),
                        pltpu.VMEM((nb * GDN_HEADS, GDN_HEAD_DIM, GDN_HEAD_DIM), F32),
                        pltpu.VMEM((nb, r, GDN_WIDTH), F32)],
        compiler_params=pltpu.CompilerParams(dimension_semantics=("arbitrary",),
                                             vmem_limit_bytes=VMEM_LIMIT),
        name="gdn",
    )(graw3, gab3, gz3, conv_w, alog_row, dtb_row, norm_g)


def _out_proj_kernel(x_ref, oa_ref, og_ref, wa_ref, wg_ref, g_ref, x1_ref, h2_ref):
    x1 = (x_ref[...] + jnp.dot(oa_ref[...], wa_ref[...], preferred_element_type=F32)
          + jnp.dot(og_ref[...], wg_ref[...], preferred_element_type=F32))
    x1_ref[...] = x1
    h2_ref[...] = _rms(x1, g_ref[...]).astype(BF16)


def _out_proj(x2, oa2, og2, wo_a, wo_g, g, tm):
    m = x2.shape[0]
    row = lambda i: (i, 0)
    fixed = lambda i: (0, 0)
    return pl.pallas_call(
        _out_proj_kernel,
        grid=(m // tm,),
        in_specs=[pl.BlockSpec((tm, D_MODEL), row),
                  pl.BlockSpec((tm, DA_WIDTH), row),
                  pl.BlockSpec((tm, GDN_WIDTH), row),
                  pl.BlockSpec((DA_WIDTH, D_MODEL), fixed),
                  pl.BlockSpec((GDN_WIDTH, D_MODEL), fixed),
                  pl.BlockSpec((1, D_MODEL), fixed)],
        out_specs=[pl.BlockSpec((tm, D_MODEL), row), pl.BlockSpec((tm, D_MODEL), row)],
        out_shape=[jax.ShapeDtypeStruct((m, D_MODEL), F32), jax.ShapeDtypeStruct((m, D_MODEL), BF16)],
        compiler_params=pltpu.CompilerParams(dimension_semantics=("arbitrary",),
                                             vmem_limit_bytes=VMEM_LIMIT),
        name="out_proj",
    )(x2, oa2, og2, wo_a, wo_g, g)


FFN_FC = 256


def _ffn_kernel(x1_ref, h2_ref, wup_ref, cw_ref, cb_ref, wd_ref, g_ref, out_ref,
                ubuf_ref, carry_ref, acc_ref, *, tm, tiles_per_seq):
    i = pl.program_id(0)
    halo = SUBLANES
    first = (i % tiles_per_seq) == 0
    h2 = h2_ref[...]

    @pl.when(i == 0)
    def _():
        carry_ref[...] = jnp.zeros_like(carry_ref)

    def conv_part(slot, lo):
        u = jnp.dot(h2, wup_ref[:, lo:lo + FFN_FC], preferred_element_type=F32)
        prev = carry_ref[:, lo:lo + FFN_FC]
        ubuf_ref[slot, 0:halo, :] = jnp.where(first, jnp.zeros_like(prev), prev)
        ubuf_ref[slot, halo:halo + tm, :] = u
        carry_ref[:, lo:lo + FFN_FC] = u[tm - halo:tm, :]
        cw = cw_ref[:, lo:lo + FFN_FC]
        return (cw[0:1] * ubuf_ref[slot, halo - 2:halo - 2 + tm, :]
                + cw[1:2] * ubuf_ref[slot, halo - 1:halo - 1 + tm, :]
                + cw[2:3] * u + cb_ref[:, lo:lo + FFN_FC])

    for cidx in range(D_FF // FFN_FC):
        gate = conv_part(2 * (cidx % 2), cidx * FFN_FC)
        val = conv_part(2 * (cidx % 2) + 1, D_FF + cidx * FFN_FC)
        act = (_silu(gate) * val).astype(BF16)
        contrib = jnp.dot(act, wd_ref[cidx * FFN_FC:(cidx + 1) * FFN_FC, :], preferred_element_type=F32)
        if cidx == 0:
            acc_ref[...] = contrib
        else:
            acc_ref[...] += contrib

    out_ref[...] = _rms(x1_ref[...] + acc_ref[...], g_ref[...])


def _ffn(x1, h2, w_up, conv_w, conv_b, w_down, g, tm, tiles_per_seq):
    m = x1.shape[0]
    row = lambda i: (i, 0)
    fixed = lambda i: (0, 0)
    once = pl.Buffered(1)
    return pl.pallas_call(
        functools.partial(_ffn_kernel, tm=tm, tiles_per_seq=tiles_per_seq),
        grid=(m // tm,),
        in_specs=[pl.BlockSpec((tm, D_MODEL), row),
                  pl.BlockSpec((tm, D_MODEL), row),
                  pl.BlockSpec((D_MODEL, 2 * D_FF), fixed, pipeline_mode=once),
                  pl.BlockSpec((FFN_CONV, 2 * D_FF), fixed),
                  pl.BlockSpec((1, 2 * D_FF), fixed),
                  pl.BlockSpec((D_FF, D_MODEL), fixed, pipeline_mode=once),
                  pl.BlockSpec((1, D_MODEL), fixed)],
        out_specs=pl.BlockSpec((tm, D_MODEL), row),
        out_shape=jax.ShapeDtypeStruct((m, D_MODEL), F32),
        scratch_shapes=[pltpu.VMEM((4, tm + SUBLANES, FFN_FC), F32),
                        pltpu.VMEM((SUBLANES, 2 * D_FF), F32),
                        pltpu.VMEM((tm, D_MODEL), F32)],
        compiler_params=pltpu.CompilerParams(dimension_semantics=("arbitrary",),
                                             vmem_limit_bytes=VMEM_LIMIT),
        name="ffn",
    )(x1, h2, w_up, conv_w, conv_b, w_down, g)


def _pad_lanes(v):
    return jnp.zeros((1, LANES), F32).at[0, :v.shape[0]].set(v.astype(F32))


def kernel(x, attn_norm_g, w_in, da_lambda_q1, da_lambda_k1, da_lambda_q2, da_lambda_k2, da_subln_g,
           gdn_conv_w, gdn_a_log, gdn_dt_bias, gdn_norm_g, w_out, ffn_norm_g, w_up, ffn_conv_w,
           ffn_conv_b, w_down, final_norm_g):
    b, s, d = x.shape
    m = b * s
    tm = 512
    assert d == D_MODEL and s % tm == 0 and w_in.shape[0] == 1, (x.shape, w_in.shape)
    l = 0
    x2 = x.reshape(m, d)

    n_main = 3 * DA_WIDTH + 3 * GDN_WIDTH
    wi = w_in[l]
    w1 = jnp.concatenate(
        [wi[:, :n_main], wi[:, n_main + 2 * GDN_HEADS:], wi[:, n_main:n_main + 2 * GDN_HEADS],
         jnp.zeros((d, LANES - 2 * GDN_HEADS), wi.dtype)], axis=1).astype(BF16)
    lam_p = jnp.stack([da_lambda_q1[l], da_lambda_k1[l], da_lambda_q2[l], da_lambda_k2[l]]).astype(F32)

    qk, va, graw, gz, gab = _in_proj(x2, attn_norm_g[l].reshape(1, d).astype(F32), w1, tm)

    oa = _diff_attn(lam_p, da_subln_g[l].reshape(1, DA_V_DIM).astype(F32),
                    qk.reshape(b, s, 2 * DA_WIDTH), va.reshape(b, s, 2 * DA_WIDTH), 512)

    og = _gdn(graw.reshape(b, s, 3 * GDN_WIDTH), gab.reshape(b, s, LANES), gz.reshape(b, s, GDN_WIDTH),
              gdn_conv_w[l].astype(F32), _pad_lanes(gdn_a_log[l]),
              _pad_lanes(gdn_dt_bias[l]), gdn_norm_g[l].reshape(1, GDN_HEAD_DIM).astype(F32), 256)

    wo = w_out[l].astype(BF16)
    x1, h2 = _out_proj(x2, oa.reshape(m, DA_WIDTH), og.reshape(m, GDN_WIDTH), wo[:DA_WIDTH], wo[DA_WIDTH:],
                       ffn_norm_g[l].reshape(1, d).astype(F32), tm)

    out = _ffn(x1, h2, w_up[l].astype(BF16), ffn_conv_w[l].astype(F32),
               ffn_conv_b[l].reshape(1, 2 * D_FF).astype(F32), w_down[l].astype(BF16),
               final_norm_g.reshape(1, d).astype(F32), tm, s // tm)
    return out.reshape(b, s, d)
```

```python
import functools
import math

import jax
import jax.numpy as jnp
from jax import lax
from jax.experimental import pallas as pl
from jax.experimental.pallas import tpu as pltpu

F32 = jnp.float32
BF16 = jnp.bfloat16

EPS = 1e-6
LOG2E = 1.4426950408889634
NEG_BIG = -1e30

D_MODEL = 1024
DA_HEADS = 4
DA_HEAD_DIM = 64
DA_V_DIM = 128
DA_WIDTH = DA_HEADS * DA_V_DIM
GDN_HEADS = 4
GDN_HEAD_DIM = 128
GDN_WIDTH = GDN_HEADS * GDN_HEAD_DIM
GDN_CONV = 4
GDN_CHUNK = 64
D_FF = 2816
FFN_CONV = 3
LAM_INIT = 0.8 - 0.6 * math.exp(-0.3 * 0)

LANES = 128
SUBLANES = 8
VMEM_LIMIT = 56 * 1024 * 1024

W1_COLS = 3 * DA_WIDTH + 3 * GDN_WIDTH + GDN_WIDTH + LANES


def _mm(a, b):
    return jnp.dot(a.astype(BF16), b.astype(BF16), preferred_element_type=F32)


def _mm_nt(a, b):
    return lax.dot_general(a.astype(BF16), b.astype(BF16), (((1,), (1,)), ((), ())),
                           preferred_element_type=F32)


def _mm_tn(a, b):
    return lax.dot_general(a.astype(BF16), b.astype(BF16), (((0,), (0,)), ((), ())),
                           preferred_element_type=F32)


def _mm_f32(a, b):
    return jnp.dot(a, b, preferred_element_type=F32, precision=lax.Precision.HIGHEST)


def _silu(x):
    return x * (1.0 / (1.0 + jnp.exp(-x)))


def _rms(x, g):
    return x * lax.rsqrt(jnp.mean(x * x, axis=-1, keepdims=True) + EPS) * g


def _in_proj_kernel(x_ref, g_ref, w_ref, qk_ref, va_ref, graw_ref, gz_ref, gab_ref):
    hb = _rms(x_ref[...], g_ref[...]).astype(BF16)

    def proj(lo, width):
        return jnp.dot(hb, w_ref[:, lo:lo + width], preferred_element_type=F32)

    qk_ref[:, 0:DA_WIDTH] = (proj(0, DA_WIDTH) * (DA_HEAD_DIM ** -0.5 * LOG2E)).astype(BF16)
    qk_ref[:, DA_WIDTH:2 * DA_WIDTH] = proj(DA_WIDTH, DA_WIDTH).astype(BF16)
    v = proj(2 * DA_WIDTH, DA_WIDTH)
    tm = v.shape[0]
    ones_col = jnp.where(lax.broadcasted_iota(jnp.int32, (tm, LANES), 1) == 0, 1.0, 0.0).astype(BF16)
    for h in range(DA_HEADS):
        va_ref[:, 2 * h * LANES:(2 * h + 1) * LANES] = v[:, h * DA_V_DIM:(h + 1) * DA_V_DIM].astype(BF16)
        va_ref[:, (2 * h + 1) * LANES:(2 * h + 2) * LANES] = ones_col
    base = 3 * DA_WIDTH
    for c in range(3):
        graw_ref[:, c * GDN_WIDTH:(c + 1) * GDN_WIDTH] = proj(base + c * GDN_WIDTH, GDN_WIDTH)
    gz_ref[...] = proj(base + 3 * GDN_WIDTH, GDN_WIDTH)
    gab_ref[...] = proj(base + 4 * GDN_WIDTH, LANES)


def _in_proj(x2, g, w1, tm):
    m = x2.shape[0]
    row = lambda i: (i, 0)
    fixed = lambda i: (0, 0)
    return pl.pallas_call(
        _in_proj_kernel,
        grid=(m // tm,),
        in_specs=[pl.BlockSpec((tm, D_MODEL), row),
                  pl.BlockSpec((1, D_MODEL), fixed),
                  pl.BlockSpec((D_MODEL, W1_COLS), fixed)],
        out_specs=[pl.BlockSpec((tm, 2 * DA_WIDTH), row),
                   pl.BlockSpec((tm, 2 * DA_WIDTH), row),
                   pl.BlockSpec((tm, 3 * GDN_WIDTH), row),
                   pl.BlockSpec((tm, GDN_WIDTH), row),
                   pl.BlockSpec((tm, LANES), row)],
        out_shape=[jax.ShapeDtypeStruct((m, 2 * DA_WIDTH), BF16),
                   jax.ShapeDtypeStruct((m, 2 * DA_WIDTH), BF16),
                   jax.ShapeDtypeStruct((m, 3 * GDN_WIDTH), F32),
                   jax.ShapeDtypeStruct((m, GDN_WIDTH), F32),
                   jax.ShapeDtypeStruct((m, LANES), F32)],
        compiler_params=pltpu.CompilerParams(dimension_semantics=("arbitrary",),
                                             vmem_limit_bytes=VMEM_LIMIT),
        name="in_proj",
    )(x2, g, w1)


ATTN_HEADS_PER_STEP = 4


def _attn_kernel(lam_ref, sg_ref, q_ref, k_ref, v_ref, o_ref, acc_ref, *, t, hp):
    hg = pl.program_id(1)
    qi = pl.program_id(2)

    lane = lax.broadcasted_iota(jnp.int32, (t, LANES), 1)
    col_f = lax.broadcasted_iota(jnp.int32, (1, t), 1).astype(F32)
    row_i = lax.broadcasted_iota(jnp.int32, (t, t), 0)
    col_i = lax.broadcasted_iota(jnp.int32, (t, t), 1)

    qm, slope_row = [], []
    for j in range(hp):
        q = q_ref[0, :, j * LANES:(j + 1) * LANES]
        zero = jnp.zeros_like(q)
        qm += [jnp.where(lane < DA_HEAD_DIM, q, zero), jnp.where(lane >= DA_HEAD_DIM, q, zero)]
        hf = (hg * hp + j + 1).astype(F32)
        slope_row.append(jnp.exp2(jnp.full((1, t), -8.0 / DA_HEADS, F32) * hf) * LOG2E)

    acc_ref[...] = jnp.zeros_like(acc_ref)

    def step(ki, ms, masked):
        start = pl.multiple_of(ki * t, t)
        pos = ((ki - qi) * t).astype(F32) + col_f
        new_ms = []
        for j in range(hp):
            k = k_ref[0, pl.ds(start, t), j * LANES:(j + 1) * LANES]
            v = v_ref[0, pl.ds(start, t), 2 * j * LANES:2 * (j + 1) * LANES]
            cb = pos * slope_row[j]
            for c in range(2):
                idx = 2 * j + c
                s = lax.dot_general(qm[idx], k, (((1,), (1,)), ((), ())), preferred_element_type=F32) + cb
                if masked:
                    s = jnp.where(col_i <= row_i, s, NEG_BIG)
                m_new = jnp.maximum(ms[idx], jnp.max(s, axis=-1, keepdims=True))
                p = jnp.exp2(s - m_new)
                acc_ref[idx] = (jnp.exp2(ms[idx] - m_new) * acc_ref[idx]
                                + jnp.dot(p.astype(BF16), v, preferred_element_type=F32))
                new_ms.append(m_new)
        return tuple(new_ms)

    m0 = tuple(jnp.full((t, 1), NEG_BIG, F32) for _ in range(2 * hp))
    ms = lax.fori_loop(0, qi, lambda ki, c: step(ki, c, False), m0)
    lax.fori_loop(qi, qi + 1, lambda ki, c: step(ki, c, True), ms)

    lam_p = lam_ref[...]
    lam = (jnp.exp(jnp.sum(lam_p[0:1] * lam_p[1:2], axis=-1, keepdims=True))
           - jnp.exp(jnp.sum(lam_p[2:3] * lam_p[3:4], axis=-1, keepdims=True)) + LAM_INIT)
    for j in range(hp):
        a1 = acc_ref[2 * j]
        a2 = acc_ref[2 * j + 1]
        o = (a1[:, :DA_V_DIM] / a1[:, DA_V_DIM:DA_V_DIM + 1]
             - lam * (a2[:, :DA_V_DIM] / a2[:, DA_V_DIM:DA_V_DIM + 1]))
        o_ref[0, :, j * DA_V_DIM:(j + 1) * DA_V_DIM] = (_rms(o, sg_ref[...]) * (1.0 - LAM_INIT)).astype(BF16)


def _diff_attn(lam_p, subln_g, qk3, va3, t):
    b, s, _ = qk3.shape
    hp = ATTN_HEADS_PER_STEP
    ngroups = DA_HEADS // hp
    return pl.pallas_call(
        functools.partial(_attn_kernel, t=t, hp=hp),
        grid=(b, ngroups, s // t),
        in_specs=[pl.BlockSpec((4, DA_HEAD_DIM), lambda bi, hg, qi: (0, 0)),
                  pl.BlockSpec((1, DA_V_DIM), lambda bi, hg, qi: (0, 0)),
                  pl.BlockSpec((1, t, hp * LANES), lambda bi, hg, qi: (bi, qi, hg)),
                  pl.BlockSpec((1, s, hp * LANES), lambda bi, hg, qi: (bi, 0, ngroups + hg),
                               pipeline_mode=pl.Buffered(1)),
                  pl.BlockSpec((1, s, 2 * hp * LANES), lambda bi, hg, qi: (bi, 0, hg),
                               pipeline_mode=pl.Buffered(1))],
        out_specs=pl.BlockSpec((1, t, hp * DA_V_DIM), lambda bi, hg, qi: (bi, qi, hg)),
        out_shape=jax.ShapeDtypeStruct((b, s, DA_WIDTH), BF16),
        scratch_shapes=[pltpu.VMEM((2 * hp, t, 2 * LANES), F32)],
        compiler_params=pltpu.CompilerParams(dimension_semantics=("arbitrary", "arbitrary", "arbitrary"),
                                             vmem_limit_bytes=VMEM_LIMIT),
        name="diff_attn",
    )(lam_p, subln_g, qk3, qk3, va3)


def _gdn_kernel(graw_ref, gab_ref, gz_ref, cw_ref, alog_ref, dtb_ref, ng_ref, og_ref,
                cbuf_ref, state_ref, o_ref, *, nb, r):
    i = pl.program_id(0)
    c = GDN_CHUNK
    nchunk = r // c
    halo = SUBLANES

    @pl.when(i == 0)
    def _():
        cbuf_ref[:, 0:halo, :] = jnp.zeros((nb, halo, 3 * GDN_WIDTH), F32)
        state_ref[...] = jnp.zeros_like(state_ref)

    cbuf_ref[:, halo:halo + r, :] = graw_ref[...]

    ri = lax.broadcasted_iota(jnp.int32, (r, r), 0)
    ci = lax.broadcasted_iota(jnp.int32, (r, r), 1)
    rx = ri ^ ci
    same_chunk = rx < c
    lower = ci <= ri
    strict = ci < ri
    eye = jnp.where(ci == ri, 1.0, 0.0).astype(F32)
    tri_ones = jnp.where(same_chunk, jnp.where(lower, 1.0, 0.0), 0.0).astype(F32)
    blk_ones = jnp.where(same_chunk, 1.0, 0.0).astype(F32)

    cw = cw_ref[...]
    per_head = []
    for b in range(nb):
        xc = cw[0:1] * cbuf_ref[b, halo - 3:halo - 3 + r, :]
        for j in range(1, GDN_CONV):
            xc = xc + cw[j:j + 1] * cbuf_ref[b, halo - 3 + j:halo - 3 + j + r, :]
        xc = _silu(xc)

        gab = gab_ref[b]
        sp_in = gab + dtb_ref[...]
        softplus = jnp.maximum(sp_in, 0.0) + jnp.log(1.0 + jnp.exp(-jnp.abs(sp_in)))
        g_all = -jnp.exp(alog_ref[...]) * softplus
        beta_all = 1.0 / (1.0 + jnp.exp(-gab))
        gc_all = _mm_f32(tri_ones, g_all)
        gl_all = _mm_f32(blk_ones, g_all)
        gc_t = jnp.transpose(gc_all)

        for h in range(GDN_HEADS):
            qh = xc[:, h * GDN_HEAD_DIM:(h + 1) * GDN_HEAD_DIM]
            kh = xc[:, GDN_WIDTH + h * GDN_HEAD_DIM:GDN_WIDTH + (h + 1) * GDN_HEAD_DIM]
            vh = xc[:, 2 * GDN_WIDTH + h * GDN_HEAD_DIM:2 * GDN_WIDTH + (h + 1) * GDN_HEAD_DIM]
            qh = qh * lax.rsqrt(jnp.sum(qh * qh, axis=-1, keepdims=True) + EPS) * (GDN_HEAD_DIM ** -0.5)
            kh = kh * lax.rsqrt(jnp.sum(kh * kh, axis=-1, keepdims=True) + EPS)
            beta = beta_all[:, GDN_HEADS + h:GDN_HEADS + h + 1]
            gc = gc_all[:, h:h + 1]
            gl = gl_all[:, h:h + 1]
            gc_row = gc_t[h:h + 1, :]

            decay = jnp.exp(jnp.where(same_chunk, jnp.where(lower, gc - gc_row, -jnp.inf), -jnp.inf))
            kb = kh * beta
            vb = vh * beta
            kbf = kh.astype(BF16)
            lmat = jnp.where(strict, _mm_nt(kb, kbf) * decay, 0.0)
            qk = _mm_nt(qh, kbf) * decay

            x1 = jnp.where(rx < 16, -lmat, 0.0)
            p = eye + x1
            x2 = _mm(x1, x1)
            p = p + _mm(x2, p)
            x4 = _mm(x2, x2)
            p = p + _mm(x4, p)
            x8 = _mm(x4, x4)
            p = p + _mm(x8, p)
            b32 = jnp.where(rx < 32, jnp.where(rx >= 16, lmat, 0.0), 0.0)
            p = p - _mm(p, _mm(b32, p))
            b64 = jnp.where(rx >= 32, lmat, 0.0)
            tmat = p - _mm(p, _mm(b64, p))

            eg = jnp.exp(gc)
            uw = _mm(tmat, jnp.concatenate([vb, kb * eg], axis=1))
            per_head.append(dict(
                u=uw[:, :GDN_HEAD_DIM], w=uw[:, GDN_HEAD_DIM:], qk=qk, qdec=qh * eg,
                kdec=kh * jnp.exp(gl - gc), egl=jnp.exp(gl)))

    for n in range(nchunk):
        rows = slice(n * c, (n + 1) * c)
        for idx, d in enumerate(per_head):
            b, h = divmod(idx, GDN_HEADS)
            st = state_ref[idx]
            ws_qs = _mm(jnp.concatenate([d["w"][rows], d["qdec"][rows]], axis=0), st)
            v_new = d["u"][rows] - ws_qs[:c]
            o = ws_qs[c:] + _mm(d["qk"][rows, n * c:(n + 1) * c], v_new)
            state_ref[idx] = st * d["egl"][n * c:n * c + 1, :] + _mm_tn(d["kdec"][rows], v_new)
            o_ref[b, rows, h * GDN_HEAD_DIM:(h + 1) * GDN_HEAD_DIM] = o

    for b in range(nb):
        for h in range(GDN_HEADS):
            cols = slice(h * GDN_HEAD_DIM, (h + 1) * GDN_HEAD_DIM)
            o = _rms(o_ref[b, :, cols], ng_ref[...])
            og_ref[b, :, cols] = (o * _silu(gz_ref[b, :, cols])).astype(BF16)

    cbuf_ref[:, 0:halo, :] = cbuf_ref[:, r:r + halo, :]


def _gdn(graw3, gab3, gz3, conv_w, alog_row, dtb_row, norm_g, r):
    nb, s, _ = graw3.shape
    blk = lambda i: (0, i, 0)
    fixed = lambda i: (0, 0)
    return pl.pallas_call(
        functools.partial(_gdn_kernel, nb=nb, r=r),
        grid=(s // r,),
        in_specs=[pl.BlockSpec((nb, r, 3 * GDN_WIDTH), blk),
                  pl.BlockSpec((nb, r, LANES), blk),
                  pl.BlockSpec((nb, r, GDN_WIDTH), blk),
                  pl.BlockSpec((GDN_CONV, 3 * GDN_WIDTH), fixed),
                  pl.BlockSpec((1, LANES), fixed),
                  pl.BlockSpec((1, LANES), fixed),
                  pl.BlockSpec((1, GDN_HEAD_DIM), fixed)],
        out_specs=pl.BlockSpec((nb, r, GDN_WIDTH), blk),
        out_shape=jax.ShapeDtypeStruct((nb, s, GDN_WIDTH), BF16),
        scratch_shapes=[pltpu.VMEM((nb, r + 2 * SUBLANES, 3 * GDN_WIDTH), F32),
                        pltpu.VMEM((nb * GDN_HEADS, GDN_HEAD_DIM, GDN_HEAD_DIM), F32),
                        pltpu.VMEM((nb, r, GDN_WIDTH), F32)],
        compiler_params=pltpu.CompilerParams(dimension_semantics=("arbitrary",),
                                             vmem_limit_bytes=VMEM_LIMIT),
        name="gdn",
    )(graw3, gab3, gz3, conv_w, alog_row, dtb_row, norm_g)


def _out_proj_kernel(x_ref, oa_ref, og_ref, wa_ref, wg_ref, g_ref, x1_ref, h2_ref):
    x1 = (x_ref[...] + jnp.dot(oa_ref[...], wa_ref[...], preferred_element_type=F32)
          + jnp.dot(og_ref[...], wg_ref[...], preferred_element_type=F32))
    x1_ref[...] = x1
    h2_ref[...] = _rms(x1, g_ref[...]).astype(BF16)


def _out_proj(x2, oa2, og2, wo_a, wo_g, g, tm):
    m = x2.shape[0]
    row = lambda i: (i, 0)
    fixed = lambda i: (0, 0)
    return pl.pallas_call(
        _out_proj_kernel,
        grid=(m // tm,),
        in_specs=[pl.BlockSpec((tm, D_MODEL), row),
                  pl.BlockSpec((tm, DA_WIDTH), row),
                  pl.BlockSpec((tm, GDN_WIDTH), row),
                  pl.BlockSpec((DA_WIDTH, D_MODEL), fixed),
                  pl.BlockSpec((GDN_WIDTH, D_MODEL), fixed),
                  pl.BlockSpec((1, D_MODEL), fixed)],
        out_specs=[pl.BlockSpec((tm, D_MODEL), row), pl.BlockSpec((tm, D_MODEL), row)],
        out_shape=[jax.ShapeDtypeStruct((m, D_MODEL), F32), jax.ShapeDtypeStruct((m, D_MODEL), BF16)],
        compiler_params=pltpu.CompilerParams(dimension_semantics=("arbitrary",),
                                             vmem_limit_bytes=VMEM_LIMIT),
        name="out_proj",
    )(x2, oa2, og2, wo_a, wo_g, g)


FFN_FC = 256


def _ffn_kernel(x1_ref, h2_ref, wup_ref, cw_ref, cb_ref, wd_ref, g_ref, out_ref,
                ubuf_ref, carry_ref, acc_ref, *, tm, tiles_per_seq):
    i = pl.program_id(0)
    halo = SUBLANES
    first = (i % tiles_per_seq) == 0
    h2 = h2_ref[...]

    @pl.when(i == 0)
    def _():
        carry_ref[...] = jnp.zeros_like(carry_ref)

    def conv_part(slot, lo):
        u = jnp.dot(h2, wup_ref[:, lo:lo + FFN_FC], preferred_element_type=F32)
        prev = carry_ref[:, lo:lo + FFN_FC]
        ubuf_ref[slot, 0:halo, :] = jnp.where(first, jnp.zeros_like(prev), prev)
        ubuf_ref[slot, halo:halo + tm, :] = u
        carry_ref[:, lo:lo + FFN_FC] = u[tm - halo:tm, :]
        cw = cw_ref[:, lo:lo + FFN_FC]
        return (cw[0:1] * ubuf_ref[slot, halo - 2:halo - 2 + tm, :]
                + cw[1:2] * ubuf_ref[slot, halo - 1:halo - 1 + tm, :]
                + cw[2:3] * u + cb_ref[:, lo:lo + FFN_FC])

    for cidx in range(D_FF // FFN_FC):
        gate = conv_part(2 * (cidx % 2), cidx * FFN_FC)
        val = conv_part(2 * (cidx % 2) + 1, D_FF + cidx * FFN_FC)
        act = (_silu(gate) * val).astype(BF16)
        contrib = jnp.dot(act, wd_ref[cidx * FFN_FC:(cidx + 1) * FFN_FC, :], preferred_element_type=F32)
        if cidx == 0:
            acc_ref[...] = contrib
        else:
            acc_ref[...] += contrib

    out_ref[...] = _rms(x1_ref[...] + acc_ref[...], g_ref[...])


def _ffn(x1, h2, w_up, conv_w, conv_b, w_down, g, tm, tiles_per_seq):
    m = x1.shape[0]
    row = lambda i: (i, 0)
    fixed = lambda i: (0, 0)
    once = pl.Buffered(1)
    return pl.pallas_call(
        functools.partial(_ffn_kernel, tm=tm, tiles_per_seq=tiles_per_seq),
        grid=(m // tm,),
        in_specs=[pl.BlockSpec((tm, D_MODEL), row),
                  pl.BlockSpec((tm, D_MODEL), row),
                  pl.BlockSpec((D_MODEL, 2 * D_FF), fixed, pipeline_mode=once),
                  pl.BlockSpec((FFN_CONV, 2 * D_FF), fixed),
                  pl.BlockSpec((1, 2 * D_FF), fixed),
                  pl.BlockSpec((D_FF, D_MODEL), fixed, pipeline_mode=once),
                  pl.BlockSpec((1, D_MODEL), fixed)],
        out_specs=pl.BlockSpec((tm, D_MODEL), row),
        out_shape=jax.ShapeDtypeStruct((m, D_MODEL), F32),
        scratch_shapes=[pltpu.VMEM((4, tm + SUBLANES, FFN_FC), F32),
                        pltpu.VMEM((SUBLANES, 2 * D_FF), F32),
                        pltpu.VMEM((tm, D_MODEL), F32)],
        compiler_params=pltpu.CompilerParams(dimension_semantics=("arbitrary",),
                                             vmem_limit_bytes=VMEM_LIMIT),
        name="ffn",
    )(x1, h2, w_up, conv_w, conv_b, w_down, g)


def _pad_lanes(v):
    return jnp.zeros((1, LANES), F32).at[0, :v.shape[0]].set(v.astype(F32))


def kernel(x, attn_norm_g, w_in, da_lambda_q1, da_lambda_k1, da_lambda_q2, da_lambda_k2, da_subln_g,
           gdn_conv_w, gdn_a_log, gdn_dt_bias, gdn_norm_g, w_out, ffn_norm_g, w_up, ffn_conv_w,
           ffn_conv_b, w_down, final_norm_g):
    b, s, d = x.shape
    m = b * s
    tm = 512
    assert d == D_MODEL and s % tm == 0 and w_in.shape[0] == 1, (x.shape, w_in.shape)
    l = 0
    x2 = x.reshape(m, d)

    n_main = 3 * DA_WIDTH + 3 * GDN_WIDTH
    wi = w_in[l]
    w1 = jnp.concatenate(
        [wi[:, :n_main], wi[:, n_main + 2 * GDN_HEADS:], wi[:, n_main:n_main + 2 * GDN_HEADS],
         jnp.zeros((d, LANES - 2 * GDN_HEADS), wi.dtype)], axis=1).astype(BF16)
    lam_p = jnp.stack([da_lambda_q1[l], da_lambda_k1[l], da_lambda_q2[l], da_lambda_k2[l]]).astype(F32)

    qk, va, graw, gz, gab = _in_proj(x2, attn_norm_g[l].reshape(1, d).astype(F32), w1, tm)

    oa = _diff_attn(lam_p, da_subln_g[l].reshape(1, DA_V_DIM).astype(F32),
                    qk.reshape(b, s, 2 * DA_WIDTH), va.reshape(b, s, 2 * DA_WIDTH), 512)

    og = _gdn(graw.reshape(b, s, 3 * GDN_WIDTH), gab.reshape(b, s, LANES), gz.reshape(b, s, GDN_WIDTH),
              gdn_conv_w[l].astype(F32), _pad_lanes(gdn_a_log[l]),
              _pad_lanes(gdn_dt_bias[l]), gdn_norm_g[l].reshape(1, GDN_HEAD_DIM).astype(F32), 256)

    wo = w_out[l].astype(BF16)
    x1, h2 = _out_proj(x2, oa.reshape(m, DA_WIDTH), og.reshape(m, GDN_WIDTH), wo[:DA_WIDTH], wo[DA_WIDTH:],
                       ffn_norm_g[l].reshape(1, d).astype(F32), tm)

    out = _ffn(x1, h2, w_up[l].astype(BF16), ffn_conv_w[l].astype(F32),
               ffn_conv_b[l].reshape(1, 2 * D_FF).astype(F32), w_down[l].astype(BF16),
               final_norm_g.reshape(1, d).astype(F32), tm, s // tm)
    return out.reshape(b, s, d)
```

```python
import functools
import math

import jax
import jax.numpy as jnp
from jax import lax
from jax.experimental import pallas as pl
from jax.experimental.pallas import tpu as pltpu

F32 = jnp.float32
BF16 = jnp.bfloat16

EPS = 1e-6
LOG2E = 1.4426950408889634
NEG_BIG = -1e30

D_MODEL = 1024
DA_HEADS = 4
DA_HEAD_DIM = 64
DA_V_DIM = 128
DA_WIDTH = DA_HEADS * DA_V_DIM
GDN_HEADS = 4
GDN_HEAD_DIM = 128
GDN_WIDTH = GDN_HEADS * GDN_HEAD_DIM
GDN_CONV = 4
GDN_CHUNK = 64
D_FF = 2816
FFN_CONV = 3
LAM_INIT = 0.8 - 0.6 * math.exp(-0.3 * 0)

LANES = 128
SUBLANES = 8
VMEM_LIMIT = 56 * 1024 * 1024

W1_COLS = 3 * DA_WIDTH + 3 * GDN_WIDTH + GDN_WIDTH + LANES


def _mm(a, b):
    return jnp.dot(a.astype(BF16), b.astype(BF16), preferred_element_type=F32)


def _mm_nt(a, b):
    return lax.dot_general(a.astype(BF16), b.astype(BF16), (((1,), (1,)), ((), ())),
                           preferred_element_type=F32)


def _silu(x):
    return x * (1.0 / (1.0 + jnp.exp(-x)))


def _rms(x, g):
    return x * lax.rsqrt(jnp.mean(x * x, axis=-1, keepdims=True) + EPS) * g


def _in_proj_kernel(x_ref, g_ref, w_ref, cw_ref, alog_ref, dtb_ref,
                    qk_ref, va_ref, gqkv_ref, gz_ref, gate_ref, cbuf_ref, *, tm, tiles_per_seq):
    i = pl.program_id(0)
    halo = SUBLANES
    hb = _rms(x_ref[...], g_ref[...]).astype(BF16)

    def proj(lo, width):
        return jnp.dot(hb, w_ref[:, lo:lo + width], preferred_element_type=F32)

    qk_ref[:, 0:DA_WIDTH] = (proj(0, DA_WIDTH) * (DA_HEAD_DIM ** -0.5 * LOG2E)).astype(BF16)
    qk_ref[:, DA_WIDTH:2 * DA_WIDTH] = proj(DA_WIDTH, DA_WIDTH).astype(BF16)
    v = proj(2 * DA_WIDTH, DA_WIDTH)
    lane = lax.broadcasted_iota(jnp.int32, (tm, LANES), 1)
    ones_col = jnp.where(lane == 0, 1.0, 0.0).astype(BF16)
    for h in range(DA_HEADS):
        va_ref[:, 2 * h * LANES:(2 * h + 1) * LANES] = v[:, h * DA_V_DIM:(h + 1) * DA_V_DIM].astype(BF16)
        va_ref[:, (2 * h + 1) * LANES:(2 * h + 2) * LANES] = ones_col
    base = 3 * DA_WIDTH
    gz_ref[...] = proj(base + 3 * GDN_WIDTH, GDN_WIDTH)

    gab = proj(base + 4 * GDN_WIDTH, LANES)
    sp_in = gab + dtb_ref[...]
    softplus = jnp.maximum(sp_in, 0.0) + jnp.log(1.0 + jnp.exp(-jnp.abs(sp_in)))
    gate_ref[...] = jnp.where(lane < GDN_HEADS, -jnp.exp(alog_ref[...]) * softplus, 1.0 / (1.0 + jnp.exp(-gab)))

    @pl.when(i % tiles_per_seq == 0)
    def _():
        cbuf_ref[0:halo, :] = jnp.zeros((halo, 3 * GDN_WIDTH), F32)

    for c in range(3):
        cbuf_ref[halo:halo + tm, c * GDN_WIDTH:(c + 1) * GDN_WIDTH] = proj(base + c * GDN_WIDTH, GDN_WIDTH)
    for c in range(3 * GDN_HEADS):
        cols = slice(c * GDN_HEAD_DIM, (c + 1) * GDN_HEAD_DIM)
        cw = cw_ref[:, cols]
        xc = cw[0:1] * cbuf_ref[halo - 3:halo - 3 + tm, cols]
        for j in range(1, GDN_CONV):
            xc = xc + cw[j:j + 1] * cbuf_ref[halo - 3 + j:halo - 3 + j + tm, cols]
        xc = _silu(xc)
        if c < GDN_HEADS:
            xc = xc * (lax.rsqrt(jnp.sum(xc * xc, axis=-1, keepdims=True) + EPS) * GDN_HEAD_DIM ** -0.5)
        elif c < 2 * GDN_HEADS:
            xc = xc * lax.rsqrt(jnp.sum(xc * xc, axis=-1, keepdims=True) + EPS)
        gqkv_ref[:, cols] = xc
    cbuf_ref[0:halo, :] = cbuf_ref[tm:tm + halo, :]


def _in_proj(x2, g, w1, conv_w, alog_row, dtb_row, tm, tiles_per_seq):
    m = x2.shape[0]
    row = lambda i: (i, 0)
    fixed = lambda i: (0, 0)
    return pl.pallas_call(
        functools.partial(_in_proj_kernel, tm=tm, tiles_per_seq=tiles_per_seq),
        grid=(m // tm,),
        in_specs=[pl.BlockSpec((tm, D_MODEL), row),
                  pl.BlockSpec((1, D_MODEL), fixed),
                  pl.BlockSpec((D_MODEL, W1_COLS), fixed),
                  pl.BlockSpec((GDN_CONV, 3 * GDN_WIDTH), fixed),
                  pl.BlockSpec((1, LANES), fixed),
                  pl.BlockSpec((1, LANES), fixed)],
        out_specs=[pl.BlockSpec((tm, 2 * DA_WIDTH), row),
                   pl.BlockSpec((tm, 2 * DA_WIDTH), row),
                   pl.BlockSpec((tm, 3 * GDN_WIDTH), row),
                   pl.BlockSpec((tm, GDN_WIDTH), row),
                   pl.BlockSpec((tm, LANES), row)],
        out_shape=[jax.ShapeDtypeStruct((m, 2 * DA_WIDTH), BF16),
                   jax.ShapeDtypeStruct((m, 2 * DA_WIDTH), BF16),
                   jax.ShapeDtypeStruct((m, 3 * GDN_WIDTH), F32),
                   jax.ShapeDtypeStruct((m, GDN_WIDTH), F32),
                   jax.ShapeDtypeStruct((m, LANES), F32)],
        scratch_shapes=[pltpu.VMEM((tm + 2 * SUBLANES, 3 * GDN_WIDTH), F32)],
        compiler_params=pltpu.CompilerParams(dimension_semantics=("arbitrary",),
                                             vmem_limit_bytes=VMEM_LIMIT),
        name="in_proj",
    )(x2, g, w1, conv_w, alog_row, dtb_row)


ATTN_HEADS_PER_STEP = 4


def _attn_kernel(lam_ref, sg_ref, q_ref, k_ref, v_ref, o_ref, acc_ref, *, t, hp):
    hg = pl.program_id(1)
    qi = pl.program_id(2)

    lane = lax.broadcasted_iota(jnp.int32, (t, LANES), 1)
    col_f = lax.broadcasted_iota(jnp.int32, (1, t), 1).astype(F32)
    row_i = lax.broadcasted_iota(jnp.int32, (t, t), 0)
    col_i = lax.broadcasted_iota(jnp.int32, (t, t), 1)

    qm, slope_row = [], []
    for j in range(hp):
        q = q_ref[0, :, j * LANES:(j + 1) * LANES]
        zero = jnp.zeros_like(q)
        qm += [jnp.where(lane < DA_HEAD_DIM, q, zero), jnp.where(lane >= DA_HEAD_DIM, q, zero)]
        hf = (hg * hp + j + 1).astype(F32)
        slope_row.append(jnp.exp2(jnp.full((1, t), -8.0 / DA_HEADS, F32) * hf) * LOG2E)

    acc_ref[...] = jnp.zeros_like(acc_ref)

    def step(ki, ms, masked):
        start = pl.multiple_of(ki * t, t)
        pos = ((ki - qi) * t).astype(F32) + col_f
        nchain = 2 * hp

        def logits(idx):
            j = idx // 2
            k = k_ref[0, pl.ds(start, t), j * LANES:(j + 1) * LANES]
            s = (lax.dot_general(qm[idx], k, (((1,), (1,)), ((), ())), preferred_element_type=F32)
                 + pos * slope_row[j])
            return jnp.where(col_i <= row_i, s, NEG_BIG) if masked else s

        def probs(idx, s):
            m_new = jnp.maximum(ms[idx], jnp.max(s, axis=-1, keepdims=True))
            return m_new, jnp.exp2(s - m_new).astype(BF16)

        def accumulate(idx, m_new, p):
            j = idx // 2
            v = v_ref[0, pl.ds(start, t), 2 * j * LANES:2 * (j + 1) * LANES]
            acc_ref[idx] = (jnp.exp2(ms[idx] - m_new) * acc_ref[idx]
                            + jnp.dot(p, v, preferred_element_type=F32))

        s_prev, mp_prev, new_ms = None, None, []
        for slot in range(nchain + 2):
            s_cur = logits(slot) if slot < nchain else None
            mp_cur = probs(slot - 1, s_prev) if 1 <= slot <= nchain else None
            if slot >= 2:
                accumulate(slot - 2, *mp_prev)
                new_ms.append(mp_prev[0])
            s_prev, mp_prev = s_cur, mp_cur
        return tuple(new_ms)

    m0 = tuple(jnp.full((t, 1), NEG_BIG, F32) for _ in range(2 * hp))
    ms = lax.fori_loop(0, qi, lambda ki, c: step(ki, c, False), m0)
    lax.fori_loop(qi, qi + 1, lambda ki, c: step(ki, c, True), ms)

    lam_p = lam_ref[...]
    lam = (jnp.exp(jnp.sum(lam_p[0:1] * lam_p[1:2], axis=-1, keepdims=True))
           - jnp.exp(jnp.sum(lam_p[2:3] * lam_p[3:4], axis=-1, keepdims=True)) + LAM_INIT)
    for j in range(hp):
        a1 = acc_ref[2 * j]
        a2 = acc_ref[2 * j + 1]
        o = (a1[:, :DA_V_DIM] / a1[:, DA_V_DIM:DA_V_DIM + 1]
             - lam * (a2[:, :DA_V_DIM] / a2[:, DA_V_DIM:DA_V_DIM + 1]))
        o_ref[0, :, j * DA_V_DIM:(j + 1) * DA_V_DIM] = (_rms(o, sg_ref[...]) * (1.0 - LAM_INIT)).astype(BF16)


def _diff_attn(lam_p, subln_g, qk3, va3, t):
    b, s, _ = qk3.shape
    hp = ATTN_HEADS_PER_STEP
    ngroups = DA_HEADS // hp
    return pl.pallas_call(
        functools.partial(_attn_kernel, t=t, hp=hp),
        grid=(b, ngroups, s // t),
        in_specs=[pl.BlockSpec((4, DA_HEAD_DIM), lambda bi, hg, qi: (0, 0)),
                  pl.BlockSpec((1, DA_V_DIM), lambda bi, hg, qi: (0, 0)),
                  pl.BlockSpec((1, t, hp * LANES), lambda bi, hg, qi: (bi, qi, hg)),
                  pl.BlockSpec((1, s, hp * LANES), lambda bi, hg, qi: (bi, 0, ngroups + hg),
                               pipeline_mode=pl.Buffered(1)),
                  pl.BlockSpec((1, s, 2 * hp * LANES), lambda bi, hg, qi: (bi, 0, hg),
                               pipeline_mode=pl.Buffered(1))],
        out_specs=pl.BlockSpec((1, t, hp * DA_V_DIM), lambda bi, hg, qi: (bi, qi, hg)),
        out_shape=jax.ShapeDtypeStruct((b, s, DA_WIDTH), BF16),
        scratch_shapes=[pltpu.VMEM((2 * hp, t, 2 * LANES), F32)],
        compiler_params=pltpu.CompilerParams(dimension_semantics=("arbitrary", "arbitrary", "arbitrary"),
                                             vmem_limit_bytes=VMEM_LIMIT),
        name="diff_attn",
    )(lam_p, subln_g, qk3, qk3, va3)


GDN_GROUP = 2 * GDN_CHUNK
GDN_WQ_ROWS = 2 * GDN_CHUNK
GDN_KQ_ROWS = GDN_HEAD_DIM + GDN_CHUNK


def _sum3(ones_bf16, x):
    hi = x.astype(BF16)
    r1 = x - hi.astype(F32)
    mid = r1.astype(BF16)
    lo = (r1 - mid.astype(F32)).astype(BF16)
    dot = lambda t: jnp.dot(ones_bf16, t, preferred_element_type=F32)
    return dot(hi) + dot(mid) + dot(lo)


def _gdn_prep_kernel(gqkv_ref, gate_ref, u_ref, wq_ref, kq_ref, egl_ref, *, rb):
    c = GDN_CHUNK
    r = GDN_GROUP
    ri = lax.broadcasted_iota(jnp.int32, (r, r), 0)
    ci = lax.broadcasted_iota(jnp.int32, (r, r), 1)
    rx = ri ^ ci
    same_chunk = rx < c
    lower = ci <= ri
    strict = ci < ri
    eye = jnp.where(ci == ri, 1.0, 0.0).astype(F32)
    tri_ones = jnp.where(same_chunk, jnp.where(lower, 1.0, 0.0), 0.0).astype(BF16)
    blk_ones = jnp.where(same_chunk, 1.0, 0.0).astype(BF16)

    ngroups = rb // r
    chains = [(gidx, h) for gidx in range(ngroups) for h in range(GDN_HEADS)]
    each = lambda f: [f(j) for j in range(len(chains))]
    rows_of = lambda j: slice(chains[j][0] * r, (chains[j][0] + 1) * r)
    head_of = lambda j: chains[j][1]

    gate = [gate_ref[gidx * r:(gidx + 1) * r, :] for gidx in range(ngroups)]
    gc_all = [_sum3(tri_ones, gt) for gt in gate]
    gl_all = [_sum3(blk_ones, gt) for gt in gate]
    gc_t = [jnp.transpose(x) for x in gc_all]
    for gidx in range(ngroups):
        egl_ref[gidx * r:(gidx + 1) * r, :] = jnp.exp(gl_all[gidx])

    def hslice(j, part):
        lo = part * GDN_WIDTH + head_of(j) * GDN_HEAD_DIM
        return gqkv_ref[rows_of(j), lo:lo + GDN_HEAD_DIM]

    qh = each(lambda j: hslice(j, 0))
    kh = each(lambda j: hslice(j, 1))
    vh = each(lambda j: hslice(j, 2))
    beta = each(lambda j: gate[chains[j][0]][:, GDN_HEADS + head_of(j):GDN_HEADS + head_of(j) + 1])
    gc = each(lambda j: gc_all[chains[j][0]][:, head_of(j):head_of(j) + 1])
    gl = each(lambda j: gl_all[chains[j][0]][:, head_of(j):head_of(j) + 1])
    gc_row = each(lambda j: gc_t[chains[j][0]][head_of(j):head_of(j) + 1, :])

    decay = each(lambda j: jnp.exp(jnp.where(same_chunk, jnp.where(lower, gc[j] - gc_row[j], -jnp.inf), -jnp.inf)))
    kb = each(lambda j: kh[j] * beta[j])
    vb = each(lambda j: vh[j] * beta[j])
    kbf = each(lambda j: kh[j].astype(BF16))
    kk = each(lambda j: _mm_nt(kb[j], kbf[j]))
    qk = each(lambda j: _mm_nt(qh[j], kbf[j]))
    lmat = each(lambda j: jnp.where(strict, kk[j] * decay[j], 0.0))
    qk = each(lambda j: (qk[j] * decay[j]).astype(BF16))

    x1 = each(lambda j: jnp.where(rx < 16, -lmat[j], 0.0))
    p = each(lambda j: eye + x1[j])
    x2 = each(lambda j: _mm(x1[j], x1[j]))
    x2p = each(lambda j: _mm(x2[j], p[j]))
    x4 = each(lambda j: _mm(x2[j], x2[j]))
    p = each(lambda j: p[j] + x2p[j])
    x4p = each(lambda j: _mm(x4[j], p[j]))
    x8 = each(lambda j: _mm(x4[j], x4[j]))
    p = each(lambda j: p[j] + x4p[j])
    x8p = each(lambda j: _mm(x8[j], p[j]))
    p = each(lambda j: p[j] + x8p[j])
    b32 = each(lambda j: jnp.where(rx < 32, jnp.where(rx >= 16, lmat[j], 0.0), 0.0))
    t1 = each(lambda j: _mm(b32[j], p[j]))
    t2 = each(lambda j: _mm(p[j], t1[j]))
    p = each(lambda j: p[j] - t2[j])
    b64 = each(lambda j: jnp.where(rx >= 32, lmat[j], 0.0))
    t1 = each(lambda j: _mm(b64[j], p[j]))
    t2 = each(lambda j: _mm(p[j], t1[j]))
    tmat = each(lambda j: p[j] - t2[j])

    eg = each(lambda j: jnp.exp(gc[j]))
    uw = each(lambda j: _mm(tmat[j], jnp.concatenate([vb[j], kb[j] * eg[j]], axis=1)))
    qdec = each(lambda j: (qh[j] * eg[j]).astype(BF16))
    kdec_t = each(lambda j: jnp.transpose(kh[j] * jnp.exp(gl[j] - gc[j])).astype(BF16))
    for j, (gidx, h) in enumerate(chains):
        u_ref[rows_of(j), h * GDN_HEAD_DIM:(h + 1) * GDN_HEAD_DIM] = uw[j][:, :GDN_HEAD_DIM]
        w = uw[j][:, GDN_HEAD_DIM:].astype(BF16)
        for n in range(r // c):
            ch = gidx * (r // c) + n
            cr = slice(n * c, (n + 1) * c)
            wq_ref[h, ch * GDN_WQ_ROWS:ch * GDN_WQ_ROWS + c, :] = w[cr]
            wq_ref[h, ch * GDN_WQ_ROWS + c:(ch + 1) * GDN_WQ_ROWS, :] = qdec[j][cr]
            kq_ref[h, ch * GDN_KQ_ROWS:ch * GDN_KQ_ROWS + GDN_HEAD_DIM, :] = kdec_t[j][:, cr]
            kq_ref[h, ch * GDN_KQ_ROWS + GDN_HEAD_DIM:(ch + 1) * GDN_KQ_ROWS, :] = qk[j][cr, cr]


def _gdn_prep(gqkv, gate, rb):
    m = gqkv.shape[0]
    nch = rb // GDN_CHUNK
    row = lambda i: (i, 0)
    hrow = lambda i: (0, i, 0)
    return pl.pallas_call(
        functools.partial(_gdn_prep_kernel, rb=rb),
        grid=(m // rb,),
        in_specs=[pl.BlockSpec((rb, 3 * GDN_WIDTH), row),
                  pl.BlockSpec((rb, LANES), row)],
        out_specs=[pl.BlockSpec((rb, GDN_WIDTH), row),
                   pl.BlockSpec((GDN_HEADS, nch * GDN_WQ_ROWS, GDN_HEAD_DIM), hrow),
                   pl.BlockSpec((GDN_HEADS, nch * GDN_KQ_ROWS, GDN_CHUNK), hrow),
                   pl.BlockSpec((rb, LANES), row)],
        out_shape=[jax.ShapeDtypeStruct((m, GDN_WIDTH), F32),
                   jax.ShapeDtypeStruct((GDN_HEADS, m // GDN_CHUNK * GDN_WQ_ROWS, GDN_HEAD_DIM), BF16),
                   jax.ShapeDtypeStruct((GDN_HEADS, m // GDN_CHUNK * GDN_KQ_ROWS, GDN_CHUNK), BF16),
                   jax.ShapeDtypeStruct((m, LANES), F32)],
        compiler_params=pltpu.CompilerParams(dimension_semantics=("arbitrary",),
                                             vmem_limit_bytes=VMEM_LIMIT),
        name="gdn_prep",
    )(gqkv, gate)


def _gdn_scan_kernel(egl_ref, u_ref, wq_ref, kq_ref, gz_ref, ng_ref, og_ref, state_ref, *, nb, g, nchunks):
    i = pl.program_id(0)
    c = GDN_CHUNK

    @pl.when(i == 0)
    def _():
        state_ref[...] = jnp.zeros_like(state_ref)

    chains = [(b, h) for b in range(nb) for h in range(GDN_HEADS)]
    each = lambda f: [f(j, *chains[j]) for j in range(len(chains))]
    hcols = lambda h: slice(h * GDN_HEAD_DIM, (h + 1) * GDN_HEAD_DIM)
    states = each(lambda j, b, h: state_ref[j])
    for n in range(g):
        rows = slice(n * c, (n + 1) * c)
        ws_qs = each(lambda j, b, h: jnp.dot(wq_ref[h, b, n * GDN_WQ_ROWS:(n + 1) * GDN_WQ_ROWS, :],
                                             states[j].astype(BF16),
                                             preferred_element_type=F32))
        v_new = each(lambda j, b, h: (u_ref[b, rows, hcols(h)] - ws_qs[j][:c]).astype(BF16))
        kv_qv = each(lambda j, b, h: jnp.dot(kq_ref[h, b, n * GDN_KQ_ROWS:(n + 1) * GDN_KQ_ROWS, :], v_new[j],
                                             preferred_element_type=F32))
        states = each(lambda j, b, h: states[j] * egl_ref[(b * nchunks + i * g + n) * GDN_HEADS + h]
                      + kv_qv[j][:GDN_HEAD_DIM])
        for j, (b, h) in enumerate(chains):
            o = _rms(ws_qs[j][c:] + kv_qv[j][GDN_HEAD_DIM:], ng_ref[...])
            og_ref[b, rows, hcols(h)] = (o * _silu(gz_ref[b, rows, hcols(h)])).astype(BF16)
    for j in range(len(chains)):
        state_ref[j] = states[j]


def _gdn_scan(egl_flat, u3, wq4, kq4, gz3, norm_g, g):
    nb, s, _ = u3.shape
    nchunks = s // GDN_CHUNK
    blk = lambda i: (0, i, 0)
    hblk = lambda i: (0, 0, i, 0)
    return pl.pallas_call(
        functools.partial(_gdn_scan_kernel, nb=nb, g=g, nchunks=nchunks),
        grid=(nchunks // g,),
        in_specs=[pl.BlockSpec(memory_space=pltpu.SMEM),
                  pl.BlockSpec((nb, g * GDN_CHUNK, GDN_WIDTH), blk),
                  pl.BlockSpec((GDN_HEADS, nb, g * GDN_WQ_ROWS, GDN_HEAD_DIM), hblk),
                  pl.BlockSpec((GDN_HEADS, nb, g * GDN_KQ_ROWS, GDN_CHUNK), hblk),
                  pl.BlockSpec((nb, g * GDN_CHUNK, GDN_WIDTH), blk),
                  pl.BlockSpec((1, GDN_HEAD_DIM), lambda i: (0, 0))],
        out_specs=pl.BlockSpec((nb, g * GDN_CHUNK, GDN_WIDTH), blk),
        out_shape=jax.ShapeDtypeStruct((nb, s, GDN_WIDTH), BF16),
        scratch_shapes=[pltpu.VMEM((nb * GDN_HEADS, GDN_HEAD_DIM, GDN_HEAD_DIM), F32)],
        compiler_params=pltpu.CompilerParams(dimension_semantics=("arbitrary",),
                                             vmem_limit_bytes=VMEM_LIMIT),
        name="gdn_scan",
    )(egl_flat, u3, wq4, kq4, gz3, norm_g)


def _out_proj_kernel(x_ref, oa_ref, og_ref, wa_ref, wg_ref, g_ref, x1_ref, h2_ref):
    x1 = (x_ref[...] + jnp.dot(oa_ref[...], wa_ref[...], preferred_element_type=F32)
          + jnp.dot(og_ref[...], wg_ref[...], preferred_element_type=F32))
    x1_ref[...] = x1
    h2_ref[...] = _rms(x1, g_ref[...]).astype(BF16)


def _out_proj(x2, oa2, og2, wo_a, wo_g, g, tm):
    m = x2.shape[0]
    row = lambda i: (i, 0)
    fixed = lambda i: (0, 0)
    return pl.pallas_call(
        _out_proj_kernel,
        grid=(m // tm,),
        in_specs=[pl.BlockSpec((tm, D_MODEL), row),
                  pl.BlockSpec((tm, DA_WIDTH), row),
                  pl.BlockSpec((tm, GDN_WIDTH), row),
                  pl.BlockSpec((DA_WIDTH, D_MODEL), fixed),
                  pl.BlockSpec((GDN_WIDTH, D_MODEL), fixed),
                  pl.BlockSpec((1, D_MODEL), fixed)],
        out_specs=[pl.BlockSpec((tm, D_MODEL), row), pl.BlockSpec((tm, D_MODEL), row)],
        out_shape=[jax.ShapeDtypeStruct((m, D_MODEL), F32), jax.ShapeDtypeStruct((m, D_MODEL), BF16)],
        compiler_params=pltpu.CompilerParams(dimension_semantics=("arbitrary",),
                                             vmem_limit_bytes=VMEM_LIMIT),
        name="out_proj",
    )(x2, oa2, og2, wo_a, wo_g, g)


FFN_FC = 256


def _ffn_kernel(x1_ref, h2_ref, wup_ref, cw_ref, cb_ref, wd_ref, g_ref, out_ref,
                ubuf_ref, carry_ref, acc_ref, *, tm, tiles_per_seq):
    i = pl.program_id(0)
    halo = SUBLANES
    first = (i % tiles_per_seq) == 0
    h2 = h2_ref[...]

    @pl.when(i == 0)
    def _():
        carry_ref[...] = jnp.zeros_like(carry_ref)

    def conv_part(slot, lo):
        u = jnp.dot(h2, wup_ref[:, lo:lo + FFN_FC], preferred_element_type=F32)
        prev = carry_ref[:, lo:lo + FFN_FC]
        ubuf_ref[slot, 0:halo, :] = jnp.where(first, jnp.zeros_like(prev), prev)
        ubuf_ref[slot, halo:halo + tm, :] = u
        carry_ref[:, lo:lo + FFN_FC] = u[tm - halo:tm, :]
        cw = cw_ref[:, lo:lo + FFN_FC]
        return (cw[0:1] * ubuf_ref[slot, halo - 2:halo - 2 + tm, :]
                + cw[1:2] * ubuf_ref[slot, halo - 1:halo - 1 + tm, :]
                + cw[2:3] * u + cb_ref[:, lo:lo + FFN_FC])

    for cidx in range(D_FF // FFN_FC):
        gate = conv_part(2 * (cidx % 2), cidx * FFN_FC)
        val = conv_part(2 * (cidx % 2) + 1, D_FF + cidx * FFN_FC)
        act = (_silu(gate) * val).astype(BF16)
        contrib = jnp.dot(act, wd_ref[cidx * FFN_FC:(cidx + 1) * FFN_FC, :], preferred_element_type=F32)
        if cidx == 0:
            acc_ref[...] = contrib
        else:
            acc_ref[...] += contrib

    out_ref[...] = _rms(x1_ref[...] + acc_ref[...], g_ref[...])


def _ffn(x1, h2, w_up, conv_w, conv_b, w_down, g, tm, tiles_per_seq):
    m = x1.shape[0]
    row = lambda i: (i, 0)
    fixed = lambda i: (0, 0)
    once = pl.Buffered(1)
    return pl.pallas_call(
        functools.partial(_ffn_kernel, tm=tm, tiles_per_seq=tiles_per_seq),
        grid=(m // tm,),
        in_specs=[pl.BlockSpec((tm, D_MODEL), row),
                  pl.BlockSpec((tm, D_MODEL), row),
                  pl.BlockSpec((D_MODEL, 2 * D_FF), fixed, pipeline_mode=once),
                  pl.BlockSpec((FFN_CONV, 2 * D_FF), fixed),
                  pl.BlockSpec((1, 2 * D_FF), fixed),
                  pl.BlockSpec((D_FF, D_MODEL), fixed, pipeline_mode=once),
                  pl.BlockSpec((1, D_MODEL), fixed)],
        out_specs=pl.BlockSpec((tm, D_MODEL), row),
        out_shape=jax.ShapeDtypeStruct((m, D_MODEL), F32),
        scratch_shapes=[pltpu.VMEM((4, tm + SUBLANES, FFN_FC), F32),
                        pltpu.VMEM((SUBLANES, 2 * D_FF), F32),
                        pltpu.VMEM((tm, D_MODEL), F32)],
        compiler_params=pltpu.CompilerParams(dimension_semantics=("arbitrary",),
                                             vmem_limit_bytes=VMEM_LIMIT),
        name="ffn",
    )(x1, h2, w_up, conv_w, conv_b, w_down, g)


def _pad_lanes(v):
    return jnp.zeros((1, LANES), F32).at[0, :v.shape[0]].set(v.astype(F32))


def kernel(x, attn_norm_g, w_in, da_lambda_q1, da_lambda_k1, da_lambda_q2, da_lambda_k2, da_subln_g,
           gdn_conv_w, gdn_a_log, gdn_dt_bias, gdn_norm_g, w_out, ffn_norm_g, w_up, ffn_conv_w,
           ffn_conv_b, w_down, final_norm_g):
    b, s, d = x.shape
    m = b * s
    tm = 512
    assert d == D_MODEL and s % tm == 0 and w_in.shape[0] == 1, (x.shape, w_in.shape)
    l = 0
    x2 = x.reshape(m, d)

    n_main = 3 * DA_WIDTH + 3 * GDN_WIDTH
    wi = w_in[l]
    w1 = jnp.concatenate(
        [wi[:, :n_main], wi[:, n_main + 2 * GDN_HEADS:], wi[:, n_main:n_main + 2 * GDN_HEADS],
         jnp.zeros((d, LANES - 2 * GDN_HEADS), wi.dtype)], axis=1).astype(BF16)
    lam_p = jnp.stack([da_lambda_q1[l], da_lambda_k1[l], da_lambda_q2[l], da_lambda_k2[l]]).astype(F32)

    qk, va, gqkv, gz, gate = _in_proj(x2, attn_norm_g[l].reshape(1, d).astype(F32), w1,
                                      gdn_conv_w[l].astype(F32), _pad_lanes(gdn_a_log[l]),
                                      _pad_lanes(gdn_dt_bias[l]), tm, s // tm)

    oa = _diff_attn(lam_p, da_subln_g[l].reshape(1, DA_V_DIM).astype(F32),
                    qk.reshape(b, s, 2 * DA_WIDTH), va.reshape(b, s, 2 * DA_WIDTH), 512)

    u, wq, kq, egl = _gdn_prep(gqkv, gate, 256)
    nchunks = s // GDN_CHUNK
    egl_flat = egl.reshape(b, nchunks, GDN_CHUNK, LANES)[:, :, 0, :GDN_HEADS].reshape(-1)
    og = _gdn_scan(egl_flat, u.reshape(b, s, GDN_WIDTH),
                   wq.reshape(GDN_HEADS, b, nchunks * GDN_WQ_ROWS, GDN_HEAD_DIM),
                   kq.reshape(GDN_HEADS, b, nchunks * GDN_KQ_ROWS, GDN_CHUNK),
                   gz.reshape(b, s, GDN_WIDTH), gdn_norm_g[l].reshape(1, GDN_HEAD_DIM).astype(F32), 4)

    wo = w_out[l].astype(BF16)
    x1, h2 = _out_proj(x2, oa.reshape(m, DA_WIDTH), og.reshape(m, GDN_WIDTH), wo[:DA_WIDTH], wo[DA_WIDTH:],
                       ffn_norm_g[l].reshape(1, d).astype(F32), tm)

    out = _ffn(x1, h2, w_up[l].astype(BF16), ffn_conv_w[l].astype(F32),
               ffn_conv_b[l].reshape(1, 2 * D_FF).astype(F32), w_down[l].astype(BF16),
               final_norm_g.reshape(1, d).astype(F32), tm, s // tm)
    return out.reshape(b, s, d)
```

```python
import functools
import math

import jax
import jax.numpy as jnp
from jax import lax
from jax.experimental import pallas as pl
from jax.experimental.pallas import tpu as pltpu

F32 = jnp.float32
BF16 = jnp.bfloat16

EPS = 1e-6
LOG2E = 1.4426950408889634
NEG_BIG = -1e30

D_MODEL = 1024
DA_HEADS = 4
DA_HEAD_DIM = 64
DA_V_DIM = 128
DA_WIDTH = DA_HEADS * DA_V_DIM
GDN_HEADS = 4
GDN_HEAD_DIM = 128
GDN_WIDTH = GDN_HEADS * GDN_HEAD_DIM
GDN_CONV = 4
GDN_CHUNK = 64
D_FF = 2816
FFN_CONV = 3
LAM_INIT = 0.8 - 0.6 * math.exp(-0.3 * 0)

LANES = 128
SUBLANES = 8
VMEM_LIMIT = 56 * 1024 * 1024

W1_COLS = 3 * DA_WIDTH + 3 * GDN_WIDTH + GDN_WIDTH + LANES


def _mm(a, b):
    return jnp.dot(a.astype(BF16), b.astype(BF16), preferred_element_type=F32)


def _mm_nt(a, b):
    return lax.dot_general(a.astype(BF16), b.astype(BF16), (((1,), (1,)), ((), ())),
                           preferred_element_type=F32)


def _silu(x):
    return x * (1.0 / (1.0 + jnp.exp(-x)))


def _rms(x, g):
    return x * lax.rsqrt(jnp.mean(x * x, axis=-1, keepdims=True) + EPS) * g


def _in_proj_kernel(x_ref, g_ref, w_ref, cw_ref, alog_ref, dtb_ref,
                    qk_ref, va_ref, gqkv_ref, gz_ref, gate_ref, cbuf_ref, *, tm, tiles_per_seq):
    i = pl.program_id(0)
    halo = SUBLANES
    hb = _rms(x_ref[...], g_ref[...]).astype(BF16)

    def proj(lo, width):
        return jnp.dot(hb, w_ref[:, lo:lo + width], preferred_element_type=F32)

    lane = lax.broadcasted_iota(jnp.int32, (tm, LANES), 1)
    base = 3 * DA_WIDTH
    half = DA_WIDTH // 2

    @pl.when(i % tiles_per_seq == 0)
    def _():
        cbuf_ref[0:halo, :] = jnp.zeros((halo, 3 * GDN_WIDTH), F32)

    def gdn_part(c):
        cbuf_ref[halo:halo + tm, c * GDN_WIDTH:(c + 1) * GDN_WIDTH] = proj(base + c * GDN_WIDTH, GDN_WIDTH)

    def q_part(c):
        qk_ref[:, c * half:(c + 1) * half] = (proj(c * half, half) * (DA_HEAD_DIM ** -0.5 * LOG2E)).astype(BF16)

    def k_part(c):
        qk_ref[:, DA_WIDTH + c * half:DA_WIDTH + (c + 1) * half] = proj(DA_WIDTH + c * half, half).astype(BF16)

    def v_part(c):
        v = proj(2 * DA_WIDTH + c * half, half)
        ones_col = jnp.where(lane == 0, 1.0, 0.0).astype(BF16)
        for hh in range(2):
            h = 2 * c + hh
            va_ref[:, 2 * h * LANES:(2 * h + 1) * LANES] = v[:, hh * DA_V_DIM:(hh + 1) * DA_V_DIM].astype(BF16)
            va_ref[:, (2 * h + 1) * LANES:(2 * h + 2) * LANES] = ones_col

    def z_part(c):
        gz_ref[:, c * half:(c + 1) * half] = proj(base + 3 * GDN_WIDTH + c * half, half)

    def gate_part(_):
        gab = proj(base + 4 * GDN_WIDTH, LANES)
        sp_in = gab + dtb_ref[...]
        softplus = jnp.maximum(sp_in, 0.0) + jnp.log(1.0 + jnp.exp(-jnp.abs(sp_in)))
        gate_ref[...] = jnp.where(lane < GDN_HEADS, -jnp.exp(alog_ref[...]) * softplus,
                                  1.0 / (1.0 + jnp.exp(-gab)))

    def conv_group(c):
        cols = slice(c * GDN_HEAD_DIM, (c + 1) * GDN_HEAD_DIM)
        cw = cw_ref[:, cols]
        xc = cw[0:1] * cbuf_ref[halo - 3:halo - 3 + tm, cols]
        for j in range(1, GDN_CONV):
            xc = xc + cw[j:j + 1] * cbuf_ref[halo - 3 + j:halo - 3 + j + tm, cols]
        xc = _silu(xc)
        if c < GDN_HEADS:
            xc = xc * (lax.rsqrt(jnp.sum(xc * xc, axis=-1, keepdims=True) + EPS) * GDN_HEAD_DIM ** -0.5)
        elif c < 2 * GDN_HEADS:
            xc = xc * lax.rsqrt(jnp.sum(xc * xc, axis=-1, keepdims=True) + EPS)
        gqkv_ref[:, cols] = xc

    matmul_tasks = ([(gdn_part, c) for c in range(3)] + [(q_part, 0), (q_part, 1), (k_part, 0), (k_part, 1),
                    (v_part, 0), (v_part, 1), (z_part, 0), (z_part, 1), (gate_part, 0)])
    ngroups = 3 * GDN_HEADS
    for step in range(max(len(matmul_tasks), ngroups + 1)):
        if step < len(matmul_tasks):
            fn, arg = matmul_tasks[step]
            fn(arg)
        if 1 <= step <= ngroups:
            conv_group(step - 1)
    cbuf_ref[0:halo, :] = cbuf_ref[tm:tm + halo, :]


def _in_proj(x2, g, w1, conv_w, alog_row, dtb_row, tm, tiles_per_seq):
    m = x2.shape[0]
    row = lambda i: (i, 0)
    fixed = lambda i: (0, 0)
    return pl.pallas_call(
        functools.partial(_in_proj_kernel, tm=tm, tiles_per_seq=tiles_per_seq),
        grid=(m // tm,),
        in_specs=[pl.BlockSpec((tm, D_MODEL), row),
                  pl.BlockSpec((1, D_MODEL), fixed),
                  pl.BlockSpec((D_MODEL, W1_COLS), fixed),
                  pl.BlockSpec((GDN_CONV, 3 * GDN_WIDTH), fixed),
                  pl.BlockSpec((1, LANES), fixed),
                  pl.BlockSpec((1, LANES), fixed)],
        out_specs=[pl.BlockSpec((tm, 2 * DA_WIDTH), row),
                   pl.BlockSpec((tm, 2 * DA_WIDTH), row),
                   pl.BlockSpec((tm, 3 * GDN_WIDTH), row),
                   pl.BlockSpec((tm, GDN_WIDTH), row),
                   pl.BlockSpec((tm, LANES), row)],
        out_shape=[jax.ShapeDtypeStruct((m, 2 * DA_WIDTH), BF16),
                   jax.ShapeDtypeStruct((m, 2 * DA_WIDTH), BF16),
                   jax.ShapeDtypeStruct((m, 3 * GDN_WIDTH), F32),
                   jax.ShapeDtypeStruct((m, GDN_WIDTH), F32),
                   jax.ShapeDtypeStruct((m, LANES), F32)],
        scratch_shapes=[pltpu.VMEM((tm + 2 * SUBLANES, 3 * GDN_WIDTH), F32)],
        compiler_params=pltpu.CompilerParams(dimension_semantics=("arbitrary",),
                                             vmem_limit_bytes=VMEM_LIMIT),
        name="in_proj",
    )(x2, g, w1, conv_w, alog_row, dtb_row)


ATTN_HEADS_PER_STEP = 4


def _attn_kernel(lam_ref, sg_ref, q_ref, k_ref, v_ref, o_ref, acc_ref, *, t, hp):
    hg = pl.program_id(1)
    qi = pl.program_id(2)

    lane = lax.broadcasted_iota(jnp.int32, (t, LANES), 1)
    col_f = lax.broadcasted_iota(jnp.int32, (1, t), 1).astype(F32)
    row_i = lax.broadcasted_iota(jnp.int32, (t, t), 0)
    col_i = lax.broadcasted_iota(jnp.int32, (t, t), 1)

    qm, slope_row = [], []
    for j in range(hp):
        q = q_ref[0, :, j * LANES:(j + 1) * LANES]
        zero = jnp.zeros_like(q)
        qm += [jnp.where(lane < DA_HEAD_DIM, q, zero), jnp.where(lane >= DA_HEAD_DIM, q, zero)]
        hf = (hg * hp + j + 1).astype(F32)
        slope_row.append(jnp.exp2(jnp.full((1, t), -8.0 / DA_HEADS, F32) * hf) * LOG2E)

    acc_ref[...] = jnp.zeros_like(acc_ref)

    def step(ki, ms, masked):
        start = pl.multiple_of(ki * t, t)
        pos = ((ki - qi) * t).astype(F32) + col_f
        nchain = 2 * hp

        def logits(idx):
            j = idx // 2
            k = k_ref[0, pl.ds(start, t), j * LANES:(j + 1) * LANES]
            s = (lax.dot_general(qm[idx], k, (((1,), (1,)), ((), ())), preferred_element_type=F32)
                 + pos * slope_row[j])
            return jnp.where(col_i <= row_i, s, NEG_BIG) if masked else s

        def probs(idx, s):
            m_new = jnp.maximum(ms[idx], jnp.max(s, axis=-1, keepdims=True))
            return m_new, jnp.exp2(s - m_new).astype(BF16)

        def accumulate(idx, m_new, p):
            j = idx // 2
            v = v_ref[0, pl.ds(start, t), 2 * j * LANES:2 * (j + 1) * LANES]
            acc_ref[idx] = (jnp.exp2(ms[idx] - m_new) * acc_ref[idx]
                            + jnp.dot(p, v, preferred_element_type=F32))

        s_prev, mp_prev, new_ms = None, None, []
        for slot in range(nchain + 2):
            s_cur = logits(slot) if slot < nchain else None
            mp_cur = probs(slot - 1, s_prev) if 1 <= slot <= nchain else None
            if slot >= 2:
                accumulate(slot - 2, *mp_prev)
                new_ms.append(mp_prev[0])
            s_prev, mp_prev = s_cur, mp_cur
        return tuple(new_ms)

    m0 = tuple(jnp.full((t, 1), NEG_BIG, F32) for _ in range(2 * hp))
    ms = lax.fori_loop(0, qi, lambda ki, c: step(ki, c, False), m0)
    lax.fori_loop(qi, qi + 1, lambda ki, c: step(ki, c, True), ms)

    lam_p = lam_ref[...]
    lam = (jnp.exp(jnp.sum(lam_p[0:1] * lam_p[1:2], axis=-1, keepdims=True))
           - jnp.exp(jnp.sum(lam_p[2:3] * lam_p[3:4], axis=-1, keepdims=True)) + LAM_INIT)
    for j in range(hp):
        a1 = acc_ref[2 * j]
        a2 = acc_ref[2 * j + 1]
        o = (a1[:, :DA_V_DIM] / a1[:, DA_V_DIM:DA_V_DIM + 1]
             - lam * (a2[:, :DA_V_DIM] / a2[:, DA_V_DIM:DA_V_DIM + 1]))
        o_ref[0, :, j * DA_V_DIM:(j + 1) * DA_V_DIM] = (_rms(o, sg_ref[...]) * (1.0 - LAM_INIT)).astype(BF16)


def _diff_attn(lam_p, subln_g, qk3, va3, t):
    b, s, _ = qk3.shape
    hp = ATTN_HEADS_PER_STEP
    ngroups = DA_HEADS // hp
    return pl.pallas_call(
        functools.partial(_attn_kernel, t=t, hp=hp),
        grid=(b, ngroups, s // t),
        in_specs=[pl.BlockSpec((4, DA_HEAD_DIM), lambda bi, hg, qi: (0, 0)),
                  pl.BlockSpec((1, DA_V_DIM), lambda bi, hg, qi: (0, 0)),
                  pl.BlockSpec((1, t, hp * LANES), lambda bi, hg, qi: (bi, qi, hg)),
                  pl.BlockSpec((1, s, hp * LANES), lambda bi, hg, qi: (bi, 0, ngroups + hg),
                               pipeline_mode=pl.Buffered(1)),
                  pl.BlockSpec((1, s, 2 * hp * LANES), lambda bi, hg, qi: (bi, 0, hg),
                               pipeline_mode=pl.Buffered(1))],
        out_specs=pl.BlockSpec((1, t, hp * DA_V_DIM), lambda bi, hg, qi: (bi, qi, hg)),
        out_shape=jax.ShapeDtypeStruct((b, s, DA_WIDTH), BF16),
        scratch_shapes=[pltpu.VMEM((2 * hp, t, 2 * LANES), F32)],
        compiler_params=pltpu.CompilerParams(dimension_semantics=("arbitrary", "arbitrary", "arbitrary"),
                                             vmem_limit_bytes=VMEM_LIMIT),
        name="diff_attn",
    )(lam_p, subln_g, qk3, qk3, va3)


GDN_GROUP = 2 * GDN_CHUNK
GDN_WQ_ROWS = 2 * GDN_CHUNK
GDN_KQ_ROWS = GDN_HEAD_DIM + GDN_CHUNK


def _sum3(ones_bf16, x):
    hi = x.astype(BF16)
    r1 = x - hi.astype(F32)
    mid = r1.astype(BF16)
    lo = (r1 - mid.astype(F32)).astype(BF16)
    dot = lambda t: jnp.dot(ones_bf16, t, preferred_element_type=F32)
    return dot(hi) + dot(mid) + dot(lo)


def _gdn_prep_kernel(gqkv_ref, gate_ref, u_ref, wq_ref, kq_ref, egl_ref, *, rb):
    c = GDN_CHUNK
    r = GDN_GROUP
    ri = lax.broadcasted_iota(jnp.int32, (r, r), 0)
    ci = lax.broadcasted_iota(jnp.int32, (r, r), 1)
    rx = ri ^ ci
    same_chunk = rx < c
    lower = ci <= ri
    strict = ci < ri
    eye = jnp.where(ci == ri, 1.0, 0.0).astype(F32)
    tri_ones = jnp.where(same_chunk, jnp.where(lower, 1.0, 0.0), 0.0).astype(BF16)
    blk_ones = jnp.where(same_chunk, 1.0, 0.0).astype(BF16)

    ngroups = rb // r
    chains = [(gidx, h) for gidx in range(ngroups) for h in range(GDN_HEADS)]
    each = lambda f: [f(j) for j in range(len(chains))]
    rows_of = lambda j: slice(chains[j][0] * r, (chains[j][0] + 1) * r)
    head_of = lambda j: chains[j][1]

    gate = [gate_ref[gidx * r:(gidx + 1) * r, :] for gidx in range(ngroups)]
    gc_all = [_sum3(tri_ones, gt) for gt in gate]
    gl_all = [_sum3(blk_ones, gt) for gt in gate]
    gc_t = [jnp.transpose(x) for x in gc_all]
    for gidx in range(ngroups):
        egl_ref[gidx * r:(gidx + 1) * r, :] = jnp.exp(gl_all[gidx])

    def hslice(j, part):
        lo = part * GDN_WIDTH + head_of(j) * GDN_HEAD_DIM
        return gqkv_ref[rows_of(j), lo:lo + GDN_HEAD_DIM]

    qh = each(lambda j: hslice(j, 0))
    kh = each(lambda j: hslice(j, 1))
    vh = each(lambda j: hslice(j, 2))
    beta = each(lambda j: gate[chains[j][0]][:, GDN_HEADS + head_of(j):GDN_HEADS + head_of(j) + 1])
    gc = each(lambda j: gc_all[chains[j][0]][:, head_of(j):head_of(j) + 1])
    gl = each(lambda j: gl_all[chains[j][0]][:, head_of(j):head_of(j) + 1])
    gc_row = each(lambda j: gc_t[chains[j][0]][head_of(j):head_of(j) + 1, :])

    decay = each(lambda j: jnp.exp(jnp.where(same_chunk, jnp.where(lower, gc[j] - gc_row[j], -jnp.inf), -jnp.inf)))
    kb = each(lambda j: kh[j] * beta[j])
    vb = each(lambda j: vh[j] * beta[j])
    kbf = each(lambda j: kh[j].astype(BF16))
    kk = each(lambda j: _mm_nt(kb[j], kbf[j]))
    qk = each(lambda j: _mm_nt(qh[j], kbf[j]))
    lmat = each(lambda j: jnp.where(strict, kk[j] * decay[j], 0.0))
    qk = each(lambda j: (qk[j] * decay[j]).astype(BF16))

    x1 = each(lambda j: jnp.where(rx < 16, -lmat[j], 0.0))
    p = each(lambda j: eye + x1[j])
    x2 = each(lambda j: _mm(x1[j], x1[j]))
    x2p = each(lambda j: _mm(x2[j], p[j]))
    x4 = each(lambda j: _mm(x2[j], x2[j]))
    p = each(lambda j: p[j] + x2p[j])
    x4p = each(lambda j: _mm(x4[j], p[j]))
    x8 = each(lambda j: _mm(x4[j], x4[j]))
    p = each(lambda j: p[j] + x4p[j])
    x8p = each(lambda j: _mm(x8[j], p[j]))
    p = each(lambda j: p[j] + x8p[j])
    b32 = each(lambda j: jnp.where(rx < 32, jnp.where(rx >= 16, lmat[j], 0.0), 0.0))
    t1 = each(lambda j: _mm(b32[j], p[j]))
    t2 = each(lambda j: _mm(p[j], t1[j]))
    p = each(lambda j: p[j] - t2[j])
    b64 = each(lambda j: jnp.where(rx >= 32, lmat[j], 0.0))
    t1 = each(lambda j: _mm(b64[j], p[j]))
    t2 = each(lambda j: _mm(p[j], t1[j]))
    tmat = each(lambda j: p[j] - t2[j])

    eg = each(lambda j: jnp.exp(gc[j]))
    uw = each(lambda j: _mm(tmat[j], jnp.concatenate([vb[j], kb[j] * eg[j]], axis=1)))
    qdec = each(lambda j: (qh[j] * eg[j]).astype(BF16))
    kdec_t = each(lambda j: jnp.transpose(kh[j] * jnp.exp(gl[j] - gc[j])).astype(BF16))
    for j, (gidx, h) in enumerate(chains):
        u_ref[rows_of(j), h * GDN_HEAD_DIM:(h + 1) * GDN_HEAD_DIM] = uw[j][:, :GDN_HEAD_DIM]
        w = uw[j][:, GDN_HEAD_DIM:].astype(BF16)
        for n in range(r // c):
            ch = gidx * (r // c) + n
            cr = slice(n * c, (n + 1) * c)
            wq_ref[h, ch * GDN_WQ_ROWS:ch * GDN_WQ_ROWS + c, :] = w[cr]
            wq_ref[h, ch * GDN_WQ_ROWS + c:(ch + 1) * GDN_WQ_ROWS, :] = qdec[j][cr]
            kq_ref[h, ch * GDN_KQ_ROWS:ch * GDN_KQ_ROWS + GDN_HEAD_DIM, :] = kdec_t[j][:, cr]
            kq_ref[h, ch * GDN_KQ_ROWS + GDN_HEAD_DIM:(ch + 1) * GDN_KQ_ROWS, :] = qk[j][cr, cr]


def _gdn_prep(gqkv, gate, rb):
    m = gqkv.shape[0]
    nch = rb // GDN_CHUNK
    row = lambda i: (i, 0)
    hrow = lambda i: (0, i, 0)
    return pl.pallas_call(
        functools.partial(_gdn_prep_kernel, rb=rb),
        grid=(m // rb,),
        in_specs=[pl.BlockSpec((rb, 3 * GDN_WIDTH), row),
                  pl.BlockSpec((rb, LANES), row)],
        out_specs=[pl.BlockSpec((rb, GDN_WIDTH), row),
                   pl.BlockSpec((GDN_HEADS, nch * GDN_WQ_ROWS, GDN_HEAD_DIM), hrow),
                   pl.BlockSpec((GDN_HEADS, nch * GDN_KQ_ROWS, GDN_CHUNK), hrow),
                   pl.BlockSpec((rb, LANES), row)],
        out_shape=[jax.ShapeDtypeStruct((m, GDN_WIDTH), F32),
                   jax.ShapeDtypeStruct((GDN_HEADS, m // GDN_CHUNK * GDN_WQ_ROWS, GDN_HEAD_DIM), BF16),
                   jax.ShapeDtypeStruct((GDN_HEADS, m // GDN_CHUNK * GDN_KQ_ROWS, GDN_CHUNK), BF16),
                   jax.ShapeDtypeStruct((m, LANES), F32)],
        compiler_params=pltpu.CompilerParams(dimension_semantics=("arbitrary",),
                                             vmem_limit_bytes=VMEM_LIMIT),
        name="gdn_prep",
    )(gqkv, gate)


def _gdn_scan_kernel(egl_ref, u_ref, wq_ref, kq_ref, gz_ref, ng_ref, og_ref, state_ref, *, nb, g, nchunks):
    i = pl.program_id(0)
    c = GDN_CHUNK

    @pl.when(i == 0)
    def _():
        state_ref[...] = jnp.zeros_like(state_ref)

    chains = [(b, h) for b in range(nb) for h in range(GDN_HEADS)]
    each = lambda f: [f(j, *chains[j]) for j in range(len(chains))]
    hcols = lambda h: slice(h * GDN_HEAD_DIM, (h + 1) * GDN_HEAD_DIM)
    states = each(lambda j, b, h: state_ref[j])
    for n in range(g):
        rows = slice(n * c, (n + 1) * c)
        ws_qs = each(lambda j, b, h: jnp.dot(wq_ref[h, b, n * GDN_WQ_ROWS:(n + 1) * GDN_WQ_ROWS, :],
                                             states[j].astype(BF16),
                                             preferred_element_type=F32))
        v_new = each(lambda j, b, h: (u_ref[b, rows, hcols(h)] - ws_qs[j][:c]).astype(BF16))
        kv_qv = each(lambda j, b, h: jnp.dot(kq_ref[h, b, n * GDN_KQ_ROWS:(n + 1) * GDN_KQ_ROWS, :], v_new[j],
                                             preferred_element_type=F32))
        states = each(lambda j, b, h: states[j] * egl_ref[(b * nchunks + i * g + n) * GDN_HEADS + h]
                      + kv_qv[j][:GDN_HEAD_DIM])
        for j, (b, h) in enumerate(chains):
            o = _rms(ws_qs[j][c:] + kv_qv[j][GDN_HEAD_DIM:], ng_ref[...])
            og_ref[b, rows, hcols(h)] = (o * _silu(gz_ref[b, rows, hcols(h)])).astype(BF16)
    for j in range(len(chains)):
        state_ref[j] = states[j]


def _gdn_scan(egl_flat, u3, wq4, kq4, gz3, norm_g, g):
    nb, s, _ = u3.shape
    nchunks = s // GDN_CHUNK
    blk = lambda i: (0, i, 0)
    hblk = lambda i: (0, 0, i, 0)
    return pl.pallas_call(
        functools.partial(_gdn_scan_kernel, nb=nb, g=g, nchunks=nchunks),
        grid=(nchunks // g,),
        in_specs=[pl.BlockSpec(memory_space=pltpu.SMEM),
                  pl.BlockSpec((nb, g * GDN_CHUNK, GDN_WIDTH), blk),
                  pl.BlockSpec((GDN_HEADS, nb, g * GDN_WQ_ROWS, GDN_HEAD_DIM), hblk),
                  pl.BlockSpec((GDN_HEADS, nb, g * GDN_KQ_ROWS, GDN_CHUNK), hblk),
                  pl.BlockSpec((nb, g * GDN_CHUNK, GDN_WIDTH), blk),
                  pl.BlockSpec((1, GDN_HEAD_DIM), lambda i: (0, 0))],
        out_specs=pl.BlockSpec((nb, g * GDN_CHUNK, GDN_WIDTH), blk),
        out_shape=jax.ShapeDtypeStruct((nb, s, GDN_WIDTH), BF16),
        scratch_shapes=[pltpu.VMEM((nb * GDN_HEADS, GDN_HEAD_DIM, GDN_HEAD_DIM), F32)],
        compiler_params=pltpu.CompilerParams(dimension_semantics=("arbitrary",),
                                             vmem_limit_bytes=VMEM_LIMIT),
        name="gdn_scan",
    )(egl_flat, u3, wq4, kq4, gz3, norm_g)


def _out_proj_kernel(x_ref, oa_ref, og_ref, wa_ref, wg_ref, g_ref, x1_ref, h2_ref):
    x1 = (x_ref[...] + jnp.dot(oa_ref[...], wa_ref[...], preferred_element_type=F32)
          + jnp.dot(og_ref[...], wg_ref[...], preferred_element_type=F32))
    x1_ref[...] = x1
    h2_ref[...] = _rms(x1, g_ref[...]).astype(BF16)


def _out_proj(x2, oa2, og2, wo_a, wo_g, g, tm):
    m = x2.shape[0]
    row = lambda i: (i, 0)
    fixed = lambda i: (0, 0)
    return pl.pallas_call(
        _out_proj_kernel,
        grid=(m // tm,),
        in_specs=[pl.BlockSpec((tm, D_MODEL), row),
                  pl.BlockSpec((tm, DA_WIDTH), row),
                  pl.BlockSpec((tm, GDN_WIDTH), row),
                  pl.BlockSpec((DA_WIDTH, D_MODEL), fixed),
                  pl.BlockSpec((GDN_WIDTH, D_MODEL), fixed),
                  pl.BlockSpec((1, D_MODEL), fixed)],
        out_specs=[pl.BlockSpec((tm, D_MODEL), row), pl.BlockSpec((tm, D_MODEL), row)],
        out_shape=[jax.ShapeDtypeStruct((m, D_MODEL), F32), jax.ShapeDtypeStruct((m, D_MODEL), BF16)],
        compiler_params=pltpu.CompilerParams(dimension_semantics=("arbitrary",),
                                             vmem_limit_bytes=VMEM_LIMIT),
        name="out_proj",
    )(x2, oa2, og2, wo_a, wo_g, g)


FFN_FC = 256


def _ffn_kernel(x1_ref, h2_ref, wup_ref, cw_ref, cb_ref, wd_ref, g_ref, out_ref,
                ubuf_ref, carry_ref, acc_ref, *, tm, tiles_per_seq):
    i = pl.program_id(0)
    halo = SUBLANES
    first = (i % tiles_per_seq) == 0
    h2 = h2_ref[...]

    @pl.when(i == 0)
    def _():
        carry_ref[...] = jnp.zeros_like(carry_ref)

    nslab = D_FF // FFN_FC
    col_of = lambda cidx, part: part * D_FF + cidx * FFN_FC
    slot_of = lambda cidx, part: 2 * (cidx % 2) + part

    def up(cidx):
        for part in range(2):
            lo, slot = col_of(cidx, part), slot_of(cidx, part)
            u = jnp.dot(h2, wup_ref[:, lo:lo + FFN_FC], preferred_element_type=F32)
            prev = carry_ref[:, lo:lo + FFN_FC]
            ubuf_ref[slot, 0:halo, :] = jnp.where(first, jnp.zeros_like(prev), prev)
            ubuf_ref[slot, halo:halo + tm, :] = u
            carry_ref[:, lo:lo + FFN_FC] = u[tm - halo:tm, :]

    def conv(cidx, part):
        lo, slot = col_of(cidx, part), slot_of(cidx, part)
        cw = cw_ref[:, lo:lo + FFN_FC]
        return (cw[0:1] * ubuf_ref[slot, halo - 2:halo - 2 + tm, :]
                + cw[1:2] * ubuf_ref[slot, halo - 1:halo - 1 + tm, :]
                + cw[2:3] * ubuf_ref[slot, halo:halo + tm, :] + cb_ref[:, lo:lo + FFN_FC])

    def down(cidx, act):
        contrib = jnp.dot(act, wd_ref[cidx * FFN_FC:(cidx + 1) * FFN_FC, :], preferred_element_type=F32)
        if cidx == 0:
            acc_ref[...] = contrib
        else:
            acc_ref[...] += contrib

    act_prev = None
    for slot in range(nslab + 2):
        if slot < nslab:
            up(slot)
        act_cur = (_silu(conv(slot - 1, 0)) * conv(slot - 1, 1)).astype(BF16) if 1 <= slot <= nslab else None
        if slot >= 2:
            down(slot - 2, act_prev)
        act_prev = act_cur

    out_ref[...] = _rms(x1_ref[...] + acc_ref[...], g_ref[...])


def _ffn(x1, h2, w_up, conv_w, conv_b, w_down, g, tm, tiles_per_seq):
    m = x1.shape[0]
    row = lambda i: (i, 0)
    fixed = lambda i: (0, 0)
    once = pl.Buffered(1)
    return pl.pallas_call(
        functools.partial(_ffn_kernel, tm=tm, tiles_per_seq=tiles_per_seq),
        grid=(m // tm,),
        in_specs=[pl.BlockSpec((tm, D_MODEL), row),
                  pl.BlockSpec((tm, D_MODEL), row),
                  pl.BlockSpec((D_MODEL, 2 * D_FF), fixed, pipeline_mode=once),
                  pl.BlockSpec((FFN_CONV, 2 * D_FF), fixed),
                  pl.BlockSpec((1, 2 * D_FF), fixed),
                  pl.BlockSpec((D_FF, D_MODEL), fixed, pipeline_mode=once),
                  pl.BlockSpec((1, D_MODEL), fixed)],
        out_specs=pl.BlockSpec((tm, D_MODEL), row),
        out_shape=jax.ShapeDtypeStruct((m, D_MODEL), F32),
        scratch_shapes=[pltpu.VMEM((4, tm + SUBLANES, FFN_FC), F32),
                        pltpu.VMEM((SUBLANES, 2 * D_FF), F32),
                        pltpu.VMEM((tm, D_MODEL), F32)],
        compiler_params=pltpu.CompilerParams(dimension_semantics=("arbitrary",),
                                             vmem_limit_bytes=VMEM_LIMIT),
        name="ffn",
    )(x1, h2, w_up, conv_w, conv_b, w_down, g)


def _pad_lanes(v):
    return jnp.zeros((1, LANES), F32).at[0, :v.shape[0]].set(v.astype(F32))


def kernel(x, attn_norm_g, w_in, da_lambda_q1, da_lambda_k1, da_lambda_q2, da_lambda_k2, da_subln_g,
           gdn_conv_w, gdn_a_log, gdn_dt_bias, gdn_norm_g, w_out, ffn_norm_g, w_up, ffn_conv_w,
           ffn_conv_b, w_down, final_norm_g):
    b, s, d = x.shape
    m = b * s
    tm = 512
    assert d == D_MODEL and s % tm == 0 and w_in.shape[0] == 1, (x.shape, w_in.shape)
    l = 0
    x2 = x.reshape(m, d)

    n_main = 3 * DA_WIDTH + 3 * GDN_WIDTH
    wi = w_in[l]
    w1 = jnp.concatenate(
        [wi[:, :n_main], wi[:, n_main + 2 * GDN_HEADS:], wi[:, n_main:n_main + 2 * GDN_HEADS],
         jnp.zeros((d, LANES - 2 * GDN_HEADS), wi.dtype)], axis=1).astype(BF16)
    lam_p = jnp.stack([da_lambda_q1[l], da_lambda_k1[l], da_lambda_q2[l], da_lambda_k2[l]]).astype(F32)

    qk, va, gqkv, gz, gate = _in_proj(x2, attn_norm_g[l].reshape(1, d).astype(F32), w1,
                                      gdn_conv_w[l].astype(F32), _pad_lanes(gdn_a_log[l]),
                                      _pad_lanes(gdn_dt_bias[l]), tm, s // tm)

    oa = _diff_attn(lam_p, da_subln_g[l].reshape(1, DA_V_DIM).astype(F32),
                    qk.reshape(b, s, 2 * DA_WIDTH), va.reshape(b, s, 2 * DA_WIDTH), 512)

    u, wq, kq, egl = _gdn_prep(gqkv, gate, 256)
    nchunks = s // GDN_CHUNK
    egl_flat = egl.reshape(b, nchunks, GDN_CHUNK, LANES)[:, :, 0, :GDN_HEADS].reshape(-1)
    og = _gdn_scan(egl_flat, u.reshape(b, s, GDN_WIDTH),
                   wq.reshape(GDN_HEADS, b, nchunks * GDN_WQ_ROWS, GDN_HEAD_DIM),
                   kq.reshape(GDN_HEADS, b, nchunks * GDN_KQ_ROWS, GDN_CHUNK),
                   gz.reshape(b, s, GDN_WIDTH), gdn_norm_g[l].reshape(1, GDN_HEAD_DIM).astype(F32), 4)

    wo = w_out[l].astype(BF16)
    x1, h2 = _out_proj(x2, oa.reshape(m, DA_WIDTH), og.reshape(m, GDN_WIDTH), wo[:DA_WIDTH], wo[DA_WIDTH:],
                       ffn_norm_g[l].reshape(1, d).astype(F32), tm)

    out = _ffn(x1, h2, w_up[l].astype(BF16), ffn_conv_w[l].astype(F32),
               ffn_conv_b[l].reshape(1, 2 * D_FF).astype(F32), w_down[l].astype(BF16),
               final_norm_g.reshape(1, d).astype(F32), tm, s // tm)
    return out.reshape(b, s, d)
```

```python
import functools
import math

import jax
import jax.numpy as jnp
from jax import lax
from jax.experimental import pallas as pl
from jax.experimental.pallas import tpu as pltpu

F32 = jnp.float32
BF16 = jnp.bfloat16

EPS = 1e-6
LOG2E = 1.4426950408889634
NEG_BIG = -1e30

D_MODEL = 1024
DA_HEADS = 4
DA_HEAD_DIM = 64
DA_V_DIM = 128
DA_WIDTH = DA_HEADS * DA_V_DIM
GDN_HEADS = 4
GDN_HEAD_DIM = 128
GDN_WIDTH = GDN_HEADS * GDN_HEAD_DIM
GDN_CONV = 4
GDN_CHUNK = 64
D_FF = 2816
FFN_CONV = 3
LAM_INIT = 0.8 - 0.6 * math.exp(-0.3 * 0)

LANES = 128
SUBLANES = 8
VMEM_LIMIT = 56 * 1024 * 1024

W1_COLS = 3 * DA_WIDTH + 3 * GDN_WIDTH + GDN_WIDTH + LANES


def _mm(a, b):
    return jnp.dot(a.astype(BF16), b.astype(BF16), preferred_element_type=F32)


def _mm_nt(a, b):
    return lax.dot_general(a.astype(BF16), b.astype(BF16), (((1,), (1,)), ((), ())),
                           preferred_element_type=F32)


def _silu(x):
    return x * (1.0 / (1.0 + jnp.exp(-x)))


def _rms(x, g):
    return x * lax.rsqrt(jnp.mean(x * x, axis=-1, keepdims=True) + EPS) * g


def _in_proj_kernel(x_ref, g_ref, w_ref, cw_ref, alog_ref, dtb_ref,
                    qk_ref, va_ref, gqkv_ref, gz_ref, gate_ref, cbuf_ref, *, tm, tiles_per_seq):
    i = pl.program_id(0)
    halo = SUBLANES
    hb = _rms(x_ref[...], g_ref[...]).astype(BF16)

    def proj(lo, width):
        return jnp.dot(hb, w_ref[:, lo:lo + width], preferred_element_type=F32)

    lane = lax.broadcasted_iota(jnp.int32, (tm, LANES), 1)
    base = 3 * DA_WIDTH
    half = DA_WIDTH // 2

    @pl.when(i % tiles_per_seq == 0)
    def _():
        cbuf_ref[0:halo, :] = jnp.zeros((halo, 3 * GDN_WIDTH), F32)

    def gdn_part(c):
        cbuf_ref[halo:halo + tm, c * GDN_WIDTH:(c + 1) * GDN_WIDTH] = proj(base + c * GDN_WIDTH, GDN_WIDTH)

    def q_part(c):
        qk_ref[:, c * half:(c + 1) * half] = (proj(c * half, half) * (DA_HEAD_DIM ** -0.5 * LOG2E)).astype(BF16)

    def k_part(c):
        qk_ref[:, DA_WIDTH + c * half:DA_WIDTH + (c + 1) * half] = proj(DA_WIDTH + c * half, half).astype(BF16)

    def v_part(c):
        v = proj(2 * DA_WIDTH + c * half, half)
        ones_col = jnp.where(lane == 0, 1.0, 0.0).astype(BF16)
        for hh in range(2):
            h = 2 * c + hh
            va_ref[:, 2 * h * LANES:(2 * h + 1) * LANES] = v[:, hh * DA_V_DIM:(hh + 1) * DA_V_DIM].astype(BF16)
            va_ref[:, (2 * h + 1) * LANES:(2 * h + 2) * LANES] = ones_col

    def z_part(c):
        gz_ref[:, c * half:(c + 1) * half] = proj(base + 3 * GDN_WIDTH + c * half, half)

    def gate_part(_):
        gab = proj(base + 4 * GDN_WIDTH, LANES)
        sp_in = gab + dtb_ref[...]
        softplus = jnp.maximum(sp_in, 0.0) + jnp.log(1.0 + jnp.exp(-jnp.abs(sp_in)))
        gate_ref[...] = jnp.where(lane < GDN_HEADS, -jnp.exp(alog_ref[...]) * softplus,
                                  1.0 / (1.0 + jnp.exp(-gab)))

    def conv_group(c):
        cols = slice(c * GDN_HEAD_DIM, (c + 1) * GDN_HEAD_DIM)
        cw = cw_ref[:, cols]
        xc = cw[0:1] * cbuf_ref[halo - 3:halo - 3 + tm, cols]
        for j in range(1, GDN_CONV):
            xc = xc + cw[j:j + 1] * cbuf_ref[halo - 3 + j:halo - 3 + j + tm, cols]
        xc = _silu(xc)
        if c < GDN_HEADS:
            xc = xc * (lax.rsqrt(jnp.sum(xc * xc, axis=-1, keepdims=True) + EPS) * GDN_HEAD_DIM ** -0.5)
        elif c < 2 * GDN_HEADS:
            xc = xc * lax.rsqrt(jnp.sum(xc * xc, axis=-1, keepdims=True) + EPS)
        gqkv_ref[:, cols] = xc

    matmul_tasks = ([(gdn_part, c) for c in range(3)] + [(q_part, 0), (q_part, 1), (k_part, 0), (k_part, 1),
                    (v_part, 0), (v_part, 1), (z_part, 0), (z_part, 1), (gate_part, 0)])
    ngroups = 3 * GDN_HEADS
    for step in range(max(len(matmul_tasks), ngroups + 1)):
        if step < len(matmul_tasks):
            fn, arg = matmul_tasks[step]
            fn(arg)
        if 1 <= step <= ngroups:
            conv_group(step - 1)
    cbuf_ref[0:halo, :] = cbuf_ref[tm:tm + halo, :]


def _in_proj(x2, g, w1, conv_w, alog_row, dtb_row, tm, tiles_per_seq):
    m = x2.shape[0]
    row = lambda i: (i, 0)
    fixed = lambda i: (0, 0)
    return pl.pallas_call(
        functools.partial(_in_proj_kernel, tm=tm, tiles_per_seq=tiles_per_seq),
        grid=(m // tm,),
        in_specs=[pl.BlockSpec((tm, D_MODEL), row),
                  pl.BlockSpec((1, D_MODEL), fixed),
                  pl.BlockSpec((D_MODEL, W1_COLS), fixed),
                  pl.BlockSpec((GDN_CONV, 3 * GDN_WIDTH), fixed),
                  pl.BlockSpec((1, LANES), fixed),
                  pl.BlockSpec((1, LANES), fixed)],
        out_specs=[pl.BlockSpec((tm, 2 * DA_WIDTH), row),
                   pl.BlockSpec((tm, 2 * DA_WIDTH), row),
                   pl.BlockSpec((tm, 3 * GDN_WIDTH), row),
                   pl.BlockSpec((tm, GDN_WIDTH), row),
                   pl.BlockSpec((tm, LANES), row)],
        out_shape=[jax.ShapeDtypeStruct((m, 2 * DA_WIDTH), BF16),
                   jax.ShapeDtypeStruct((m, 2 * DA_WIDTH), BF16),
                   jax.ShapeDtypeStruct((m, 3 * GDN_WIDTH), F32),
                   jax.ShapeDtypeStruct((m, GDN_WIDTH), F32),
                   jax.ShapeDtypeStruct((m, LANES), F32)],
        scratch_shapes=[pltpu.VMEM((tm + 2 * SUBLANES, 3 * GDN_WIDTH), F32)],
        compiler_params=pltpu.CompilerParams(dimension_semantics=("arbitrary",),
                                             vmem_limit_bytes=VMEM_LIMIT),
        name="in_proj",
    )(x2, g, w1, conv_w, alog_row, dtb_row)


ATTN_HEADS_PER_STEP = 4


def _attn_kernel(lam_ref, sg_ref, q_ref, k_ref, v_ref, o_ref, acc_ref, *, t, hp):
    hg = pl.program_id(1)
    qi = pl.program_id(2)

    lane = lax.broadcasted_iota(jnp.int32, (t, LANES), 1)
    col_f = lax.broadcasted_iota(jnp.int32, (1, t), 1).astype(F32)
    row_i = lax.broadcasted_iota(jnp.int32, (t, t), 0)
    col_i = lax.broadcasted_iota(jnp.int32, (t, t), 1)

    qm, slope_row = [], []
    for j in range(hp):
        q = q_ref[0, :, j * LANES:(j + 1) * LANES]
        zero = jnp.zeros_like(q)
        qm += [jnp.where(lane < DA_HEAD_DIM, q, zero), jnp.where(lane >= DA_HEAD_DIM, q, zero)]
        hf = (hg * hp + j + 1).astype(F32)
        slope_row.append(jnp.exp2(jnp.full((1, t), -8.0 / DA_HEADS, F32) * hf) * LOG2E)

    acc_ref[...] = jnp.zeros_like(acc_ref)

    nchain = 2 * hp

    def step(blocks, ms):
        ms = list(ms)
        work = [(ki, masked, idx) for ki, masked in blocks for idx in range(nchain)]

        def logits(ki, masked, idx):
            j = idx // 2
            k = k_ref[0, pl.ds(pl.multiple_of(ki * t, t), t), j * LANES:(j + 1) * LANES]
            pos = ((ki - qi) * t).astype(F32) + col_f
            s = (lax.dot_general(qm[idx], k, (((1,), (1,)), ((), ())), preferred_element_type=F32)
                 + pos * slope_row[j])
            return jnp.where(col_i <= row_i, s, NEG_BIG) if masked else s

        def probs(idx, s):
            m_new = jnp.maximum(ms[idx], jnp.max(s, axis=-1, keepdims=True))
            return m_new, jnp.exp2(s - m_new).astype(BF16)

        def accumulate(ki, idx, m_new, p):
            j = idx // 2
            v = v_ref[0, pl.ds(pl.multiple_of(ki * t, t), t), 2 * j * LANES:2 * (j + 1) * LANES]
            acc_ref[idx] = (jnp.exp2(ms[idx] - m_new) * acc_ref[idx]
                            + jnp.dot(p, v, preferred_element_type=F32))
            ms[idx] = m_new

        s_prev, mp_prev = None, None
        for slot in range(len(work) + 2):
            s_cur = logits(*work[slot]) if slot < len(work) else None
            mp_cur = probs(work[slot - 1][2], s_prev) if 1 <= slot <= len(work) else None
            if slot >= 2:
                accumulate(work[slot - 2][0], work[slot - 2][2], *mp_prev)
            s_prev, mp_prev = s_cur, mp_cur
        return tuple(ms)

    m0 = tuple(jnp.full((t, 1), NEG_BIG, F32) for _ in range(nchain))
    npair = qi // 2
    odd = qi - 2 * npair
    ms = lax.fori_loop(0, npair, lambda j, c: step([(2 * j, False), (2 * j + 1, False)], c), m0)
    ms = lax.fori_loop(0, odd, lambda _, c: step([(qi - 1, False), (qi, True)], c), ms)
    lax.fori_loop(0, 1 - odd, lambda _, c: step([(qi, True)], c), ms)

    lam_p = lam_ref[...]
    lam = (jnp.exp(jnp.sum(lam_p[0:1] * lam_p[1:2], axis=-1, keepdims=True))
           - jnp.exp(jnp.sum(lam_p[2:3] * lam_p[3:4], axis=-1, keepdims=True)) + LAM_INIT)
    for j in range(hp):
        a1 = acc_ref[2 * j]
        a2 = acc_ref[2 * j + 1]
        o = (a1[:, :DA_V_DIM] / a1[:, DA_V_DIM:DA_V_DIM + 1]
             - lam * (a2[:, :DA_V_DIM] / a2[:, DA_V_DIM:DA_V_DIM + 1]))
        o_ref[0, :, j * DA_V_DIM:(j + 1) * DA_V_DIM] = (_rms(o, sg_ref[...]) * (1.0 - LAM_INIT)).astype(BF16)


def _diff_attn(lam_p, subln_g, qk3, va3, t):
    b, s, _ = qk3.shape
    hp = ATTN_HEADS_PER_STEP
    ngroups = DA_HEADS // hp
    return pl.pallas_call(
        functools.partial(_attn_kernel, t=t, hp=hp),
        grid=(b, ngroups, s // t),
        in_specs=[pl.BlockSpec((4, DA_HEAD_DIM), lambda bi, hg, qi: (0, 0)),
                  pl.BlockSpec((1, DA_V_DIM), lambda bi, hg, qi: (0, 0)),
                  pl.BlockSpec((1, t, hp * LANES), lambda bi, hg, qi: (bi, qi, hg)),
                  pl.BlockSpec((1, s, hp * LANES), lambda bi, hg, qi: (bi, 0, ngroups + hg),
                               pipeline_mode=pl.Buffered(1)),
                  pl.BlockSpec((1, s, 2 * hp * LANES), lambda bi, hg, qi: (bi, 0, hg),
                               pipeline_mode=pl.Buffered(1))],
        out_specs=pl.BlockSpec((1, t, hp * DA_V_DIM), lambda bi, hg, qi: (bi, qi, hg)),
        out_shape=jax.ShapeDtypeStruct((b, s, DA_WIDTH), BF16),
        scratch_shapes=[pltpu.VMEM((2 * hp, t, 2 * LANES), F32)],
        compiler_params=pltpu.CompilerParams(dimension_semantics=("arbitrary", "arbitrary", "arbitrary"),
                                             vmem_limit_bytes=VMEM_LIMIT),
        name="diff_attn",
    )(lam_p, subln_g, qk3, qk3, va3)


GDN_GROUP = 2 * GDN_CHUNK
GDN_WQ_ROWS = 2 * GDN_CHUNK
GDN_KQ_ROWS = GDN_HEAD_DIM + GDN_CHUNK


def _sum3(ones_bf16, x):
    hi = x.astype(BF16)
    r1 = x - hi.astype(F32)
    mid = r1.astype(BF16)
    lo = (r1 - mid.astype(F32)).astype(BF16)
    dot = lambda t: jnp.dot(ones_bf16, t, preferred_element_type=F32)
    return dot(hi) + dot(mid) + dot(lo)


def _gdn_prep_kernel(gqkv_ref, gate_ref, u_ref, wq_ref, kq_ref, egl_ref, *, rb):
    c = GDN_CHUNK
    r = GDN_GROUP
    ri = lax.broadcasted_iota(jnp.int32, (r, r), 0)
    ci = lax.broadcasted_iota(jnp.int32, (r, r), 1)
    rx = ri ^ ci
    same_chunk = rx < c
    lower = ci <= ri
    strict = ci < ri
    eye = jnp.where(ci == ri, 1.0, 0.0).astype(F32)
    tri_ones = jnp.where(same_chunk, jnp.where(lower, 1.0, 0.0), 0.0).astype(BF16)
    blk_ones = jnp.where(same_chunk, 1.0, 0.0).astype(BF16)

    ngroups = rb // r
    chains = [(gidx, h) for gidx in range(ngroups) for h in range(GDN_HEADS)]
    each = lambda f: [f(j) for j in range(len(chains))]
    rows_of = lambda j: slice(chains[j][0] * r, (chains[j][0] + 1) * r)
    head_of = lambda j: chains[j][1]

    gate = [gate_ref[gidx * r:(gidx + 1) * r, :] for gidx in range(ngroups)]
    gc_all = [_sum3(tri_ones, gt) for gt in gate]
    gl_all = [_sum3(blk_ones, gt) for gt in gate]
    gc_t = [jnp.transpose(x) for x in gc_all]
    for gidx in range(ngroups):
        egl_ref[gidx * r:(gidx + 1) * r, :] = jnp.exp(gl_all[gidx])

    def hslice(j, part):
        lo = part * GDN_WIDTH + head_of(j) * GDN_HEAD_DIM
        return gqkv_ref[rows_of(j), lo:lo + GDN_HEAD_DIM]

    qh = each(lambda j: hslice(j, 0))
    kh = each(lambda j: hslice(j, 1))
    vh = each(lambda j: hslice(j, 2))
    beta = each(lambda j: gate[chains[j][0]][:, GDN_HEADS + head_of(j):GDN_HEADS + head_of(j) + 1])
    gc = each(lambda j: gc_all[chains[j][0]][:, head_of(j):head_of(j) + 1])
    gl = each(lambda j: gl_all[chains[j][0]][:, head_of(j):head_of(j) + 1])
    gc_row = each(lambda j: gc_t[chains[j][0]][head_of(j):head_of(j) + 1, :])

    decay = each(lambda j: jnp.exp(jnp.where(same_chunk, jnp.where(lower, gc[j] - gc_row[j], -jnp.inf), -jnp.inf)))
    kb = each(lambda j: kh[j] * beta[j])
    vb = each(lambda j: vh[j] * beta[j])
    kbf = each(lambda j: kh[j].astype(BF16))
    kk = each(lambda j: _mm_nt(kb[j], kbf[j]))
    qk = each(lambda j: _mm_nt(qh[j], kbf[j]))
    lmat = each(lambda j: jnp.where(strict, kk[j] * decay[j], 0.0))
    qk = each(lambda j: (qk[j] * decay[j]).astype(BF16))

    x1 = each(lambda j: jnp.where(rx < 16, -lmat[j], 0.0))
    p = each(lambda j: eye + x1[j])
    x2 = each(lambda j: _mm(x1[j], x1[j]))
    x2p = each(lambda j: _mm(x2[j], p[j]))
    x4 = each(lambda j: _mm(x2[j], x2[j]))
    p = each(lambda j: p[j] + x2p[j])
    x4p = each(lambda j: _mm(x4[j], p[j]))
    x8 = each(lambda j: _mm(x4[j], x4[j]))
    p = each(lambda j: p[j] + x4p[j])
    x8p = each(lambda j: _mm(x8[j], p[j]))
    p = each(lambda j: p[j] + x8p[j])
    b32 = each(lambda j: jnp.where(rx < 32, jnp.where(rx >= 16, lmat[j], 0.0), 0.0))
    t1 = each(lambda j: _mm(b32[j], p[j]))
    t2 = each(lambda j: _mm(p[j], t1[j]))
    p = each(lambda j: p[j] - t2[j])
    b64 = each(lambda j: jnp.where(rx >= 32, lmat[j], 0.0))
    t1 = each(lambda j: _mm(b64[j], p[j]))
    t2 = each(lambda j: _mm(p[j], t1[j]))
    tmat = each(lambda j: p[j] - t2[j])

    eg = each(lambda j: jnp.exp(gc[j]))
    uw = each(lambda j: _mm(tmat[j], jnp.concatenate([vb[j], kb[j] * eg[j]], axis=1)))
    qdec = each(lambda j: (qh[j] * eg[j]).astype(BF16))
    kdec_t = each(lambda j: jnp.transpose(kh[j] * jnp.exp(gl[j] - gc[j])).astype(BF16))
    for j, (gidx, h) in enumerate(chains):
        u_ref[rows_of(j), h * GDN_HEAD_DIM:(h + 1) * GDN_HEAD_DIM] = uw[j][:, :GDN_HEAD_DIM]
        w = uw[j][:, GDN_HEAD_DIM:].astype(BF16)
        for n in range(r // c):
            ch = gidx * (r // c) + n
            cr = slice(n * c, (n + 1) * c)
            wq_ref[h, ch * GDN_WQ_ROWS:ch * GDN_WQ_ROWS + c, :] = w[cr]
            wq_ref[h, ch * GDN_WQ_ROWS + c:(ch + 1) * GDN_WQ_ROWS, :] = qdec[j][cr]
            kq_ref[h, ch * GDN_KQ_ROWS:ch * GDN_KQ_ROWS + GDN_HEAD_DIM, :] = kdec_t[j][:, cr]
            kq_ref[h, ch * GDN_KQ_ROWS + GDN_HEAD_DIM:(ch + 1) * GDN_KQ_ROWS, :] = qk[j][cr, cr]


def _gdn_prep(gqkv, gate, rb):
    m = gqkv.shape[0]
    nch = rb // GDN_CHUNK
    row = lambda i: (i, 0)
    hrow = lambda i: (0, i, 0)
    return pl.pallas_call(
        functools.partial(_gdn_prep_kernel, rb=rb),
        grid=(m // rb,),
        in_specs=[pl.BlockSpec((rb, 3 * GDN_WIDTH), row),
                  pl.BlockSpec((rb, LANES), row)],
        out_specs=[pl.BlockSpec((rb, GDN_WIDTH), row),
                   pl.BlockSpec((GDN_HEADS, nch * GDN_WQ_ROWS, GDN_HEAD_DIM), hrow),
                   pl.BlockSpec((GDN_HEADS, nch * GDN_KQ_ROWS, GDN_CHUNK), hrow),
                   pl.BlockSpec((rb, LANES), row)],
        out_shape=[jax.ShapeDtypeStruct((m, GDN_WIDTH), F32),
                   jax.ShapeDtypeStruct((GDN_HEADS, m // GDN_CHUNK * GDN_WQ_ROWS, GDN_HEAD_DIM), BF16),
                   jax.ShapeDtypeStruct((GDN_HEADS, m // GDN_CHUNK * GDN_KQ_ROWS, GDN_CHUNK), BF16),
                   jax.ShapeDtypeStruct((m, LANES), F32)],
        compiler_params=pltpu.CompilerParams(dimension_semantics=("arbitrary",),
                                             vmem_limit_bytes=VMEM_LIMIT),
        name="gdn_prep",
    )(gqkv, gate)


def _gdn_scan_kernel(egl_ref, u_ref, wq_ref, kq_ref, gz_ref, ng_ref, og_ref, state_ref, *, nb, g, nchunks):
    i = pl.program_id(0)
    c = GDN_CHUNK

    @pl.when(i == 0)
    def _():
        state_ref[...] = jnp.zeros_like(state_ref)

    chains = [(b, h) for b in range(nb) for h in range(GDN_HEADS)]
    each = lambda f: [f(j, *chains[j]) for j in range(len(chains))]
    hcols = lambda h: slice(h * GDN_HEAD_DIM, (h + 1) * GDN_HEAD_DIM)
    states = each(lambda j, b, h: state_ref[j])
    for n in range(g):
        rows = slice(n * c, (n + 1) * c)
        ws_qs = each(lambda j, b, h: jnp.dot(wq_ref[h, b, n * GDN_WQ_ROWS:(n + 1) * GDN_WQ_ROWS, :],
                                             states[j].astype(BF16),
                                             preferred_element_type=F32))
        v_new = each(lambda j, b, h: (u_ref[b, rows, hcols(h)] - ws_qs[j][:c]).astype(BF16))
        kv_qv = each(lambda j, b, h: jnp.dot(kq_ref[h, b, n * GDN_KQ_ROWS:(n + 1) * GDN_KQ_ROWS, :], v_new[j],
                                             preferred_element_type=F32))
        states = each(lambda j, b, h: states[j] * egl_ref[(b * nchunks + i * g + n) * GDN_HEADS + h]
                      + kv_qv[j][:GDN_HEAD_DIM])
        for j, (b, h) in enumerate(chains):
            o = _rms(ws_qs[j][c:] + kv_qv[j][GDN_HEAD_DIM:], ng_ref[...])
            og_ref[b, rows, hcols(h)] = (o * _silu(gz_ref[b, rows, hcols(h)])).astype(BF16)
    for j in range(len(chains)):
        state_ref[j] = states[j]


def _gdn_scan(egl_flat, u3, wq4, kq4, gz3, norm_g, g):
    nb, s, _ = u3.shape
    nchunks = s // GDN_CHUNK
    blk = lambda i: (0, i, 0)
    hblk = lambda i: (0, 0, i, 0)
    return pl.pallas_call(
        functools.partial(_gdn_scan_kernel, nb=nb, g=g, nchunks=nchunks),
        grid=(nchunks // g,),
        in_specs=[pl.BlockSpec(memory_space=pltpu.SMEM),
                  pl.BlockSpec((nb, g * GDN_CHUNK, GDN_WIDTH), blk),
                  pl.BlockSpec((GDN_HEADS, nb, g * GDN_WQ_ROWS, GDN_HEAD_DIM), hblk),
                  pl.BlockSpec((GDN_HEADS, nb, g * GDN_KQ_ROWS, GDN_CHUNK), hblk),
                  pl.BlockSpec((nb, g * GDN_CHUNK, GDN_WIDTH), blk),
                  pl.BlockSpec((1, GDN_HEAD_DIM), lambda i: (0, 0))],
        out_specs=pl.BlockSpec((nb, g * GDN_CHUNK, GDN_WIDTH), blk),
        out_shape=jax.ShapeDtypeStruct((nb, s, GDN_WIDTH), BF16),
        scratch_shapes=[pltpu.VMEM((nb * GDN_HEADS, GDN_HEAD_DIM, GDN_HEAD_DIM), F32)],
        compiler_params=pltpu.CompilerParams(dimension_semantics=("arbitrary",),
                                             vmem_limit_bytes=VMEM_LIMIT),
        name="gdn_scan",
    )(egl_flat, u3, wq4, kq4, gz3, norm_g)


def _out_proj_kernel(x_ref, oa_ref, og_ref, wa_ref, wg_ref, g_ref, x1_ref, h2_ref):
    x1 = (x_ref[...] + jnp.dot(oa_ref[...], wa_ref[...], preferred_element_type=F32)
          + jnp.dot(og_ref[...], wg_ref[...], preferred_element_type=F32))
    x1_ref[...] = x1
    h2_ref[...] = _rms(x1, g_ref[...]).astype(BF16)


def _out_proj(x2, oa2, og2, wo_a, wo_g, g, tm):
    m = x2.shape[0]
    row = lambda i: (i, 0)
    fixed = lambda i: (0, 0)
    return pl.pallas_call(
        _out_proj_kernel,
        grid=(m // tm,),
        in_specs=[pl.BlockSpec((tm, D_MODEL), row),
                  pl.BlockSpec((tm, DA_WIDTH), row),
                  pl.BlockSpec((tm, GDN_WIDTH), row),
                  pl.BlockSpec((DA_WIDTH, D_MODEL), fixed),
                  pl.BlockSpec((GDN_WIDTH, D_MODEL), fixed),
                  pl.BlockSpec((1, D_MODEL), fixed)],
        out_specs=[pl.BlockSpec((tm, D_MODEL), row), pl.BlockSpec((tm, D_MODEL), row)],
        out_shape=[jax.ShapeDtypeStruct((m, D_MODEL), F32), jax.ShapeDtypeStruct((m, D_MODEL), BF16)],
        compiler_params=pltpu.CompilerParams(dimension_semantics=("arbitrary",),
                                             vmem_limit_bytes=VMEM_LIMIT),
        name="out_proj",
    )(x2, oa2, og2, wo_a, wo_g, g)


FFN_FC = 256


def _ffn_kernel(x1_ref, h2_ref, wup_ref, cw_ref, cb_ref, wd_ref, g_ref, out_ref,
                ubuf_ref, carry_ref, acc_ref, *, tm, tiles_per_seq):
    i = pl.program_id(0)
    halo = SUBLANES
    first = (i % tiles_per_seq) == 0
    h2 = h2_ref[...]

    @pl.when(i == 0)
    def _():
        carry_ref[...] = jnp.zeros_like(carry_ref)

    nslab = D_FF // FFN_FC
    col_of = lambda cidx, part: part * D_FF + cidx * FFN_FC
    slot_of = lambda cidx, part: 2 * (cidx % 2) + part

    def up(cidx):
        for part in range(2):
            lo, slot = col_of(cidx, part), slot_of(cidx, part)
            u = jnp.dot(h2, wup_ref[:, lo:lo + FFN_FC], preferred_element_type=F32)
            prev = carry_ref[:, lo:lo + FFN_FC]
            ubuf_ref[slot, 0:halo, :] = jnp.where(first, jnp.zeros_like(prev), prev)
            ubuf_ref[slot, halo:halo + tm, :] = u
            carry_ref[:, lo:lo + FFN_FC] = u[tm - halo:tm, :]

    def conv(cidx, part):
        lo, slot = col_of(cidx, part), slot_of(cidx, part)
        cw = cw_ref[:, lo:lo + FFN_FC]
        return (cw[0:1] * ubuf_ref[slot, halo - 2:halo - 2 + tm, :]
                + cw[1:2] * ubuf_ref[slot, halo - 1:halo - 1 + tm, :]
                + cw[2:3] * ubuf_ref[slot, halo:halo + tm, :] + cb_ref[:, lo:lo + FFN_FC])

    def down(cidx, act):
        contrib = jnp.dot(act, wd_ref[cidx * FFN_FC:(cidx + 1) * FFN_FC, :], preferred_element_type=F32)
        if cidx == 0:
            acc_ref[...] = contrib
        else:
            acc_ref[...] += contrib

    act_prev = None
    for slot in range(nslab + 2):
        if slot < nslab:
            up(slot)
        act_cur = (_silu(conv(slot - 1, 0)) * conv(slot - 1, 1)).astype(BF16) if 1 <= slot <= nslab else None
        if slot >= 2:
            down(slot - 2, act_prev)
        act_prev = act_cur

    out_ref[...] = _rms(x1_ref[...] + acc_ref[...], g_ref[...])


def _ffn(x1, h2, w_up, conv_w, conv_b, w_down, g, tm, tiles_per_seq):
    m = x1.shape[0]
    row = lambda i: (i, 0)
    fixed = lambda i: (0, 0)
    once = pl.Buffered(1)
    return pl.pallas_call(
        functools.partial(_ffn_kernel, tm=tm, tiles_per_seq=tiles_per_seq),
        grid=(m // tm,),
        in_specs=[pl.BlockSpec((tm, D_MODEL), row),
                  pl.BlockSpec((tm, D_MODEL), row),
                  pl.BlockSpec((D_MODEL, 2 * D_FF), fixed, pipeline_mode=once),
                  pl.BlockSpec((FFN_CONV, 2 * D_FF), fixed),
                  pl.BlockSpec((1, 2 * D_FF), fixed),
                  pl.BlockSpec((D_FF, D_MODEL), fixed, pipeline_mode=once),
                  pl.BlockSpec((1, D_MODEL), fixed)],
        out_specs=pl.BlockSpec((tm, D_MODEL), row),
        out_shape=jax.ShapeDtypeStruct((m, D_MODEL), F32),
        scratch_shapes=[pltpu.VMEM((4, tm + SUBLANES, FFN_FC), F32),
                        pltpu.VMEM((SUBLANES, 2 * D_FF), F32),
                        pltpu.VMEM((tm, D_MODEL), F32)],
        compiler_params=pltpu.CompilerParams(dimension_semantics=("arbitrary",),
                                             vmem_limit_bytes=VMEM_LIMIT),
        name="ffn",
    )(x1, h2, w_up, conv_w, conv_b, w_down, g)


def _pad_lanes(v):
    return jnp.zeros((1, LANES), F32).at[0, :v.shape[0]].set(v.astype(F32))


def kernel(x, attn_norm_g, w_in, da_lambda_q1, da_lambda_k1, da_lambda_q2, da_lambda_k2, da_subln_g,
           gdn_conv_w, gdn_a_log, gdn_dt_bias, gdn_norm_g, w_out, ffn_norm_g, w_up, ffn_conv_w,
           ffn_conv_b, w_down, final_norm_g):
    b, s, d = x.shape
    m = b * s
    tm = 512
    assert d == D_MODEL and s % tm == 0 and w_in.shape[0] == 1, (x.shape, w_in.shape)
    l = 0
    x2 = x.reshape(m, d)

    n_main = 3 * DA_WIDTH + 3 * GDN_WIDTH
    wi = w_in[l]
    w1 = jnp.concatenate(
        [wi[:, :n_main], wi[:, n_main + 2 * GDN_HEADS:], wi[:, n_main:n_main + 2 * GDN_HEADS],
         jnp.zeros((d, LANES - 2 * GDN_HEADS), wi.dtype)], axis=1).astype(BF16)
    lam_p = jnp.stack([da_lambda_q1[l], da_lambda_k1[l], da_lambda_q2[l], da_lambda_k2[l]]).astype(F32)

    qk, va, gqkv, gz, gate = _in_proj(x2, attn_norm_g[l].reshape(1, d).astype(F32), w1,
                                      gdn_conv_w[l].astype(F32), _pad_lanes(gdn_a_log[l]),
                                      _pad_lanes(gdn_dt_bias[l]), tm, s // tm)

    oa = _diff_attn(lam_p, da_subln_g[l].reshape(1, DA_V_DIM).astype(F32),
                    qk.reshape(b, s, 2 * DA_WIDTH), va.reshape(b, s, 2 * DA_WIDTH), 512)

    u, wq, kq, egl = _gdn_prep(gqkv, gate, 256)
    nchunks = s // GDN_CHUNK
    egl_flat = egl.reshape(b, nchunks, GDN_CHUNK, LANES)[:, :, 0, :GDN_HEADS].reshape(-1)
    og = _gdn_scan(egl_flat, u.reshape(b, s, GDN_WIDTH),
                   wq.reshape(GDN_HEADS, b, nchunks * GDN_WQ_ROWS, GDN_HEAD_DIM),
                   kq.reshape(GDN_HEADS, b, nchunks * GDN_KQ_ROWS, GDN_CHUNK),
                   gz.reshape(b, s, GDN_WIDTH), gdn_norm_g[l].reshape(1, GDN_HEAD_DIM).astype(F32), 4)

    wo = w_out[l].astype(BF16)
    x1, h2 = _out_proj(x2, oa.reshape(m, DA_WIDTH), og.reshape(m, GDN_WIDTH), wo[:DA_WIDTH], wo[DA_WIDTH:],
                       ffn_norm_g[l].reshape(1, d).astype(F32), tm)

    out = _ffn(x1, h2, w_up[l].astype(BF16), ffn_conv_w[l].astype(F32),
               ffn_conv_b[l].reshape(1, 2 * D_FF).astype(F32), w_down[l].astype(BF16),
               final_norm_g.reshape(1, d).astype(F32), tm, s // tm)
    return out.reshape(b, s, d)
```

```python
import functools
import math

import jax
import jax.numpy as jnp
from jax import lax
from jax.experimental import pallas as pl
from jax.experimental.pallas import tpu as pltpu

F32 = jnp.float32
BF16 = jnp.bfloat16

EPS = 1e-6
LOG2E = 1.4426950408889634
NEG_BIG = -1e30

D_MODEL = 1024
DA_HEADS = 4
DA_HEAD_DIM = 64
DA_V_DIM = 128
DA_WIDTH = DA_HEADS * DA_V_DIM
GDN_HEADS = 4
GDN_HEAD_DIM = 128
GDN_WIDTH = GDN_HEADS * GDN_HEAD_DIM
GDN_CONV = 4
GDN_CHUNK = 64
D_FF = 2816
FFN_CONV = 3
LAM_INIT = 0.8 - 0.6 * math.exp(-0.3 * 0)

LANES = 128
SUBLANES = 8
VMEM_LIMIT = 56 * 1024 * 1024

W1_COLS = 3 * DA_WIDTH + 3 * GDN_WIDTH + GDN_WIDTH + LANES


def _mm(a, b):
    return jnp.dot(a.astype(BF16), b.astype(BF16), preferred_element_type=F32)


def _mm_nt(a, b):
    return lax.dot_general(a.astype(BF16), b.astype(BF16), (((1,), (1,)), ((), ())),
                           preferred_element_type=F32)


def _sum3(ones_bf16, x, left=True):
    hi = x.astype(BF16)
    r1 = x - hi.astype(F32)
    mid = r1.astype(BF16)
    lo = (r1 - mid.astype(F32)).astype(BF16)
    if left:
        dot = lambda t: jnp.dot(ones_bf16, t, preferred_element_type=F32)
    else:
        dot = lambda t: jnp.dot(t, ones_bf16, preferred_element_type=F32)
    return dot(hi) + dot(mid) + dot(lo)


def _silu(x):
    return x * (1.0 / (1.0 + jnp.exp(-x)))


def _rms(x, g):
    return x * lax.rsqrt(jnp.mean(x * x, axis=-1, keepdims=True) + EPS) * g


def _in_proj_kernel(x_ref, g_ref, w_ref, cw_ref, alog_ref, dtb_ref,
                    qk_ref, va_ref, gqkv_ref, gz_ref, gate_ref, cbuf_ref, *, tm, tiles_per_seq):
    i = pl.program_id(0)
    halo = SUBLANES
    hb = _rms(x_ref[...], g_ref[...]).astype(BF16)

    def proj(lo, width):
        return jnp.dot(hb, w_ref[:, lo:lo + width], preferred_element_type=F32)

    lane = lax.broadcasted_iota(jnp.int32, (tm, LANES), 1)
    base = 3 * DA_WIDTH
    half = DA_WIDTH // 2

    @pl.when(i % tiles_per_seq == 0)
    def _():
        cbuf_ref[0:halo, :] = jnp.zeros((halo, 3 * GDN_WIDTH), F32)

    def gdn_part(c):
        cbuf_ref[halo:halo + tm, c * half:(c + 1) * half] = proj(base + c * half, half)

    def q_part(c):
        qk_ref[:, c * half:(c + 1) * half] = (proj(c * half, half) * (DA_HEAD_DIM ** -0.5 * LOG2E)).astype(BF16)

    def k_part(c):
        qk_ref[:, DA_WIDTH + c * half:DA_WIDTH + (c + 1) * half] = proj(DA_WIDTH + c * half, half).astype(BF16)

    def v_part(c):
        v = proj(2 * DA_WIDTH + c * half, half)
        ones_col = jnp.ones((tm, LANES), BF16)
        for hh in range(2):
            h = 2 * c + hh
            va_ref[:, 2 * h * LANES:(2 * h + 1) * LANES] = v[:, hh * DA_V_DIM:(hh + 1) * DA_V_DIM].astype(BF16)
            va_ref[:, (2 * h + 1) * LANES:(2 * h + 2) * LANES] = ones_col

    def z_part(c):
        gz_ref[:, c * half:(c + 1) * half] = proj(base + 3 * GDN_WIDTH + c * half, half)

    def gate_part(_):
        gab = proj(base + 4 * GDN_WIDTH, LANES)
        sp_in = gab + dtb_ref[...]
        softplus = jnp.maximum(sp_in, 0.0) + jnp.log(1.0 + jnp.exp(-jnp.abs(sp_in)))
        gate_ref[...] = jnp.where(lane < GDN_HEADS, -jnp.exp(alog_ref[...]) * softplus,
                                  1.0 / (1.0 + jnp.exp(-gab)))

    def conv_group(c):
        cols = slice(c * GDN_HEAD_DIM, (c + 1) * GDN_HEAD_DIM)
        cw = cw_ref[:, cols]
        xc = cw[0:1] * cbuf_ref[halo - 3:halo - 3 + tm, cols]
        for j in range(1, GDN_CONV):
            xc = xc + cw[j:j + 1] * cbuf_ref[halo - 3 + j:halo - 3 + j + tm, cols]
        xc = _silu(xc)
        if c < GDN_HEADS:
            xc = xc * (lax.rsqrt(jnp.sum(xc * xc, axis=-1, keepdims=True) + EPS) * GDN_HEAD_DIM ** -0.5)
        elif c < 2 * GDN_HEADS:
            xc = xc * lax.rsqrt(jnp.sum(xc * xc, axis=-1, keepdims=True) + EPS)
        gqkv_ref[:, cols] = xc

    matmul_tasks = ([(gdn_part, c) for c in range(6)] + [(q_part, 0), (q_part, 1), (k_part, 0), (k_part, 1),
                    (v_part, 0), (v_part, 1), (z_part, 0), (z_part, 1), (gate_part, 0)])
    ngroups = 3 * GDN_HEADS
    issued = 0
    for c in range(ngroups):
        want = min(len(matmul_tasks), 2 + (c * len(matmul_tasks)) // ngroups)
        while issued < want:
            fn, arg = matmul_tasks[issued]
            fn(arg)
            issued += 1
        conv_group(c)
    for fn, arg in matmul_tasks[issued:]:
        fn(arg)
    cbuf_ref[0:halo, :] = cbuf_ref[tm:tm + halo, :]


def _in_proj(x2, g, w1, conv_w, alog_row, dtb_row, tm, tiles_per_seq):
    m = x2.shape[0]
    row = lambda i: (i, 0)
    fixed = lambda i: (0, 0)
    return pl.pallas_call(
        functools.partial(_in_proj_kernel, tm=tm, tiles_per_seq=tiles_per_seq),
        grid=(m // tm,),
        in_specs=[pl.BlockSpec((tm, D_MODEL), row),
                  pl.BlockSpec((1, D_MODEL), fixed),
                  pl.BlockSpec((D_MODEL, W1_COLS), fixed),
                  pl.BlockSpec((GDN_CONV, 3 * GDN_WIDTH), fixed),
                  pl.BlockSpec((1, LANES), fixed),
                  pl.BlockSpec((1, LANES), fixed)],
        out_specs=[pl.BlockSpec((tm, 2 * DA_WIDTH), row),
                   pl.BlockSpec((tm, 2 * DA_WIDTH), row),
                   pl.BlockSpec((tm, 3 * GDN_WIDTH), row),
                   pl.BlockSpec((tm, GDN_WIDTH), row),
                   pl.BlockSpec((tm, LANES), row)],
        out_shape=[jax.ShapeDtypeStruct((m, 2 * DA_WIDTH), BF16),
                   jax.ShapeDtypeStruct((m, 2 * DA_WIDTH), BF16),
                   jax.ShapeDtypeStruct((m, 3 * GDN_WIDTH), F32),
                   jax.ShapeDtypeStruct((m, GDN_WIDTH), F32),
                   jax.ShapeDtypeStruct((m, LANES), F32)],
        scratch_shapes=[pltpu.VMEM((tm + 2 * SUBLANES, 3 * GDN_WIDTH), F32)],
        compiler_params=pltpu.CompilerParams(dimension_semantics=("arbitrary",),
                                             vmem_limit_bytes=VMEM_LIMIT),
        name="in_proj",
    )(x2, g, w1, conv_w, alog_row, dtb_row)


ATTN_HEADS_PER_STEP = 4


def _attn_kernel(lam_ref, sg_ref, q_ref, k_ref, v_ref, o_ref, acc_ref, m_ref, *, t, hp):
    hg = pl.program_id(1)
    qi = pl.program_id(2)

    lane = lax.broadcasted_iota(jnp.int32, (t, LANES), 1)
    col_f = lax.broadcasted_iota(jnp.int32, (1, t), 1).astype(F32)
    row_i = lax.broadcasted_iota(jnp.int32, (t, t), 0)
    col_i = lax.broadcasted_iota(jnp.int32, (t, t), 1)

    qm, slope_row = [], []
    for j in range(hp):
        q = q_ref[0, :, j * LANES:(j + 1) * LANES]
        zero = jnp.zeros_like(q)
        qm += [jnp.where(lane < DA_HEAD_DIM, q, zero), jnp.where(lane >= DA_HEAD_DIM, q, zero)]
        hf = (hg * hp + j + 1).astype(F32)
        slope_row.append(jnp.exp2(jnp.full((1, t), -8.0 / DA_HEADS, F32) * hf) * LOG2E)

    acc_ref[...] = jnp.zeros_like(acc_ref)

    nchain = 2 * hp

    m_ref[...] = jnp.full(m_ref.shape, NEG_BIG, F32)
    lane_tiles = lambda x, n: jnp.concatenate([x] * n, axis=1)

    def step(blocks):
        ms = [m_ref[idx] for idx in range(nchain)]
        work = [(ki, masked, idx) for ki, masked in blocks for idx in range(nchain)]

        def logits(ki, masked, idx):
            j = idx // 2
            k = k_ref[0, pl.ds(pl.multiple_of(ki * t, t), t), j * LANES:(j + 1) * LANES]
            pos = ((ki - qi) * t).astype(F32) + col_f
            s = (lax.dot_general(qm[idx], k, (((1,), (1,)), ((), ())), preferred_element_type=F32)
                 + pos * slope_row[j])
            return jnp.where(col_i <= row_i, s, NEG_BIG) if masked else s

        def probs(idx, s):
            m_new = jnp.maximum(ms[idx], jnp.max(s, axis=-1, keepdims=True))
            return m_new, jnp.exp2(s - lane_tiles(m_new, t // LANES)).astype(BF16)

        def accumulate(ki, idx, m_new, p):
            j = idx // 2
            v = v_ref[0, pl.ds(pl.multiple_of(ki * t, t), t), 2 * j * LANES:2 * (j + 1) * LANES]
            acc_ref[idx] = (lane_tiles(jnp.exp2(ms[idx] - m_new), 2) * acc_ref[idx]
                            + jnp.dot(p, v, preferred_element_type=F32))
            ms[idx] = m_new

        s_prev, mp_prev = None, None
        for slot in range(len(work) + 2):
            s_cur = logits(*work[slot]) if slot < len(work) else None
            mp_cur = probs(work[slot - 1][2], s_prev) if 1 <= slot <= len(work) else None
            if slot >= 2:
                accumulate(work[slot - 2][0], work[slot - 2][2], *mp_prev)
            s_prev, mp_prev = s_cur, mp_cur
        for idx in range(nchain):
            m_ref[idx] = ms[idx]

    npair = qi // 2
    odd = qi - 2 * npair

    def run(trips, blocks_of):
        def body(j, carry):
            step(blocks_of(j))
            return carry
        lax.fori_loop(0, trips, body, 0)

    run(npair, lambda j: [(2 * j, False), (2 * j + 1, False)])
    run(odd, lambda j: [(qi - 1, False), (qi, True)])
    run(1 - odd, lambda j: [(qi, True)])

    lam_p = lam_ref[...]
    lam = (jnp.exp(jnp.sum(lam_p[0:1] * lam_p[1:2], axis=-1, keepdims=True))
           - jnp.exp(jnp.sum(lam_p[2:3] * lam_p[3:4], axis=-1, keepdims=True)) + LAM_INIT)
    ones_sq = jnp.ones((DA_V_DIM, DA_V_DIM), BF16)
    for j in range(hp):
        a1 = acc_ref[2 * j]
        a2 = acc_ref[2 * j + 1]
        o = a1[:, :DA_V_DIM] / a1[:, DA_V_DIM:] - lam * (a2[:, :DA_V_DIM] / a2[:, DA_V_DIM:])
        o_ref[0, :, j * DA_V_DIM:(j + 1) * DA_V_DIM] = (
            o * lax.rsqrt(_sum3(ones_sq, o * o, left=False) * (1.0 / DA_V_DIM) + EPS)
            * sg_ref[...] * (1.0 - LAM_INIT)).astype(BF16)


def _diff_attn(lam_p, subln_g, qk3, va3, t):
    b, s, _ = qk3.shape
    hp = ATTN_HEADS_PER_STEP
    ngroups = DA_HEADS // hp
    return pl.pallas_call(
        functools.partial(_attn_kernel, t=t, hp=hp),
        grid=(b, ngroups, s // t),
        in_specs=[pl.BlockSpec((4, DA_HEAD_DIM), lambda bi, hg, qi: (0, 0)),
                  pl.BlockSpec((1, DA_V_DIM), lambda bi, hg, qi: (0, 0)),
                  pl.BlockSpec((1, t, hp * LANES), lambda bi, hg, qi: (bi, qi, hg)),
                  pl.BlockSpec((1, s, hp * LANES), lambda bi, hg, qi: (bi, 0, ngroups + hg),
                               pipeline_mode=pl.Buffered(1)),
                  pl.BlockSpec((1, s, 2 * hp * LANES), lambda bi, hg, qi: (bi, 0, hg),
                               pipeline_mode=pl.Buffered(1))],
        out_specs=pl.BlockSpec((1, t, hp * DA_V_DIM), lambda bi, hg, qi: (bi, qi, hg)),
        out_shape=jax.ShapeDtypeStruct((b, s, DA_WIDTH), BF16),
        scratch_shapes=[pltpu.VMEM((2 * hp, t, 2 * LANES), F32),
                        pltpu.VMEM((2 * hp, t, LANES), F32)],
        compiler_params=pltpu.CompilerParams(dimension_semantics=("arbitrary", "arbitrary", "arbitrary"),
                                             vmem_limit_bytes=VMEM_LIMIT),
        name="diff_attn",
    )(lam_p, subln_g, qk3, qk3, va3)


GDN_GROUP = 2 * GDN_CHUNK
GDN_WQ_ROWS = 2 * GDN_CHUNK
GDN_KQ_ROWS = GDN_HEAD_DIM + GDN_CHUNK


def _gdn_prep_kernel(gqkv_ref, gate_ref, u_ref, wq_ref, kq_ref, egl_ref, *, rb):
    c = GDN_CHUNK
    r = GDN_GROUP
    ri = lax.broadcasted_iota(jnp.int32, (r, r), 0)
    ci = lax.broadcasted_iota(jnp.int32, (r, r), 1)
    rx = ri ^ ci
    same_chunk = rx < c
    lower = ci <= ri
    strict = ci < ri
    eye = jnp.where(ci == ri, 1.0, 0.0).astype(F32)
    tri_ones = jnp.where(same_chunk, jnp.where(lower, 1.0, 0.0), 0.0).astype(BF16)
    blk_ones = jnp.where(same_chunk, 1.0, 0.0).astype(BF16)

    ngroups = rb // r
    chains = [(gidx, h) for gidx in range(ngroups) for h in range(GDN_HEADS)]
    each = lambda f: [f(j) for j in range(len(chains))]
    rows_of = lambda j: slice(chains[j][0] * r, (chains[j][0] + 1) * r)
    head_of = lambda j: chains[j][1]

    gate = [gate_ref[gidx * r:(gidx + 1) * r, :] for gidx in range(ngroups)]
    gc_all = [_sum3(tri_ones, gt) for gt in gate]
    gl_all = [_sum3(blk_ones, gt) for gt in gate]
    gc_t = [jnp.transpose(x) for x in gc_all]
    for gidx in range(ngroups):
        egl_ref[gidx * r:(gidx + 1) * r, :] = jnp.exp(gl_all[gidx])

    def hslice(j, part):
        lo = part * GDN_WIDTH + head_of(j) * GDN_HEAD_DIM
        return gqkv_ref[rows_of(j), lo:lo + GDN_HEAD_DIM]

    qh = each(lambda j: hslice(j, 0))
    kh = each(lambda j: hslice(j, 1))
    vh = each(lambda j: hslice(j, 2))
    beta = each(lambda j: gate[chains[j][0]][:, GDN_HEADS + head_of(j):GDN_HEADS + head_of(j) + 1])
    gc = each(lambda j: gc_all[chains[j][0]][:, head_of(j):head_of(j) + 1])
    gl = each(lambda j: gl_all[chains[j][0]][:, head_of(j):head_of(j) + 1])
    gc_row = each(lambda j: gc_t[chains[j][0]][head_of(j):head_of(j) + 1, :])

    decay = each(lambda j: jnp.exp(jnp.where(same_chunk, jnp.where(lower, gc[j] - gc_row[j], -jnp.inf), -jnp.inf)))
    kb = each(lambda j: kh[j] * beta[j])
    vb = each(lambda j: vh[j] * beta[j])
    kbf = each(lambda j: kh[j].astype(BF16))
    kk = each(lambda j: _mm_nt(kb[j], kbf[j]))
    qk = each(lambda j: _mm_nt(qh[j], kbf[j]))
    lmat = each(lambda j: jnp.where(strict, kk[j] * decay[j], 0.0))
    qk = each(lambda j: (qk[j] * decay[j]).astype(BF16))

    x1 = each(lambda j: jnp.where(rx < 16, -lmat[j], 0.0))
    p = each(lambda j: eye + x1[j])
    x2 = each(lambda j: _mm(x1[j], x1[j]))
    x2p = each(lambda j: _mm(x2[j], p[j]))
    x4 = each(lambda j: _mm(x2[j], x2[j]))
    p = each(lambda j: p[j] + x2p[j])
    x4p = each(lambda j: _mm(x4[j], p[j]))
    x8 = each(lambda j: _mm(x4[j], x4[j]))
    p = each(lambda j: p[j] + x4p[j])
    x8p = each(lambda j: _mm(x8[j], p[j]))
    p = each(lambda j: p[j] + x8p[j])
    b32 = each(lambda j: jnp.where(rx < 32, jnp.where(rx >= 16, lmat[j], 0.0), 0.0))
    t1 = each(lambda j: _mm(b32[j], p[j]))
    t2 = each(lambda j: _mm(p[j], t1[j]))
    p = each(lambda j: p[j] - t2[j])
    b64 = each(lambda j: jnp.where(rx >= 32, lmat[j], 0.0))
    t1 = each(lambda j: _mm(b64[j], p[j]))
    t2 = each(lambda j: _mm(p[j], t1[j]))
    tmat = each(lambda j: p[j] - t2[j])

    eg = each(lambda j: jnp.exp(gc[j]))
    uw = each(lambda j: _mm(tmat[j], jnp.concatenate([vb[j], kb[j] * eg[j]], axis=1)))
    qdec = each(lambda j: (qh[j] * eg[j]).astype(BF16))
    kdec_t = each(lambda j: jnp.transpose(kh[j] * jnp.exp(gl[j] - gc[j])).astype(BF16))
    for j, (gidx, h) in enumerate(chains):
        u_ref[rows_of(j), h * GDN_HEAD_DIM:(h + 1) * GDN_HEAD_DIM] = uw[j][:, :GDN_HEAD_DIM]
        w = uw[j][:, GDN_HEAD_DIM:].astype(BF16)
        for n in range(r // c):
            ch = gidx * (r // c) + n
            cr = slice(n * c, (n + 1) * c)
            wq_ref[h, ch * GDN_WQ_ROWS:ch * GDN_WQ_ROWS + c, :] = w[cr]
            wq_ref[h, ch * GDN_WQ_ROWS + c:(ch + 1) * GDN_WQ_ROWS, :] = qdec[j][cr]
            kq_ref[h, ch * GDN_KQ_ROWS:ch * GDN_KQ_ROWS + GDN_HEAD_DIM, :] = kdec_t[j][:, cr]
            kq_ref[h, ch * GDN_KQ_ROWS + GDN_HEAD_DIM:(ch + 1) * GDN_KQ_ROWS, :] = qk[j][cr, cr]


def _gdn_prep(gqkv, gate, rb):
    m = gqkv.shape[0]
    nch = rb // GDN_CHUNK
    row = lambda i: (i, 0)
    hrow = lambda i: (0, i, 0)
    return pl.pallas_call(
        functools.partial(_gdn_prep_kernel, rb=rb),
        grid=(m // rb,),
        in_specs=[pl.BlockSpec((rb, 3 * GDN_WIDTH), row),
                  pl.BlockSpec((rb, LANES), row)],
        out_specs=[pl.BlockSpec((rb, GDN_WIDTH), row),
                   pl.BlockSpec((GDN_HEADS, nch * GDN_WQ_ROWS, GDN_HEAD_DIM), hrow),
                   pl.BlockSpec((GDN_HEADS, nch * GDN_KQ_ROWS, GDN_CHUNK), hrow),
                   pl.BlockSpec((rb, LANES), row)],
        out_shape=[jax.ShapeDtypeStruct((m, GDN_WIDTH), F32),
                   jax.ShapeDtypeStruct((GDN_HEADS, m // GDN_CHUNK * GDN_WQ_ROWS, GDN_HEAD_DIM), BF16),
                   jax.ShapeDtypeStruct((GDN_HEADS, m // GDN_CHUNK * GDN_KQ_ROWS, GDN_CHUNK), BF16),
                   jax.ShapeDtypeStruct((m, LANES), F32)],
        compiler_params=pltpu.CompilerParams(dimension_semantics=("arbitrary",),
                                             vmem_limit_bytes=VMEM_LIMIT),
        name="gdn_prep",
    )(gqkv, gate)


def _gdn_scan_kernel(egl_ref, u_ref, wq_ref, kq_ref, gz_ref, ng_ref, og_ref, state_ref, *, nb, g, nchunks):
    i = pl.program_id(0)
    c = GDN_CHUNK

    @pl.when(i == 0)
    def _():
        state_ref[...] = jnp.zeros_like(state_ref)

    chains = [(b, h) for b in range(nb) for h in range(GDN_HEADS)]
    each = lambda f: [f(j, *chains[j]) for j in range(len(chains))]
    hcols = lambda h: slice(h * GDN_HEAD_DIM, (h + 1) * GDN_HEAD_DIM)
    states = each(lambda j, b, h: state_ref[j])
    for n in range(g):
        rows = slice(n * c, (n + 1) * c)
        ws_qs = each(lambda j, b, h: jnp.dot(wq_ref[h, b, n * GDN_WQ_ROWS:(n + 1) * GDN_WQ_ROWS, :],
                                             states[j].astype(BF16),
                                             preferred_element_type=F32))
        v_new = each(lambda j, b, h: (u_ref[b, rows, hcols(h)] - ws_qs[j][:c]).astype(BF16))
        kv_qv = each(lambda j, b, h: jnp.dot(kq_ref[h, b, n * GDN_KQ_ROWS:(n + 1) * GDN_KQ_ROWS, :], v_new[j],
                                             preferred_element_type=F32))
        states = each(lambda j, b, h: states[j] * egl_ref[(b * nchunks + i * g + n) * GDN_HEADS + h]
                      + kv_qv[j][:GDN_HEAD_DIM])
        for j, (b, h) in enumerate(chains):
            o = _rms(ws_qs[j][c:] + kv_qv[j][GDN_HEAD_DIM:], ng_ref[...])
            og_ref[b, rows, hcols(h)] = (o * _silu(gz_ref[b, rows, hcols(h)])).astype(BF16)
    for j in range(len(chains)):
        state_ref[j] = states[j]


def _gdn_scan(egl_flat, u3, wq4, kq4, gz3, norm_g, g):
    nb, s, _ = u3.shape
    nchunks = s // GDN_CHUNK
    blk = lambda i: (0, i, 0)
    hblk = lambda i: (0, 0, i, 0)
    return pl.pallas_call(
        functools.partial(_gdn_scan_kernel, nb=nb, g=g, nchunks=nchunks),
        grid=(nchunks // g,),
        in_specs=[pl.BlockSpec(memory_space=pltpu.SMEM),
                  pl.BlockSpec((nb, g * GDN_CHUNK, GDN_WIDTH), blk),
                  pl.BlockSpec((GDN_HEADS, nb, g * GDN_WQ_ROWS, GDN_HEAD_DIM), hblk),
                  pl.BlockSpec((GDN_HEADS, nb, g * GDN_KQ_ROWS, GDN_CHUNK), hblk),
                  pl.BlockSpec((nb, g * GDN_CHUNK, GDN_WIDTH), blk),
                  pl.BlockSpec((1, GDN_HEAD_DIM), lambda i: (0, 0))],
        out_specs=pl.BlockSpec((nb, g * GDN_CHUNK, GDN_WIDTH), blk),
        out_shape=jax.ShapeDtypeStruct((nb, s, GDN_WIDTH), BF16),
        scratch_shapes=[pltpu.VMEM((nb * GDN_HEADS, GDN_HEAD_DIM, GDN_HEAD_DIM), F32)],
        compiler_params=pltpu.CompilerParams(dimension_semantics=("arbitrary",),
                                             vmem_limit_bytes=VMEM_LIMIT),
        name="gdn_scan",
    )(egl_flat, u3, wq4, kq4, gz3, norm_g)


def _out_proj_kernel(x_ref, oa_ref, og_ref, wa_ref, wg_ref, g_ref, x1_ref, h2_ref):
    x1 = (x_ref[...] + jnp.dot(oa_ref[...], wa_ref[...], preferred_element_type=F32)
          + jnp.dot(og_ref[...], wg_ref[...], preferred_element_type=F32))
    x1_ref[...] = x1
    h2_ref[...] = _rms(x1, g_ref[...]).astype(BF16)


def _out_proj(x2, oa2, og2, wo_a, wo_g, g, tm):
    m = x2.shape[0]
    row = lambda i: (i, 0)
    fixed = lambda i: (0, 0)
    return pl.pallas_call(
        _out_proj_kernel,
        grid=(m // tm,),
        in_specs=[pl.BlockSpec((tm, D_MODEL), row),
                  pl.BlockSpec((tm, DA_WIDTH), row),
                  pl.BlockSpec((tm, GDN_WIDTH), row),
                  pl.BlockSpec((DA_WIDTH, D_MODEL), fixed),
                  pl.BlockSpec((GDN_WIDTH, D_MODEL), fixed),
                  pl.BlockSpec((1, D_MODEL), fixed)],
        out_specs=[pl.BlockSpec((tm, D_MODEL), row), pl.BlockSpec((tm, D_MODEL), row)],
        out_shape=[jax.ShapeDtypeStruct((m, D_MODEL), F32), jax.ShapeDtypeStruct((m, D_MODEL), BF16)],
        compiler_params=pltpu.CompilerParams(dimension_semantics=("arbitrary",),
                                             vmem_limit_bytes=VMEM_LIMIT),
        name="out_proj",
    )(x2, oa2, og2, wo_a, wo_g, g)


FFN_FC = 256


def _ffn_kernel(x1_ref, h2_ref, wup_ref, cw_ref, cb_ref, wd_ref, g_ref, out_ref,
                ubuf_ref, carry_ref, acc_ref, *, tm, tiles_per_seq):
    i = pl.program_id(0)
    halo = SUBLANES
    first = (i % tiles_per_seq) == 0
    h2 = h2_ref[...]

    @pl.when(i == 0)
    def _():
        carry_ref[...] = jnp.zeros_like(carry_ref)

    nslab = D_FF // FFN_FC
    col_of = lambda cidx, part: part * D_FF + cidx * FFN_FC
    slot_of = lambda cidx, part: 2 * (cidx % 2) + part

    def up(cidx):
        for part in range(2):
            lo, slot = col_of(cidx, part), slot_of(cidx, part)
            u = jnp.dot(h2, wup_ref[:, lo:lo + FFN_FC], preferred_element_type=F32)
            prev = carry_ref[:, lo:lo + FFN_FC]
            ubuf_ref[slot, 0:halo, :] = jnp.where(first, jnp.zeros_like(prev), prev)
            ubuf_ref[slot, halo:halo + tm, :] = u
            carry_ref[:, lo:lo + FFN_FC] = u[tm - halo:tm, :]

    def conv(cidx, part):
        lo, slot = col_of(cidx, part), slot_of(cidx, part)
        cw = cw_ref[:, lo:lo + FFN_FC]
        return (cw[0:1] * ubuf_ref[slot, halo - 2:halo - 2 + tm, :]
                + cw[1:2] * ubuf_ref[slot, halo - 1:halo - 1 + tm, :]
                + cw[2:3] * ubuf_ref[slot, halo:halo + tm, :] + cb_ref[:, lo:lo + FFN_FC])

    def down(cidx, act):
        contrib = jnp.dot(act, wd_ref[cidx * FFN_FC:(cidx + 1) * FFN_FC, :], preferred_element_type=F32)
        if cidx == 0:
            acc_ref[...] = contrib
        else:
            acc_ref[...] += contrib

    act_prev = None
    for slot in range(nslab + 2):
        if slot < nslab:
            up(slot)
        act_cur = (_silu(conv(slot - 1, 0)) * conv(slot - 1, 1)).astype(BF16) if 1 <= slot <= nslab else None
        if slot >= 2:
            down(slot - 2, act_prev)
        act_prev = act_cur

    out_ref[...] = _rms(x1_ref[...] + acc_ref[...], g_ref[...])


def _ffn(x1, h2, w_up, conv_w, conv_b, w_down, g, tm, tiles_per_seq):
    m = x1.shape[0]
    row = lambda i: (i, 0)
    fixed = lambda i: (0, 0)
    once = pl.Buffered(1)
    return pl.pallas_call(
        functools.partial(_ffn_kernel, tm=tm, tiles_per_seq=tiles_per_seq),
        grid=(m // tm,),
        in_specs=[pl.BlockSpec((tm, D_MODEL), row),
                  pl.BlockSpec((tm, D_MODEL), row),
                  pl.BlockSpec((D_MODEL, 2 * D_FF), fixed, pipeline_mode=once),
                  pl.BlockSpec((FFN_CONV, 2 * D_FF), fixed),
                  pl.BlockSpec((1, 2 * D_FF), fixed),
                  pl.BlockSpec((D_FF, D_MODEL), fixed, pipeline_mode=once),
                  pl.BlockSpec((1, D_MODEL), fixed)],
        out_specs=pl.BlockSpec((tm, D_MODEL), row),
        out_shape=jax.ShapeDtypeStruct((m, D_MODEL), F32),
        scratch_shapes=[pltpu.VMEM((4, tm + SUBLANES, FFN_FC), F32),
                        pltpu.VMEM((SUBLANES, 2 * D_FF), F32),
                        pltpu.VMEM((tm, D_MODEL), F32)],
        compiler_params=pltpu.CompilerParams(dimension_semantics=("arbitrary",),
                                             vmem_limit_bytes=VMEM_LIMIT),
        name="ffn",
    )(x1, h2, w_up, conv_w, conv_b, w_down, g)


def _pad_lanes(v):
    return jnp.zeros((1, LANES), F32).at[0, :v.shape[0]].set(v.astype(F32))


def kernel(x, attn_norm_g, w_in, da_lambda_q1, da_lambda_k1, da_lambda_q2, da_lambda_k2, da_subln_g,
           gdn_conv_w, gdn_a_log, gdn_dt_bias, gdn_norm_g, w_out, ffn_norm_g, w_up, ffn_conv_w,
           ffn_conv_b, w_down, final_norm_g):
    b, s, d = x.shape
    m = b * s
    tm = 512
    assert d == D_MODEL and s % tm == 0 and w_in.shape[0] == 1, (x.shape, w_in.shape)
    l = 0
    x2 = x.reshape(m, d)

    n_main = 3 * DA_WIDTH + 3 * GDN_WIDTH
    wi = w_in[l]
    w1 = jnp.concatenate(
        [wi[:, :n_main], wi[:, n_main + 2 * GDN_HEADS:], wi[:, n_main:n_main + 2 * GDN_HEADS],
         jnp.zeros((d, LANES - 2 * GDN_HEADS), wi.dtype)], axis=1).astype(BF16)
    lam_p = jnp.stack([da_lambda_q1[l], da_lambda_k1[l], da_lambda_q2[l], da_lambda_k2[l]]).astype(F32)

    qk, va, gqkv, gz, gate = _in_proj(x2, attn_norm_g[l].reshape(1, d).astype(F32), w1,
                                      gdn_conv_w[l].astype(F32), _pad_lanes(gdn_a_log[l]),
                                      _pad_lanes(gdn_dt_bias[l]), tm, s // tm)

    oa = _diff_attn(lam_p, da_subln_g[l].reshape(1, DA_V_DIM).astype(F32),
                    qk.reshape(b, s, 2 * DA_WIDTH), va.reshape(b, s, 2 * DA_WIDTH), 512)

    u, wq, kq, egl = _gdn_prep(gqkv, gate, 256)
    nchunks = s // GDN_CHUNK
    egl_flat = egl.reshape(b, nchunks, GDN_CHUNK, LANES)[:, :, 0, :GDN_HEADS].reshape(-1)
    og = _gdn_scan(egl_flat, u.reshape(b, s, GDN_WIDTH),
                   wq.reshape(GDN_HEADS, b, nchunks * GDN_WQ_ROWS, GDN_HEAD_DIM),
                   kq.reshape(GDN_HEADS, b, nchunks * GDN_KQ_ROWS, GDN_CHUNK),
                   gz.reshape(b, s, GDN_WIDTH), gdn_norm_g[l].reshape(1, GDN_HEAD_DIM).astype(F32), 4)

    wo = w_out[l].astype(BF16)
    x1, h2 = _out_proj(x2, oa.reshape(m, DA_WIDTH), og.reshape(m, GDN_WIDTH), wo[:DA_WIDTH], wo[DA_WIDTH:],
                       ffn_norm_g[l].reshape(1, d).astype(F32), tm)

    out = _ffn(x1, h2, w_up[l].astype(BF16), ffn_conv_w[l].astype(F32),
               ffn_conv_b[l].reshape(1, 2 * D_FF).astype(F32), w_down[l].astype(BF16),
               final_norm_g.reshape(1, d).astype(F32), tm, s // tm)
    return out.reshape(b, s, d)
```

```python
import functools
import math

import jax
import jax.numpy as jnp
from jax import lax
from jax.experimental import pallas as pl
from jax.experimental.pallas import tpu as pltpu

F32 = jnp.float32
BF16 = jnp.bfloat16

EPS = 1e-6
LOG2E = 1.4426950408889634
NEG_BIG = -1e30

D_MODEL = 1024
DA_HEADS = 4
DA_HEAD_DIM = 64
DA_V_DIM = 128
DA_WIDTH = DA_HEADS * DA_V_DIM
GDN_HEADS = 4
GDN_HEAD_DIM = 128
GDN_WIDTH = GDN_HEADS * GDN_HEAD_DIM
GDN_CONV = 4
GDN_CHUNK = 64
D_FF = 2816
FFN_CONV = 3
LAM_INIT = 0.8 - 0.6 * math.exp(-0.3 * 0)

LANES = 128
SUBLANES = 8
VMEM_LIMIT = 56 * 1024 * 1024

W_MAIN_COLS = 3 * DA_WIDTH + 3 * GDN_WIDTH


def _mm(a, b):
    return jnp.dot(a.astype(BF16), b.astype(BF16), preferred_element_type=F32)


def _mm_nt(a, b):
    return lax.dot_general(a.astype(BF16), b.astype(BF16), (((1,), (1,)), ((), ())),
                           preferred_element_type=F32)


def _sum3(ones_bf16, x, left=True):
    hi = x.astype(BF16)
    r1 = x - hi.astype(F32)
    mid = r1.astype(BF16)
    lo = (r1 - mid.astype(F32)).astype(BF16)
    if left:
        dot = lambda t: jnp.dot(ones_bf16, t, preferred_element_type=F32)
    else:
        dot = lambda t: jnp.dot(t, ones_bf16, preferred_element_type=F32)
    return dot(hi) + dot(mid) + dot(lo)


def _silu(x):
    hx = 0.5 * x
    return hx + hx * jnp.tanh(hx)


def _rms(x, g):
    return x * lax.rsqrt(jnp.mean(x * x, axis=-1, keepdims=True) + EPS) * g


def _in_proj_kernel(x_ref, g_ref, w_ref, wz_ref, wab_ref, cw_ref, alog_ref, dtb_ref,
                    qk_ref, va_ref, gqkv_ref, gz_ref, gate_ref, cbuf_ref, *, tm, tiles_per_seq):
    i = pl.program_id(0)
    halo = SUBLANES
    hb = _rms(x_ref[...], g_ref[...]).astype(BF16)

    def proj(lo, width, ref=w_ref):
        return jnp.dot(hb, ref[:, lo:lo + width], preferred_element_type=F32)

    lane = lax.broadcasted_iota(jnp.int32, (tm, LANES), 1)
    base = 3 * DA_WIDTH
    half = DA_WIDTH // 2

    @pl.when(i % tiles_per_seq == 0)
    def _():
        cbuf_ref[0:halo, :] = jnp.zeros((halo, 3 * GDN_WIDTH), F32)

    def gdn_part(c):
        cbuf_ref[halo:halo + tm, c * half:(c + 1) * half] = proj(base + c * half, half)

    def q_part(c):
        qk_ref[:, c * half:(c + 1) * half] = (proj(c * half, half) * (DA_HEAD_DIM ** -0.5 * LOG2E)).astype(BF16)

    def k_part(c):
        qk_ref[:, DA_WIDTH + c * half:DA_WIDTH + (c + 1) * half] = proj(DA_WIDTH + c * half, half).astype(BF16)

    def v_part(c):
        v = proj(2 * DA_WIDTH + c * half, half)
        ones_col = jnp.ones((tm, LANES), BF16)
        for hh in range(2):
            h = 2 * c + hh
            va_ref[:, 2 * h * LANES:(2 * h + 1) * LANES] = v[:, hh * DA_V_DIM:(hh + 1) * DA_V_DIM].astype(BF16)
            va_ref[:, (2 * h + 1) * LANES:(2 * h + 2) * LANES] = ones_col

    def z_part(c):
        gz_ref[:, c * half:(c + 1) * half] = proj(c * half, half, wz_ref)

    def gate_part(_):
        gab = proj(0, LANES, wab_ref)
        sp_in = gab + dtb_ref[...]
        softplus = jnp.maximum(sp_in, 0.0) + jnp.log(1.0 + jnp.exp(-jnp.abs(sp_in)))
        gate_ref[...] = jnp.where(lane < GDN_HEADS, -jnp.exp(alog_ref[...]) * softplus,
                                  1.0 / (1.0 + jnp.exp(-gab)))

    def conv_group(c):
        cols = slice(c * GDN_HEAD_DIM, (c + 1) * GDN_HEAD_DIM)
        cw = cw_ref[:, cols]
        xc = cw[0:1] * cbuf_ref[halo - 3:halo - 3 + tm, cols]
        for j in range(1, GDN_CONV):
            xc = xc + cw[j:j + 1] * cbuf_ref[halo - 3 + j:halo - 3 + j + tm, cols]
        xc = _silu(xc)
        if c < GDN_HEADS:
            xc = xc * (lax.rsqrt(jnp.sum(xc * xc, axis=-1, keepdims=True) + EPS) * GDN_HEAD_DIM ** -0.5)
        elif c < 2 * GDN_HEADS:
            xc = xc * lax.rsqrt(jnp.sum(xc * xc, axis=-1, keepdims=True) + EPS)
        gqkv_ref[:, cols] = xc

    matmul_tasks = ([(gdn_part, c) for c in range(6)] + [(q_part, 0), (q_part, 1), (k_part, 0), (k_part, 1),
                    (v_part, 0), (v_part, 1), (z_part, 0), (z_part, 1), (gate_part, 0)])
    ngroups = 3 * GDN_HEADS
    issued = 0
    for c in range(ngroups):
        want = min(len(matmul_tasks), 2 + (c * len(matmul_tasks)) // ngroups)
        while issued < want:
            fn, arg = matmul_tasks[issued]
            fn(arg)
            issued += 1
        conv_group(c)
    for fn, arg in matmul_tasks[issued:]:
        fn(arg)
    cbuf_ref[0:halo, :] = cbuf_ref[tm:tm + halo, :]


def _in_proj(x2, g, w_main, w_z, w_ab, conv_w, alog_row, dtb_row, tm, tiles_per_seq):
    m = x2.shape[0]
    row = lambda i: (i, 0)
    fixed = lambda i: (0, 0)
    return pl.pallas_call(
        functools.partial(_in_proj_kernel, tm=tm, tiles_per_seq=tiles_per_seq),
        grid=(m // tm,),
        in_specs=[pl.BlockSpec((tm, D_MODEL), row),
                  pl.BlockSpec((1, D_MODEL), fixed),
                  pl.BlockSpec((D_MODEL, W_MAIN_COLS), fixed),
                  pl.BlockSpec((D_MODEL, GDN_WIDTH), fixed),
                  pl.BlockSpec((D_MODEL, LANES), fixed),
                  pl.BlockSpec((GDN_CONV, 3 * GDN_WIDTH), fixed),
                  pl.BlockSpec((1, LANES), fixed),
                  pl.BlockSpec((1, LANES), fixed)],
        out_specs=[pl.BlockSpec((tm, 2 * DA_WIDTH), row),
                   pl.BlockSpec((tm, 2 * DA_WIDTH), row),
                   pl.BlockSpec((tm, 3 * GDN_WIDTH), row),
                   pl.BlockSpec((tm, GDN_WIDTH), row),
                   pl.BlockSpec((tm, LANES), row)],
        out_shape=[jax.ShapeDtypeStruct((m, 2 * DA_WIDTH), BF16),
                   jax.ShapeDtypeStruct((m, 2 * DA_WIDTH), BF16),
                   jax.ShapeDtypeStruct((m, 3 * GDN_WIDTH), F32),
                   jax.ShapeDtypeStruct((m, GDN_WIDTH), F32),
                   jax.ShapeDtypeStruct((m, LANES), F32)],
        scratch_shapes=[pltpu.VMEM((tm + 2 * SUBLANES, 3 * GDN_WIDTH), F32)],
        compiler_params=pltpu.CompilerParams(dimension_semantics=("arbitrary",),
                                             vmem_limit_bytes=VMEM_LIMIT),
        name="in_proj",
    )(x2, g, w_main, w_z, w_ab, conv_w, alog_row, dtb_row)


ATTN_HEADS_PER_STEP = 4


def _attn_kernel(lam_ref, sg_ref, q_ref, k_ref, v_ref, o_ref, acc_ref, m_ref, *, t, hp):
    hg = pl.program_id(1)
    qi = pl.program_id(2)

    lane = lax.broadcasted_iota(jnp.int32, (t, LANES), 1)
    col_f = lax.broadcasted_iota(jnp.int32, (1, t), 1).astype(F32)
    row_i = lax.broadcasted_iota(jnp.int32, (t, t), 0)
    col_i = lax.broadcasted_iota(jnp.int32, (t, t), 1)

    qm, slope_row = [], []
    for j in range(hp):
        q = q_ref[0, :, j * LANES:(j + 1) * LANES]
        zero = jnp.zeros_like(q)
        qm += [jnp.where(lane < DA_HEAD_DIM, q, zero), jnp.where(lane >= DA_HEAD_DIM, q, zero)]
        hf = (hg * hp + j + 1).astype(F32)
        slope_row.append(jnp.exp2(jnp.full((1, t), -8.0 / DA_HEADS, F32) * hf) * LOG2E)

    acc_ref[...] = jnp.zeros_like(acc_ref)

    nchain = 2 * hp

    m_ref[...] = jnp.full(m_ref.shape, NEG_BIG, F32)
    lane_tiles = lambda x, n: jnp.concatenate([x] * n, axis=1)

    def step(blocks):
        ms = [m_ref[idx] for idx in range(nchain)]
        work = [(ki, masked, idx) for ki, masked in blocks for idx in range(nchain)]

        def logits(ki, masked, idx):
            j = idx // 2
            k = k_ref[0, pl.ds(pl.multiple_of(ki * t, t), t), j * LANES:(j + 1) * LANES]
            pos = ((ki - qi) * t).astype(F32) + col_f
            s = (lax.dot_general(qm[idx], k, (((1,), (1,)), ((), ())), preferred_element_type=F32)
                 + pos * slope_row[j])
            return jnp.where(col_i <= row_i, s, NEG_BIG) if masked else s

        def probs(idx, s):
            m_new = jnp.maximum(ms[idx], jnp.max(s, axis=-1, keepdims=True))
            return m_new, jnp.exp2(s - lane_tiles(m_new, t // LANES)).astype(BF16)

        def accumulate(ki, idx, m_new, p):
            j = idx // 2
            v = v_ref[0, pl.ds(pl.multiple_of(ki * t, t), t), 2 * j * LANES:2 * (j + 1) * LANES]
            acc_ref[idx] = (lane_tiles(jnp.exp2(ms[idx] - m_new), 2) * acc_ref[idx]
                            + jnp.dot(p, v, preferred_element_type=F32))
            ms[idx] = m_new

        s_prev, mp_prev = None, None
        for slot in range(len(work) + 2):
            s_cur = logits(*work[slot]) if slot < len(work) else None
            mp_cur = probs(work[slot - 1][2], s_prev) if 1 <= slot <= len(work) else None
            if slot >= 2:
                accumulate(work[slot - 2][0], work[slot - 2][2], *mp_prev)
            s_prev, mp_prev = s_cur, mp_cur
        for idx in range(nchain):
            m_ref[idx] = ms[idx]

    npair = qi // 2
    odd = qi - 2 * npair

    def run(trips, blocks_of):
        def body(j, carry):
            step(blocks_of(j))
            return carry
        lax.fori_loop(0, trips, body, 0)

    run(npair, lambda j: [(2 * j, False), (2 * j + 1, False)])
    run(odd, lambda j: [(qi - 1, False), (qi, True)])
    run(1 - odd, lambda j: [(qi, True)])

    lam_p = lam_ref[...]
    lam = (jnp.exp(jnp.sum(lam_p[0:1] * lam_p[1:2], axis=-1, keepdims=True))
           - jnp.exp(jnp.sum(lam_p[2:3] * lam_p[3:4], axis=-1, keepdims=True)) + LAM_INIT)
    ones_sq = jnp.ones((DA_V_DIM, DA_V_DIM), BF16)
    for j in range(hp):
        a1 = acc_ref[2 * j]
        a2 = acc_ref[2 * j + 1]
        o = a1[:, :DA_V_DIM] / a1[:, DA_V_DIM:] - lam * (a2[:, :DA_V_DIM] / a2[:, DA_V_DIM:])
        o_ref[0, :, j * DA_V_DIM:(j + 1) * DA_V_DIM] = (
            o * lax.rsqrt(_sum3(ones_sq, o * o, left=False) * (1.0 / DA_V_DIM) + EPS)
            * sg_ref[...] * (1.0 - LAM_INIT)).astype(BF16)


def _diff_attn(lam_p, subln_g, qk3, va3, t):
    b, s, _ = qk3.shape
    hp = ATTN_HEADS_PER_STEP
    ngroups = DA_HEADS // hp
    return pl.pallas_call(
        functools.partial(_attn_kernel, t=t, hp=hp),
        grid=(b, ngroups, s // t),
        in_specs=[pl.BlockSpec((4, DA_HEAD_DIM), lambda bi, hg, qi: (0, 0)),
                  pl.BlockSpec((1, DA_V_DIM), lambda bi, hg, qi: (0, 0)),
                  pl.BlockSpec((1, t, hp * LANES), lambda bi, hg, qi: (bi, qi, hg)),
                  pl.BlockSpec((1, s, hp * LANES), lambda bi, hg, qi: (bi, 0, ngroups + hg),
                               pipeline_mode=pl.Buffered(1)),
                  pl.BlockSpec((1, s, 2 * hp * LANES), lambda bi, hg, qi: (bi, 0, hg),
                               pipeline_mode=pl.Buffered(1))],
        out_specs=pl.BlockSpec((1, t, hp * DA_V_DIM), lambda bi, hg, qi: (bi, qi, hg)),
        out_shape=jax.ShapeDtypeStruct((b, s, DA_WIDTH), BF16),
        scratch_shapes=[pltpu.VMEM((2 * hp, t, 2 * LANES), F32),
                        pltpu.VMEM((2 * hp, t, LANES), F32)],
        compiler_params=pltpu.CompilerParams(dimension_semantics=("arbitrary", "arbitrary", "arbitrary"),
                                             vmem_limit_bytes=VMEM_LIMIT),
        name="diff_attn",
    )(lam_p, subln_g, qk3, qk3, va3)


GDN_GROUP = 2 * GDN_CHUNK
GDN_WQ_ROWS = 2 * GDN_CHUNK
GDN_KQ_ROWS = GDN_HEAD_DIM + GDN_CHUNK


def _gdn_prep_kernel(gqkv_ref, gate_ref, u_ref, wq_ref, kq_ref, egl_ref, *, rb):
    c = GDN_CHUNK
    r = GDN_GROUP
    ri = lax.broadcasted_iota(jnp.int32, (r, r), 0)
    ci = lax.broadcasted_iota(jnp.int32, (r, r), 1)
    rx = ri ^ ci
    same_chunk = rx < c
    lower = ci <= ri
    strict = ci < ri
    eye = jnp.where(ci == ri, 1.0, 0.0).astype(F32)
    tri_ones = jnp.where(same_chunk, jnp.where(lower, 1.0, 0.0), 0.0).astype(BF16)
    blk_ones = jnp.where(same_chunk, 1.0, 0.0).astype(BF16)

    ngroups = rb // r
    chains = [(gidx, h) for gidx in range(ngroups) for h in range(GDN_HEADS)]
    each = lambda f: [f(j) for j in range(len(chains))]
    rows_of = lambda j: slice(chains[j][0] * r, (chains[j][0] + 1) * r)
    head_of = lambda j: chains[j][1]

    gate = [gate_ref[gidx * r:(gidx + 1) * r, :] for gidx in range(ngroups)]
    gc_all = [_sum3(tri_ones, gt) for gt in gate]
    gl_all = [_sum3(blk_ones, gt) for gt in gate]
    gc_t = [jnp.transpose(x) for x in gc_all]
    for gidx in range(ngroups):
        egl_ref[gidx * r:(gidx + 1) * r, :] = jnp.exp(gl_all[gidx])

    def hslice(j, part):
        lo = part * GDN_WIDTH + head_of(j) * GDN_HEAD_DIM
        return gqkv_ref[rows_of(j), lo:lo + GDN_HEAD_DIM]

    qh = each(lambda j: hslice(j, 0))
    kh = each(lambda j: hslice(j, 1))
    vh = each(lambda j: hslice(j, 2))
    beta = each(lambda j: gate[chains[j][0]][:, GDN_HEADS + head_of(j):GDN_HEADS + head_of(j) + 1])
    gc = each(lambda j: gc_all[chains[j][0]][:, head_of(j):head_of(j) + 1])
    gl = each(lambda j: gl_all[chains[j][0]][:, head_of(j):head_of(j) + 1])
    gc_row = each(lambda j: gc_t[chains[j][0]][head_of(j):head_of(j) + 1, :])

    decay = each(lambda j: jnp.exp(jnp.where(same_chunk, jnp.where(lower, gc[j] - gc_row[j], -jnp.inf), -jnp.inf)))
    kb = each(lambda j: kh[j] * beta[j])
    vb = each(lambda j: vh[j] * beta[j])
    kbf = each(lambda j: kh[j].astype(BF16))
    kk = each(lambda j: _mm_nt(kb[j], kbf[j]))
    qk = each(lambda j: _mm_nt(qh[j], kbf[j]))
    lmat = each(lambda j: jnp.where(strict, kk[j] * decay[j], 0.0))
    qk = each(lambda j: (qk[j] * decay[j]).astype(BF16))

    x1 = each(lambda j: jnp.where(rx < 16, -lmat[j], 0.0))
    p = each(lambda j: eye + x1[j])
    x2 = each(lambda j: _mm(x1[j], x1[j]))
    x2p = each(lambda j: _mm(x2[j], p[j]))
    x4 = each(lambda j: _mm(x2[j], x2[j]))
    p = each(lambda j: p[j] + x2p[j])
    x4p = each(lambda j: _mm(x4[j], p[j]))
    x8 = each(lambda j: _mm(x4[j], x4[j]))
    p = each(lambda j: p[j] + x4p[j])
    x8p = each(lambda j: _mm(x8[j], p[j]))
    p = each(lambda j: p[j] + x8p[j])
    b32 = each(lambda j: jnp.where(rx < 32, jnp.where(rx >= 16, lmat[j], 0.0), 0.0))
    t1 = each(lambda j: _mm(b32[j], p[j]))
    t2 = each(lambda j: _mm(p[j], t1[j]))
    p = each(lambda j: p[j] - t2[j])
    b64 = each(lambda j: jnp.where(rx >= 32, lmat[j], 0.0))
    t1 = each(lambda j: _mm(b64[j], p[j]))
    t2 = each(lambda j: _mm(p[j], t1[j]))
    tmat = each(lambda j: p[j] - t2[j])

    eg = each(lambda j: jnp.exp(gc[j]))
    uw = each(lambda j: _mm(tmat[j], jnp.concatenate([vb[j], kb[j] * eg[j]], axis=1)))
    qdec = each(lambda j: (qh[j] * eg[j]).astype(BF16))
    kdec_t = each(lambda j: jnp.transpose(kh[j] * jnp.exp(gl[j] - gc[j])).astype(BF16))
    for j, (gidx, h) in enumerate(chains):
        u_ref[rows_of(j), h * GDN_HEAD_DIM:(h + 1) * GDN_HEAD_DIM] = uw[j][:, :GDN_HEAD_DIM]
        w = uw[j][:, GDN_HEAD_DIM:].astype(BF16)
        for n in range(r // c):
            ch = gidx * (r // c) + n
            cr = slice(n * c, (n + 1) * c)
            wq_ref[h, ch * GDN_WQ_ROWS:ch * GDN_WQ_ROWS + c, :] = w[cr]
            wq_ref[h, ch * GDN_WQ_ROWS + c:(ch + 1) * GDN_WQ_ROWS, :] = qdec[j][cr]
            kq_ref[h, ch * GDN_KQ_ROWS:ch * GDN_KQ_ROWS + GDN_HEAD_DIM, :] = kdec_t[j][:, cr]
            kq_ref[h, ch * GDN_KQ_ROWS + GDN_HEAD_DIM:(ch + 1) * GDN_KQ_ROWS, :] = qk[j][cr, cr]


def _gdn_prep(gqkv, gate, rb):
    m = gqkv.shape[0]
    nch = rb // GDN_CHUNK
    row = lambda i: (i, 0)
    hrow = lambda i: (0, i, 0)
    return pl.pallas_call(
        functools.partial(_gdn_prep_kernel, rb=rb),
        grid=(m // rb,),
        in_specs=[pl.BlockSpec((rb, 3 * GDN_WIDTH), row),
                  pl.BlockSpec((rb, LANES), row)],
        out_specs=[pl.BlockSpec((rb, GDN_WIDTH), row),
                   pl.BlockSpec((GDN_HEADS, nch * GDN_WQ_ROWS, GDN_HEAD_DIM), hrow),
                   pl.BlockSpec((GDN_HEADS, nch * GDN_KQ_ROWS, GDN_CHUNK), hrow),
                   pl.BlockSpec((rb, LANES), row)],
        out_shape=[jax.ShapeDtypeStruct((m, GDN_WIDTH), F32),
                   jax.ShapeDtypeStruct((GDN_HEADS, m // GDN_CHUNK * GDN_WQ_ROWS, GDN_HEAD_DIM), BF16),
                   jax.ShapeDtypeStruct((GDN_HEADS, m // GDN_CHUNK * GDN_KQ_ROWS, GDN_CHUNK), BF16),
                   jax.ShapeDtypeStruct((m, LANES), F32)],
        compiler_params=pltpu.CompilerParams(dimension_semantics=("arbitrary",),
                                             vmem_limit_bytes=VMEM_LIMIT),
        name="gdn_prep",
    )(gqkv, gate)


def _gdn_scan_kernel(egl_ref, u_ref, wq_ref, kq_ref, gz_ref, ng_ref, og_ref, state_ref, *, nb, g, nchunks):
    i = pl.program_id(0)
    c = GDN_CHUNK

    @pl.when(i == 0)
    def _():
        state_ref[...] = jnp.zeros_like(state_ref)

    chains = [(b, h) for b in range(nb) for h in range(GDN_HEADS)]
    each = lambda f: [f(j, *chains[j]) for j in range(len(chains))]
    hcols = lambda h: slice(h * GDN_HEAD_DIM, (h + 1) * GDN_HEAD_DIM)
    states = each(lambda j, b, h: state_ref[j])
    for n in range(g):
        rows = slice(n * c, (n + 1) * c)
        ws_qs = each(lambda j, b, h: jnp.dot(wq_ref[h, b, n * GDN_WQ_ROWS:(n + 1) * GDN_WQ_ROWS, :],
                                             states[j].astype(BF16),
                                             preferred_element_type=F32))
        v_new = each(lambda j, b, h: (u_ref[b, rows, hcols(h)] - ws_qs[j][:c]).astype(BF16))
        kv_qv = each(lambda j, b, h: jnp.dot(kq_ref[h, b, n * GDN_KQ_ROWS:(n + 1) * GDN_KQ_ROWS, :], v_new[j],
                                             preferred_element_type=F32))
        states = each(lambda j, b, h: states[j] * egl_ref[(b * nchunks + i * g + n) * GDN_HEADS + h]
                      + kv_qv[j][:GDN_HEAD_DIM])
        for j, (b, h) in enumerate(chains):
            o = _rms(ws_qs[j][c:] + kv_qv[j][GDN_HEAD_DIM:], ng_ref[...])
            og_ref[b, rows, hcols(h)] = (o * _silu(gz_ref[b, rows, hcols(h)])).astype(BF16)
    for j in range(len(chains)):
        state_ref[j] = states[j]


def _gdn_scan(egl_flat, u3, wq4, kq4, gz3, norm_g, g):
    nb, s, _ = u3.shape
    nchunks = s // GDN_CHUNK
    blk = lambda i: (0, i, 0)
    hblk = lambda i: (0, 0, i, 0)
    return pl.pallas_call(
        functools.partial(_gdn_scan_kernel, nb=nb, g=g, nchunks=nchunks),
        grid=(nchunks // g,),
        in_specs=[pl.BlockSpec(memory_space=pltpu.SMEM),
                  pl.BlockSpec((nb, g * GDN_CHUNK, GDN_WIDTH), blk),
                  pl.BlockSpec((GDN_HEADS, nb, g * GDN_WQ_ROWS, GDN_HEAD_DIM), hblk),
                  pl.BlockSpec((GDN_HEADS, nb, g * GDN_KQ_ROWS, GDN_CHUNK), hblk),
                  pl.BlockSpec((nb, g * GDN_CHUNK, GDN_WIDTH), blk),
                  pl.BlockSpec((1, GDN_HEAD_DIM), lambda i: (0, 0))],
        out_specs=pl.BlockSpec((nb, g * GDN_CHUNK, GDN_WIDTH), blk),
        out_shape=jax.ShapeDtypeStruct((nb, s, GDN_WIDTH), BF16),
        scratch_shapes=[pltpu.VMEM((nb * GDN_HEADS, GDN_HEAD_DIM, GDN_HEAD_DIM), F32)],
        compiler_params=pltpu.CompilerParams(dimension_semantics=("arbitrary",),
                                             vmem_limit_bytes=VMEM_LIMIT),
        name="gdn_scan",
    )(egl_flat, u3, wq4, kq4, gz3, norm_g)


FFN_FC = 256


def _ffn_kernel(x_ref, oa_ref, og_ref, wa_ref, wg_ref, gffn_ref, wup_ref, cw_ref, cb_ref, wd_ref, g_ref, out_ref,
                ubuf_ref, carry_ref, acc_ref, *, tm, tiles_per_seq):
    i = pl.program_id(0)
    halo = SUBLANES
    first = (i % tiles_per_seq) == 0
    x1 = (x_ref[...] + jnp.dot(oa_ref[...], wa_ref[...], preferred_element_type=F32)
          + jnp.dot(og_ref[...], wg_ref[...], preferred_element_type=F32))
    acc_ref[...] = x1
    h2 = _rms(x1, gffn_ref[...]).astype(BF16)

    @pl.when(i == 0)
    def _():
        carry_ref[...] = jnp.zeros_like(carry_ref)

    nslab = D_FF // FFN_FC
    col_of = lambda cidx, part: part * D_FF + cidx * FFN_FC
    slot_of = lambda cidx, part: 2 * (cidx % 2) + part

    def up(cidx):
        for part in range(2):
            lo, slot = col_of(cidx, part), slot_of(cidx, part)
            u = jnp.dot(h2, wup_ref[:, lo:lo + FFN_FC], preferred_element_type=F32)
            prev = carry_ref[:, lo:lo + FFN_FC]
            ubuf_ref[slot, 0:halo, :] = jnp.where(first, jnp.zeros_like(prev), prev)
            ubuf_ref[slot, halo:halo + tm, :] = u
            carry_ref[:, lo:lo + FFN_FC] = u[tm - halo:tm, :]

    def conv(cidx, part):
        lo, slot = col_of(cidx, part), slot_of(cidx, part)
        cw = cw_ref[:, lo:lo + FFN_FC]
        return (cw[0:1] * ubuf_ref[slot, halo - 2:halo - 2 + tm, :]
                + cw[1:2] * ubuf_ref[slot, halo - 1:halo - 1 + tm, :]
                + cw[2:3] * ubuf_ref[slot, halo:halo + tm, :] + cb_ref[:, lo:lo + FFN_FC])

    def down(cidx, act):
        acc_ref[...] += jnp.dot(act, wd_ref[cidx * FFN_FC:(cidx + 1) * FFN_FC, :], preferred_element_type=F32)

    act_prev = None
    for slot in range(nslab + 2):
        if slot < nslab:
            up(slot)
        act_cur = (_silu(conv(slot - 1, 0)) * conv(slot - 1, 1)).astype(BF16) if 1 <= slot <= nslab else None
        if slot >= 2:
            down(slot - 2, act_prev)
        act_prev = act_cur

    out_ref[...] = _rms(acc_ref[...], g_ref[...])


def _ffn(x2, oa2, og2, wo_a, wo_g, g_ffn, w_up, conv_w, conv_b, w_down, g, tm, tiles_per_seq):
    m = x2.shape[0]
    row = lambda i: (i, 0)
    fixed = lambda i: (0, 0)
    once = pl.Buffered(1)
    return pl.pallas_call(
        functools.partial(_ffn_kernel, tm=tm, tiles_per_seq=tiles_per_seq),
        grid=(m // tm,),
        in_specs=[pl.BlockSpec((tm, D_MODEL), row),
                  pl.BlockSpec((tm, DA_WIDTH), row),
                  pl.BlockSpec((tm, GDN_WIDTH), row),
                  pl.BlockSpec((DA_WIDTH, D_MODEL), fixed, pipeline_mode=once),
                  pl.BlockSpec((GDN_WIDTH, D_MODEL), fixed, pipeline_mode=once),
                  pl.BlockSpec((1, D_MODEL), fixed),
                  pl.BlockSpec((D_MODEL, 2 * D_FF), fixed, pipeline_mode=once),
                  pl.BlockSpec((FFN_CONV, 2 * D_FF), fixed),
                  pl.BlockSpec((1, 2 * D_FF), fixed),
                  pl.BlockSpec((D_FF, D_MODEL), fixed, pipeline_mode=once),
                  pl.BlockSpec((1, D_MODEL), fixed)],
        out_specs=pl.BlockSpec((tm, D_MODEL), row),
        out_shape=jax.ShapeDtypeStruct((m, D_MODEL), F32),
        scratch_shapes=[pltpu.VMEM((4, tm + SUBLANES, FFN_FC), F32),
                        pltpu.VMEM((SUBLANES, 2 * D_FF), F32),
                        pltpu.VMEM((tm, D_MODEL), F32)],
        compiler_params=pltpu.CompilerParams(dimension_semantics=("arbitrary",),
                                             vmem_limit_bytes=VMEM_LIMIT),
        name="ffn",
    )(x2, oa2, og2, wo_a, wo_g, g_ffn, w_up, conv_w, conv_b, w_down, g)


def _pad_lanes(v):
    return jnp.zeros((1, LANES), F32).at[0, :v.shape[0]].set(v.astype(F32))


def kernel(x, attn_norm_g, w_in, da_lambda_q1, da_lambda_k1, da_lambda_q2, da_lambda_k2, da_subln_g,
           gdn_conv_w, gdn_a_log, gdn_dt_bias, gdn_norm_g, w_out, ffn_norm_g, w_up, ffn_conv_w,
           ffn_conv_b, w_down, final_norm_g):
    b, s, d = x.shape
    m = b * s
    tm = 512
    assert d == D_MODEL and s % tm == 0 and w_in.shape[0] == 1, (x.shape, w_in.shape)
    l = 0
    x2 = x.reshape(m, d)

    wi = w_in[l]
    w_main = wi[:, :W_MAIN_COLS].astype(BF16)
    w_z = wi[:, W_MAIN_COLS + 2 * GDN_HEADS:].astype(BF16)
    w_ab = jnp.pad(wi[:, W_MAIN_COLS:W_MAIN_COLS + 2 * GDN_HEADS], ((0, 0), (0, LANES - 2 * GDN_HEADS))).astype(BF16)
    lam_p = jnp.stack([da_lambda_q1[l], da_lambda_k1[l], da_lambda_q2[l], da_lambda_k2[l]]).astype(F32)

    qk, va, gqkv, gz, gate = _in_proj(x2, attn_norm_g[l].reshape(1, d).astype(F32), w_main, w_z, w_ab,
                                      gdn_conv_w[l].astype(F32), _pad_lanes(gdn_a_log[l]),
                                      _pad_lanes(gdn_dt_bias[l]), tm, s // tm)

    oa = _diff_attn(lam_p, da_subln_g[l].reshape(1, DA_V_DIM).astype(F32),
                    qk.reshape(b, s, 2 * DA_WIDTH), va.reshape(b, s, 2 * DA_WIDTH), 512)

    u, wq, kq, egl = _gdn_prep(gqkv, gate, 256)
    nchunks = s // GDN_CHUNK
    egl_flat = egl.reshape(b, nchunks, GDN_CHUNK, LANES)[:, :, 0, :GDN_HEADS].reshape(-1)
    og = _gdn_scan(egl_flat, u.reshape(b, s, GDN_WIDTH),
                   wq.reshape(GDN_HEADS, b, nchunks * GDN_WQ_ROWS, GDN_HEAD_DIM),
                   kq.reshape(GDN_HEADS, b, nchunks * GDN_KQ_ROWS, GDN_CHUNK),
                   gz.reshape(b, s, GDN_WIDTH), gdn_norm_g[l].reshape(1, GDN_HEAD_DIM).astype(F32), 4)

    wo = w_out[l].astype(BF16)
    out = _ffn(x2, oa.reshape(m, DA_WIDTH), og.reshape(m, GDN_WIDTH), wo[:DA_WIDTH], wo[DA_WIDTH:],
               ffn_norm_g[l].reshape(1, d).astype(F32), w_up[l].astype(BF16), ffn_conv_w[l].astype(F32),
               ffn_conv_b[l].reshape(1, 2 * D_FF).astype(F32), w_down[l].astype(BF16),
               final_norm_g.reshape(1, d).astype(F32), tm, s // tm)
    return out.reshape(b, s, d)
```

```python
import functools
import math

import jax
import jax.numpy as jnp
from jax import lax
from jax.experimental import pallas as pl
from jax.experimental.pallas import tpu as pltpu

F32 = jnp.float32
BF16 = jnp.bfloat16

EPS = 1e-6
LOG2E = 1.4426950408889634
NEG_BIG = -1e30

D_MODEL = 1024
DA_HEADS = 4
DA_HEAD_DIM = 64
DA_V_DIM = 128
DA_WIDTH = DA_HEADS * DA_V_DIM
GDN_HEADS = 4
GDN_HEAD_DIM = 128
GDN_WIDTH = GDN_HEADS * GDN_HEAD_DIM
GDN_CONV = 4
GDN_CHUNK = 64
D_FF = 2816
FFN_CONV = 3
LAM_INIT = 0.8 - 0.6 * math.exp(-0.3 * 0)

LANES = 128
SUBLANES = 8
VMEM_LIMIT = 56 * 1024 * 1024

W_MAIN_COLS = 3 * DA_WIDTH + 3 * GDN_WIDTH


def _mm(a, b):
    return jnp.dot(a.astype(BF16), b.astype(BF16), preferred_element_type=F32)


def _mm_nt(a, b):
    return lax.dot_general(a.astype(BF16), b.astype(BF16), (((1,), (1,)), ((), ())),
                           preferred_element_type=F32)


def _sum3(ones_bf16, x, left=True):
    hi = x.astype(BF16)
    r1 = x - hi.astype(F32)
    mid = r1.astype(BF16)
    lo = (r1 - mid.astype(F32)).astype(BF16)
    if left:
        dot = lambda t: jnp.dot(ones_bf16, t, preferred_element_type=F32)
    else:
        dot = lambda t: jnp.dot(t, ones_bf16, preferred_element_type=F32)
    return dot(hi) + dot(mid) + dot(lo)


def _silu(x):
    hx = 0.5 * x
    return hx + hx * jnp.tanh(hx)


def _rms(x, g):
    return x * lax.rsqrt(jnp.mean(x * x, axis=-1, keepdims=True) + EPS) * g


def _in_proj_kernel(x_ref, g_ref, w_ref, wz_ref, wab_ref, cw_ref, alog_ref, dtb_ref,
                    qk_ref, va_ref, gqkv_ref, gz_ref, gate_ref, cbuf_ref, *, tm, tiles_per_seq):
    i = pl.program_id(0)
    halo = SUBLANES
    hb = _rms(x_ref[...], g_ref[...]).astype(BF16)

    def proj(lo, width, ref=w_ref):
        return jnp.dot(hb, ref[:, lo:lo + width], preferred_element_type=F32)

    lane = lax.broadcasted_iota(jnp.int32, (tm, LANES), 1)
    base = 3 * DA_WIDTH
    half = DA_WIDTH // 2

    @pl.when(i % tiles_per_seq == 0)
    def _():
        cbuf_ref[0:halo, :] = jnp.zeros((halo, 3 * GDN_WIDTH), F32)

    def gdn_part(c):
        cbuf_ref[halo:halo + tm, c * half:(c + 1) * half] = proj(base + c * half, half)

    def q_part(c):
        qk_ref[:, c * half:(c + 1) * half] = (proj(c * half, half) * (DA_HEAD_DIM ** -0.5 * LOG2E)).astype(BF16)

    def k_part(c):
        qk_ref[:, DA_WIDTH + c * half:DA_WIDTH + (c + 1) * half] = proj(DA_WIDTH + c * half, half).astype(BF16)

    def v_part(c):
        v = proj(2 * DA_WIDTH + c * half, half)
        ones_col = jnp.ones((tm, LANES), BF16)
        for hh in range(2):
            h = 2 * c + hh
            va_ref[:, 2 * h * LANES:(2 * h + 1) * LANES] = v[:, hh * DA_V_DIM:(hh + 1) * DA_V_DIM].astype(BF16)
            va_ref[:, (2 * h + 1) * LANES:(2 * h + 2) * LANES] = ones_col

    def z_part(c):
        gz_ref[:, c * half:(c + 1) * half] = proj(c * half, half, wz_ref)

    def gate_part(_):
        gab = proj(0, LANES, wab_ref)
        sp_in = gab + dtb_ref[...]
        softplus = jnp.maximum(sp_in, 0.0) + jnp.log(1.0 + jnp.exp(-jnp.abs(sp_in)))
        gate_ref[...] = jnp.where(lane < GDN_HEADS, -jnp.exp(alog_ref[...]) * softplus,
                                  1.0 / (1.0 + jnp.exp(-gab)))

    def conv_group(c):
        cols = slice(c * GDN_HEAD_DIM, (c + 1) * GDN_HEAD_DIM)
        cw = cw_ref[:, cols]
        xc = cw[0:1] * cbuf_ref[halo - 3:halo - 3 + tm, cols]
        for j in range(1, GDN_CONV):
            xc = xc + cw[j:j + 1] * cbuf_ref[halo - 3 + j:halo - 3 + j + tm, cols]
        xc = _silu(xc)
        if c < GDN_HEADS:
            xc = xc * (lax.rsqrt(jnp.sum(xc * xc, axis=-1, keepdims=True) + EPS) * GDN_HEAD_DIM ** -0.5)
        elif c < 2 * GDN_HEADS:
            xc = xc * lax.rsqrt(jnp.sum(xc * xc, axis=-1, keepdims=True) + EPS)
        gqkv_ref[:, cols] = xc

    matmul_tasks = ([(gdn_part, c) for c in range(6)] + [(q_part, 0), (q_part, 1), (k_part, 0), (k_part, 1),
                    (v_part, 0), (v_part, 1), (z_part, 0), (z_part, 1), (gate_part, 0)])
    ngroups = 3 * GDN_HEADS
    issued = 0
    for c in range(ngroups):
        want = min(len(matmul_tasks), 2 + (c * len(matmul_tasks)) // ngroups)
        while issued < want:
            fn, arg = matmul_tasks[issued]
            fn(arg)
            issued += 1
        conv_group(c)
    for fn, arg in matmul_tasks[issued:]:
        fn(arg)
    cbuf_ref[0:halo, :] = cbuf_ref[tm:tm + halo, :]


def _in_proj(x2, g, w_main, w_z, w_ab, conv_w, alog_row, dtb_row, tm, tiles_per_seq):
    m = x2.shape[0]
    row = lambda i: (i, 0)
    fixed = lambda i: (0, 0)
    return pl.pallas_call(
        functools.partial(_in_proj_kernel, tm=tm, tiles_per_seq=tiles_per_seq),
        grid=(m // tm,),
        in_specs=[pl.BlockSpec((tm, D_MODEL), row),
                  pl.BlockSpec((1, D_MODEL), fixed),
                  pl.BlockSpec((D_MODEL, W_MAIN_COLS), fixed),
                  pl.BlockSpec((D_MODEL, GDN_WIDTH), fixed),
                  pl.BlockSpec((D_MODEL, LANES), fixed),
                  pl.BlockSpec((GDN_CONV, 3 * GDN_WIDTH), fixed),
                  pl.BlockSpec((1, LANES), fixed),
                  pl.BlockSpec((1, LANES), fixed)],
        out_specs=[pl.BlockSpec((tm, 2 * DA_WIDTH), row),
                   pl.BlockSpec((tm, 2 * DA_WIDTH), row),
                   pl.BlockSpec((tm, 3 * GDN_WIDTH), row),
                   pl.BlockSpec((tm, GDN_WIDTH), row),
                   pl.BlockSpec((tm, LANES), row)],
        out_shape=[jax.ShapeDtypeStruct((m, 2 * DA_WIDTH), BF16),
                   jax.ShapeDtypeStruct((m, 2 * DA_WIDTH), BF16),
                   jax.ShapeDtypeStruct((m, 3 * GDN_WIDTH), F32),
                   jax.ShapeDtypeStruct((m, GDN_WIDTH), F32),
                   jax.ShapeDtypeStruct((m, LANES), F32)],
        scratch_shapes=[pltpu.VMEM((tm + 2 * SUBLANES, 3 * GDN_WIDTH), F32)],
        compiler_params=pltpu.CompilerParams(dimension_semantics=("arbitrary",),
                                             vmem_limit_bytes=VMEM_LIMIT),
        name="in_proj",
    )(x2, g, w_main, w_z, w_ab, conv_w, alog_row, dtb_row)


ATTN_HEADS_PER_STEP = 4


def _attn_kernel(lam_ref, sg_ref, q_ref, k_ref, v_ref, o_ref, acc_ref, m_ref, *, t, hp):
    hg = pl.program_id(1)
    qi = pl.program_id(2)

    lane = lax.broadcasted_iota(jnp.int32, (t, LANES), 1)
    col_f = lax.broadcasted_iota(jnp.int32, (1, t), 1).astype(F32)
    row_i = lax.broadcasted_iota(jnp.int32, (t, t), 0)
    col_i = lax.broadcasted_iota(jnp.int32, (t, t), 1)

    qm, slope_row = [], []
    for j in range(hp):
        q = q_ref[0, :, j * LANES:(j + 1) * LANES]
        zero = jnp.zeros_like(q)
        qm += [jnp.where(lane < DA_HEAD_DIM, q, zero), jnp.where(lane >= DA_HEAD_DIM, q, zero)]
        hf = (hg * hp + j + 1).astype(F32)
        slope_row.append(jnp.exp2(jnp.full((1, t), -8.0 / DA_HEADS, F32) * hf) * LOG2E)

    acc_ref[...] = jnp.zeros_like(acc_ref)

    nchain = 2 * hp

    m_ref[...] = jnp.full(m_ref.shape, NEG_BIG, F32)
    lane_tiles = lambda x, n: jnp.concatenate([x] * n, axis=1)

    def step(blocks):
        ms = [m_ref[idx] for idx in range(nchain)]
        work = [(ki, masked, idx) for ki, masked in blocks for idx in range(nchain)]

        def logits(ki, masked, idx):
            j = idx // 2
            k = k_ref[0, pl.ds(pl.multiple_of(ki * t, t), t), j * LANES:(j + 1) * LANES]
            pos = ((ki - qi) * t).astype(F32) + col_f
            s = (lax.dot_general(qm[idx], k, (((1,), (1,)), ((), ())), preferred_element_type=F32)
                 + pos * slope_row[j])
            return jnp.where(col_i <= row_i, s, NEG_BIG) if masked else s

        def probs(idx, s):
            m_new = jnp.maximum(ms[idx], jnp.max(s, axis=-1, keepdims=True))
            return m_new, jnp.exp2(s - lane_tiles(m_new, t // LANES)).astype(BF16)

        def accumulate(ki, idx, m_new, p):
            j = idx // 2
            v = v_ref[0, pl.ds(pl.multiple_of(ki * t, t), t), 2 * j * LANES:2 * (j + 1) * LANES]
            acc_ref[idx] = (lane_tiles(jnp.exp2(ms[idx] - m_new), 2) * acc_ref[idx]
                            + jnp.dot(p, v, preferred_element_type=F32))
            ms[idx] = m_new

        s_prev, mp_prev = None, None
        for slot in range(len(work) + 2):
            s_cur = logits(*work[slot]) if slot < len(work) else None
            mp_cur = probs(work[slot - 1][2], s_prev) if 1 <= slot <= len(work) else None
            if slot >= 2:
                accumulate(work[slot - 2][0], work[slot - 2][2], *mp_prev)
            s_prev, mp_prev = s_cur, mp_cur
        for idx in range(nchain):
            m_ref[idx] = ms[idx]

    npair = qi // 2
    odd = qi - 2 * npair

    def run(trips, blocks_of):
        def body(j, carry):
            step(blocks_of(j))
            return carry
        lax.fori_loop(0, trips, body, 0)

    run(npair, lambda j: [(2 * j, False), (2 * j + 1, False)])
    run(odd, lambda j: [(qi - 1, False), (qi, True)])
    run(1 - odd, lambda j: [(qi, True)])

    lam_p = lam_ref[...]
    lam = (jnp.exp(jnp.sum(lam_p[0:1] * lam_p[1:2], axis=-1, keepdims=True))
           - jnp.exp(jnp.sum(lam_p[2:3] * lam_p[3:4], axis=-1, keepdims=True)) + LAM_INIT)
    ones_sq = jnp.ones((DA_V_DIM, DA_V_DIM), BF16)
    for j in range(hp):
        a1 = acc_ref[2 * j]
        a2 = acc_ref[2 * j + 1]
        o = a1[:, :DA_V_DIM] / a1[:, DA_V_DIM:] - lam * (a2[:, :DA_V_DIM] / a2[:, DA_V_DIM:])
        o_ref[0, :, j * DA_V_DIM:(j + 1) * DA_V_DIM] = (
            o * lax.rsqrt(_sum3(ones_sq, o * o, left=False) * (1.0 / DA_V_DIM) + EPS)
            * sg_ref[...] * (1.0 - LAM_INIT)).astype(BF16)


def _diff_attn(lam_p, subln_g, qk3, va3, t):
    b, s, _ = qk3.shape
    hp = ATTN_HEADS_PER_STEP
    ngroups = DA_HEADS // hp
    return pl.pallas_call(
        functools.partial(_attn_kernel, t=t, hp=hp),
        grid=(b, ngroups, s // t),
        in_specs=[pl.BlockSpec((4, DA_HEAD_DIM), lambda bi, hg, qi: (0, 0)),
                  pl.BlockSpec((1, DA_V_DIM), lambda bi, hg, qi: (0, 0)),
                  pl.BlockSpec((1, t, hp * LANES), lambda bi, hg, qi: (bi, qi, hg)),
                  pl.BlockSpec((1, s, hp * LANES), lambda bi, hg, qi: (bi, 0, ngroups + hg),
                               pipeline_mode=pl.Buffered(1)),
                  pl.BlockSpec((1, s, 2 * hp * LANES), lambda bi, hg, qi: (bi, 0, hg),
                               pipeline_mode=pl.Buffered(1))],
        out_specs=pl.BlockSpec((1, t, hp * DA_V_DIM), lambda bi, hg, qi: (bi, qi, hg)),
        out_shape=jax.ShapeDtypeStruct((b, s, DA_WIDTH), BF16),
        scratch_shapes=[pltpu.VMEM((2 * hp, t, 2 * LANES), F32),
                        pltpu.VMEM((2 * hp, t, LANES), F32)],
        compiler_params=pltpu.CompilerParams(dimension_semantics=("arbitrary", "arbitrary", "arbitrary"),
                                             vmem_limit_bytes=VMEM_LIMIT),
        name="diff_attn",
    )(lam_p, subln_g, qk3, qk3, va3)


GDN_GROUP = 2 * GDN_CHUNK
GDN_WQ_ROWS = 2 * GDN_CHUNK
GDN_KQ_ROWS = GDN_HEAD_DIM + GDN_CHUNK


def _gdn_prep_kernel(gqkv_ref, gate_ref, u_ref, wq_ref, kq_ref, egl_ref, *, rb):
    c = GDN_CHUNK
    r = GDN_GROUP
    ri = lax.broadcasted_iota(jnp.int32, (r, r), 0)
    ci = lax.broadcasted_iota(jnp.int32, (r, r), 1)
    rx = ri ^ ci
    same_chunk = rx < c
    lower = ci <= ri
    strict = ci < ri
    eye = jnp.where(ci == ri, 1.0, 0.0).astype(F32)
    tri_ones = jnp.where(same_chunk, jnp.where(lower, 1.0, 0.0), 0.0).astype(BF16)
    blk_ones = jnp.where(same_chunk, 1.0, 0.0).astype(BF16)

    ngroups = rb // r
    chains = [(gidx, h) for gidx in range(ngroups) for h in range(GDN_HEADS)]
    each = lambda f: [f(j) for j in range(len(chains))]
    rows_of = lambda j: slice(chains[j][0] * r, (chains[j][0] + 1) * r)
    head_of = lambda j: chains[j][1]

    gate = [gate_ref[gidx * r:(gidx + 1) * r, :] for gidx in range(ngroups)]
    gc_all = [_sum3(tri_ones, gt) for gt in gate]
    gl_all = [_sum3(blk_ones, gt) for gt in gate]
    gc_t = [jnp.transpose(x) for x in gc_all]
    for gidx in range(ngroups):
        egl_ref[gidx * r:(gidx + 1) * r, :] = jnp.exp(gl_all[gidx])

    def hslice(j, part):
        lo = part * GDN_WIDTH + head_of(j) * GDN_HEAD_DIM
        return gqkv_ref[rows_of(j), lo:lo + GDN_HEAD_DIM]

    qh = each(lambda j: hslice(j, 0))
    kh = each(lambda j: hslice(j, 1))
    vh = each(lambda j: hslice(j, 2))
    beta = each(lambda j: gate[chains[j][0]][:, GDN_HEADS + head_of(j):GDN_HEADS + head_of(j) + 1])
    gc = each(lambda j: gc_all[chains[j][0]][:, head_of(j):head_of(j) + 1])
    gl = each(lambda j: gl_all[chains[j][0]][:, head_of(j):head_of(j) + 1])
    gc_row = each(lambda j: gc_t[chains[j][0]][head_of(j):head_of(j) + 1, :])

    decay = each(lambda j: jnp.exp(jnp.where(same_chunk, jnp.where(lower, gc[j] - gc_row[j], -jnp.inf), -jnp.inf)))
    kb = each(lambda j: kh[j] * beta[j])
    vb = each(lambda j: vh[j] * beta[j])
    kbf = each(lambda j: kh[j].astype(BF16))
    kk = each(lambda j: _mm_nt(kb[j], kbf[j]))
    qk = each(lambda j: _mm_nt(qh[j], kbf[j]))
    lmat = each(lambda j: jnp.where(strict, kk[j] * decay[j], 0.0))
    qk = each(lambda j: (qk[j] * decay[j]).astype(BF16))

    x1 = each(lambda j: jnp.where(rx < 16, -lmat[j], 0.0))
    p = each(lambda j: eye + x1[j])
    x2 = each(lambda j: _mm(x1[j], x1[j]))
    x2p = each(lambda j: _mm(x2[j], p[j]))
    x4 = each(lambda j: _mm(x2[j], x2[j]))
    p = each(lambda j: p[j] + x2p[j])
    x4p = each(lambda j: _mm(x4[j], p[j]))
    x8 = each(lambda j: _mm(x4[j], x4[j]))
    p = each(lambda j: p[j] + x4p[j])
    x8p = each(lambda j: _mm(x8[j], p[j]))
    p = each(lambda j: p[j] + x8p[j])
    b32 = each(lambda j: jnp.where(rx < 32, jnp.where(rx >= 16, lmat[j], 0.0), 0.0))
    t1 = each(lambda j: _mm(b32[j], p[j]))
    t2 = each(lambda j: _mm(p[j], t1[j]))
    p = each(lambda j: p[j] - t2[j])
    b64 = each(lambda j: jnp.where(rx >= 32, lmat[j], 0.0))
    t1 = each(lambda j: _mm(b64[j], p[j]))
    t2 = each(lambda j: _mm(p[j], t1[j]))
    tmat = each(lambda j: p[j] - t2[j])

    eg = each(lambda j: jnp.exp(gc[j]))
    uw = each(lambda j: _mm(tmat[j], jnp.concatenate([vb[j], kb[j] * eg[j]], axis=1)))
    qdec = each(lambda j: (qh[j] * eg[j]).astype(BF16))
    kdec_t = each(lambda j: jnp.transpose(kh[j] * jnp.exp(gl[j] - gc[j])).astype(BF16))
    for j, (gidx, h) in enumerate(chains):
        u_ref[rows_of(j), h * GDN_HEAD_DIM:(h + 1) * GDN_HEAD_DIM] = uw[j][:, :GDN_HEAD_DIM]
        w = uw[j][:, GDN_HEAD_DIM:].astype(BF16)
        for n in range(r // c):
            ch = gidx * (r // c) + n
            cr = slice(n * c, (n + 1) * c)
            wq_ref[h, ch * GDN_WQ_ROWS:ch * GDN_WQ_ROWS + c, :] = w[cr]
            wq_ref[h, ch * GDN_WQ_ROWS + c:(ch + 1) * GDN_WQ_ROWS, :] = qdec[j][cr]
            kq_ref[h, ch * GDN_KQ_ROWS:ch * GDN_KQ_ROWS + GDN_HEAD_DIM, :] = kdec_t[j][:, cr]
            kq_ref[h, ch * GDN_KQ_ROWS + GDN_HEAD_DIM:(ch + 1) * GDN_KQ_ROWS, :] = qk[j][cr, cr]


def _gdn_prep(gqkv, gate, rb):
    m = gqkv.shape[0]
    nch = rb // GDN_CHUNK
    row = lambda i: (i, 0)
    hrow = lambda i: (0, i, 0)
    return pl.pallas_call(
        functools.partial(_gdn_prep_kernel, rb=rb),
        grid=(m // rb,),
        in_specs=[pl.BlockSpec((rb, 3 * GDN_WIDTH), row),
                  pl.BlockSpec((rb, LANES), row)],
        out_specs=[pl.BlockSpec((rb, GDN_WIDTH), row),
                   pl.BlockSpec((GDN_HEADS, nch * GDN_WQ_ROWS, GDN_HEAD_DIM), hrow),
                   pl.BlockSpec((GDN_HEADS, nch * GDN_KQ_ROWS, GDN_CHUNK), hrow),
                   pl.BlockSpec((rb, LANES), row)],
        out_shape=[jax.ShapeDtypeStruct((m, GDN_WIDTH), F32),
                   jax.ShapeDtypeStruct((GDN_HEADS, m // GDN_CHUNK * GDN_WQ_ROWS, GDN_HEAD_DIM), BF16),
                   jax.ShapeDtypeStruct((GDN_HEADS, m // GDN_CHUNK * GDN_KQ_ROWS, GDN_CHUNK), BF16),
                   jax.ShapeDtypeStruct((m, LANES), F32)],
        compiler_params=pltpu.CompilerParams(dimension_semantics=("arbitrary",),
                                             vmem_limit_bytes=VMEM_LIMIT),
        name="gdn_prep",
    )(gqkv, gate)


def _gdn_scan_kernel(egl_ref, u_ref, wq_ref, kq_ref, gz_ref, ng_ref, og_ref, state_ref, *, nb, g):
    i = pl.program_id(0)
    c = GDN_CHUNK

    @pl.when(i == 0)
    def _():
        state_ref[...] = jnp.zeros_like(state_ref)

    chains = [(b, h) for b in range(nb) for h in range(GDN_HEADS)]
    each = lambda f: [f(j, *chains[j]) for j in range(len(chains))]
    hcols = lambda h: slice(h * GDN_HEAD_DIM, (h + 1) * GDN_HEAD_DIM)
    states = each(lambda j, b, h: state_ref[j])
    for n in range(g):
        rows = slice(n * c, (n + 1) * c)
        ws_qs = each(lambda j, b, h: jnp.dot(wq_ref[h, b, n * GDN_WQ_ROWS:(n + 1) * GDN_WQ_ROWS, :],
                                             states[j].astype(BF16),
                                             preferred_element_type=F32))
        v_new = each(lambda j, b, h: (u_ref[b, rows, hcols(h)] - ws_qs[j][:c]).astype(BF16))
        kv_qv = each(lambda j, b, h: jnp.dot(kq_ref[h, b, n * GDN_KQ_ROWS:(n + 1) * GDN_KQ_ROWS, :], v_new[j],
                                             preferred_element_type=F32))
        states = each(lambda j, b, h: states[j] * egl_ref[b, n * c:n * c + 1, h:h + 1] + kv_qv[j][:GDN_HEAD_DIM])
        for j, (b, h) in enumerate(chains):
            o = _rms(ws_qs[j][c:] + kv_qv[j][GDN_HEAD_DIM:], ng_ref[...])
            og_ref[b, rows, hcols(h)] = (o * _silu(gz_ref[b, rows, hcols(h)])).astype(BF16)
    for j in range(len(chains)):
        state_ref[j] = states[j]


def _gdn_scan(egl3, u3, wq4, kq4, gz3, norm_g, g):
    nb, s, _ = u3.shape
    nchunks = s // GDN_CHUNK
    blk = lambda i: (0, i, 0)
    hblk = lambda i: (0, 0, i, 0)
    return pl.pallas_call(
        functools.partial(_gdn_scan_kernel, nb=nb, g=g),
        grid=(nchunks // g,),
        in_specs=[pl.BlockSpec((nb, g * GDN_CHUNK, LANES), blk),
                  pl.BlockSpec((nb, g * GDN_CHUNK, GDN_WIDTH), blk),
                  pl.BlockSpec((GDN_HEADS, nb, g * GDN_WQ_ROWS, GDN_HEAD_DIM), hblk),
                  pl.BlockSpec((GDN_HEADS, nb, g * GDN_KQ_ROWS, GDN_CHUNK), hblk),
                  pl.BlockSpec((nb, g * GDN_CHUNK, GDN_WIDTH), blk),
                  pl.BlockSpec((1, GDN_HEAD_DIM), lambda i: (0, 0))],
        out_specs=pl.BlockSpec((nb, g * GDN_CHUNK, GDN_WIDTH), blk),
        out_shape=jax.ShapeDtypeStruct((nb, s, GDN_WIDTH), BF16),
        scratch_shapes=[pltpu.VMEM((nb * GDN_HEADS, GDN_HEAD_DIM, GDN_HEAD_DIM), F32)],
        compiler_params=pltpu.CompilerParams(dimension_semantics=("arbitrary",),
                                             vmem_limit_bytes=VMEM_LIMIT),
        name="gdn_scan",
    )(egl3, u3, wq4, kq4, gz3, norm_g)


FFN_FC = 256


def _ffn_kernel(x_ref, oa_ref, og_ref, wa_ref, wg_ref, gffn_ref, wup_ref, cw_ref, cb_ref, wd_ref, g_ref, out_ref,
                ubuf_ref, carry_ref, acc_ref, *, tm, tiles_per_seq):
    i = pl.program_id(0)
    halo = SUBLANES
    first = (i % tiles_per_seq) == 0
    x1 = (x_ref[...] + jnp.dot(oa_ref[...], wa_ref[...], preferred_element_type=F32)
          + jnp.dot(og_ref[...], wg_ref[...], preferred_element_type=F32))
    acc_ref[...] = x1
    h2 = _rms(x1, gffn_ref[...]).astype(BF16)

    @pl.when(i == 0)
    def _():
        carry_ref[...] = jnp.zeros_like(carry_ref)

    nslab = D_FF // FFN_FC
    col_of = lambda cidx, part: part * D_FF + cidx * FFN_FC
    slot_of = lambda cidx, part: 2 * (cidx % 2) + part

    def up(cidx):
        for part in range(2):
            lo, slot = col_of(cidx, part), slot_of(cidx, part)
            u = jnp.dot(h2, wup_ref[:, lo:lo + FFN_FC], preferred_element_type=F32)
            prev = carry_ref[:, lo:lo + FFN_FC]
            ubuf_ref[slot, 0:halo, :] = jnp.where(first, jnp.zeros_like(prev), prev)
            ubuf_ref[slot, halo:halo + tm, :] = u
            carry_ref[:, lo:lo + FFN_FC] = u[tm - halo:tm, :]

    def conv(cidx, part):
        lo, slot = col_of(cidx, part), slot_of(cidx, part)
        cw = cw_ref[:, lo:lo + FFN_FC]
        return (cw[0:1] * ubuf_ref[slot, halo - 2:halo - 2 + tm, :]
                + cw[1:2] * ubuf_ref[slot, halo - 1:halo - 1 + tm, :]
                + cw[2:3] * ubuf_ref[slot, halo:halo + tm, :] + cb_ref[:, lo:lo + FFN_FC])

    def down(cidx, act):
        acc_ref[...] += jnp.dot(act, wd_ref[cidx * FFN_FC:(cidx + 1) * FFN_FC, :], preferred_element_type=F32)

    act_prev = None
    for slot in range(nslab + 2):
        if slot < nslab:
            up(slot)
        act_cur = (_silu(conv(slot - 1, 0)) * conv(slot - 1, 1)).astype(BF16) if 1 <= slot <= nslab else None
        if slot >= 2:
            down(slot - 2, act_prev)
        act_prev = act_cur

    out_ref[...] = _rms(acc_ref[...], g_ref[...])


def _ffn(x2, oa2, og2, wo_a, wo_g, g_ffn, w_up, conv_w, conv_b, w_down, g, tm, tiles_per_seq):
    m = x2.shape[0]
    row = lambda i: (i, 0)
    fixed = lambda i: (0, 0)
    once = pl.Buffered(1)
    return pl.pallas_call(
        functools.partial(_ffn_kernel, tm=tm, tiles_per_seq=tiles_per_seq),
        grid=(m // tm,),
        in_specs=[pl.BlockSpec((tm, D_MODEL), row),
                  pl.BlockSpec((tm, DA_WIDTH), row),
                  pl.BlockSpec((tm, GDN_WIDTH), row),
                  pl.BlockSpec((DA_WIDTH, D_MODEL), fixed, pipeline_mode=once),
                  pl.BlockSpec((GDN_WIDTH, D_MODEL), fixed, pipeline_mode=once),
                  pl.BlockSpec((1, D_MODEL), fixed),
                  pl.BlockSpec((D_MODEL, 2 * D_FF), fixed, pipeline_mode=once),
                  pl.BlockSpec((FFN_CONV, 2 * D_FF), fixed),
                  pl.BlockSpec((1, 2 * D_FF), fixed),
                  pl.BlockSpec((D_FF, D_MODEL), fixed, pipeline_mode=once),
                  pl.BlockSpec((1, D_MODEL), fixed)],
        out_specs=pl.BlockSpec((tm, D_MODEL), row),
        out_shape=jax.ShapeDtypeStruct((m, D_MODEL), F32),
        scratch_shapes=[pltpu.VMEM((4, tm + SUBLANES, FFN_FC), F32),
                        pltpu.VMEM((SUBLANES, 2 * D_FF), F32),
                        pltpu.VMEM((tm, D_MODEL), F32)],
        compiler_params=pltpu.CompilerParams(dimension_semantics=("arbitrary",),
                                             vmem_limit_bytes=VMEM_LIMIT),
        name="ffn",
    )(x2, oa2, og2, wo_a, wo_g, g_ffn, w_up, conv_w, conv_b, w_down, g)


def _pad_lanes(v):
    return jnp.zeros((1, LANES), F32).at[0, :v.shape[0]].set(v.astype(F32))


def kernel(x, attn_norm_g, w_in, da_lambda_q1, da_lambda_k1, da_lambda_q2, da_lambda_k2, da_subln_g,
           gdn_conv_w, gdn_a_log, gdn_dt_bias, gdn_norm_g, w_out, ffn_norm_g, w_up, ffn_conv_w,
           ffn_conv_b, w_down, final_norm_g):
    b, s, d = x.shape
    m = b * s
    tm = 512
    assert d == D_MODEL and s % tm == 0 and w_in.shape[0] == 1, (x.shape, w_in.shape)
    l = 0
    x2 = x.reshape(m, d)

    wi = w_in[l]
    w_main = wi[:, :W_MAIN_COLS].astype(BF16)
    w_z = wi[:, W_MAIN_COLS + 2 * GDN_HEADS:].astype(BF16)
    w_ab = jnp.pad(wi[:, W_MAIN_COLS:W_MAIN_COLS + 2 * GDN_HEADS], ((0, 0), (0, LANES - 2 * GDN_HEADS))).astype(BF16)
    lam_p = jnp.stack([da_lambda_q1[l], da_lambda_k1[l], da_lambda_q2[l], da_lambda_k2[l]]).astype(F32)

    qk, va, gqkv, gz, gate = _in_proj(x2, attn_norm_g[l].reshape(1, d).astype(F32), w_main, w_z, w_ab,
                                      gdn_conv_w[l].astype(F32), _pad_lanes(gdn_a_log[l]),
                                      _pad_lanes(gdn_dt_bias[l]), tm, s // tm)

    oa = _diff_attn(lam_p, da_subln_g[l].reshape(1, DA_V_DIM).astype(F32),
                    qk.reshape(b, s, 2 * DA_WIDTH), va.reshape(b, s, 2 * DA_WIDTH), 512)

    u, wq, kq, egl = _gdn_prep(gqkv, gate, 512)
    nchunks = s // GDN_CHUNK
    og = _gdn_scan(egl.reshape(b, s, LANES), u.reshape(b, s, GDN_WIDTH),
                   wq.reshape(GDN_HEADS, b, nchunks * GDN_WQ_ROWS, GDN_HEAD_DIM),
                   kq.reshape(GDN_HEADS, b, nchunks * GDN_KQ_ROWS, GDN_CHUNK),
                   gz.reshape(b, s, GDN_WIDTH), gdn_norm_g[l].reshape(1, GDN_HEAD_DIM).astype(F32), 8)

    wo = w_out[l].astype(BF16)
    out = _ffn(x2, oa.reshape(m, DA_WIDTH), og.reshape(m, GDN_WIDTH), wo[:DA_WIDTH], wo[DA_WIDTH:],
               ffn_norm_g[l].reshape(1, d).astype(F32), w_up[l].astype(BF16), ffn_conv_w[l].astype(F32),
               ffn_conv_b[l].reshape(1, 2 * D_FF).astype(F32), w_down[l].astype(BF16),
               final_norm_g.reshape(1, d).astype(F32), tm, s // tm)
    return out.reshape(b, s, d)
```

```python
import functools
import math

import jax
import jax.numpy as jnp
from jax import lax
from jax.experimental import pallas as pl
from jax.experimental.pallas import tpu as pltpu

F32 = jnp.float32
BF16 = jnp.bfloat16

EPS = 1e-6
LOG2E = 1.4426950408889634
NEG_BIG = -1e30

D_MODEL = 1024
DA_HEADS = 4
DA_HEAD_DIM = 64
DA_V_DIM = 128
DA_WIDTH = DA_HEADS * DA_V_DIM
GDN_HEADS = 4
GDN_HEAD_DIM = 128
GDN_WIDTH = GDN_HEADS * GDN_HEAD_DIM
GDN_CONV = 4
GDN_CHUNK = 64
D_FF = 2816
FFN_CONV = 3
LAM_INIT = 0.8 - 0.6 * math.exp(-0.3 * 0)

LANES = 128
SUBLANES = 8
VMEM_LIMIT = 56 * 1024 * 1024

W_MAIN_COLS = 3 * DA_WIDTH + 3 * GDN_WIDTH


def _mm(a, b):
    return jnp.dot(a.astype(BF16), b.astype(BF16), preferred_element_type=F32)


def _mm_nt(a, b):
    return lax.dot_general(a.astype(BF16), b.astype(BF16), (((1,), (1,)), ((), ())),
                           preferred_element_type=F32)


def _sum3(ones_bf16, x, left=True):
    hi = x.astype(BF16)
    r1 = x - hi.astype(F32)
    mid = r1.astype(BF16)
    lo = (r1 - mid.astype(F32)).astype(BF16)
    if left:
        dot = lambda t: jnp.dot(ones_bf16, t, preferred_element_type=F32)
    else:
        dot = lambda t: jnp.dot(t, ones_bf16, preferred_element_type=F32)
    return dot(hi) + dot(mid) + dot(lo)


def _silu(x):
    hx = 0.5 * x
    return hx + hx * jnp.tanh(hx)


def _rms(x, g):
    return x * lax.rsqrt(jnp.mean(x * x, axis=-1, keepdims=True) + EPS) * g


PERM_PITCH = 72


def _permute_rows(scr_ref, a, nseg=SUBLANES):
    tm, ncols = a.shape
    seg = tm // nseg
    slabs = []
    for k in range(ncols // LANES):
        for s in range(nseg):
            scr_ref[k, s * PERM_PITCH:s * PERM_PITCH + seg, :] = a[s * seg:(s + 1) * seg, k * LANES:(k + 1) * LANES]
        slabs.append(jnp.concatenate(
            [scr_ref[k, pl.ds(j, nseg, stride=PERM_PITCH), :] for j in range(seg)], axis=0))
    return jnp.concatenate(slabs, axis=1)


def _unpermute_rows(scr_ref, ap, out_ref, col0=0, slab0=0, nseg=SUBLANES):
    tm, ncols = ap.shape
    seg = tm // nseg
    for k in range(ncols // LANES):
        for j in range(seg):
            scr_ref[slab0 + k, pl.ds(j, nseg, stride=PERM_PITCH), :] = (
                ap[j * nseg:(j + 1) * nseg, k * LANES:(k + 1) * LANES])
        for s in range(nseg):
            out_ref[s * seg:(s + 1) * seg, col0 + k * LANES:col0 + (k + 1) * LANES] = (
                scr_ref[slab0 + k, s * PERM_PITCH:s * PERM_PITCH + seg, :])


def _in_proj_kernel(x_ref, g_ref, w_ref, wz_ref, wab_ref, cw_ref, alog_ref, dtb_ref,
                    qk_ref, va_ref, gqkv_ref, gz_ref, gate_ref, cbuf_ref, carry_ref, perm_ref,
                    *, tm, tiles_per_seq):
    i = pl.program_id(0)
    tile = SUBLANES
    halo = (GDN_CONV - 1) * tile
    first = (i % tiles_per_seq) == 0
    h = _rms(x_ref[...], g_ref[...])
    hb = h.astype(BF16)
    hb_perm = _permute_rows(perm_ref, h).astype(BF16)

    def proj(lo, width, ref=w_ref, lhs=hb):
        return jnp.dot(lhs, ref[:, lo:lo + width], preferred_element_type=F32)

    lane = lax.broadcasted_iota(jnp.int32, (tm, LANES), 1)
    base = 3 * DA_WIDTH
    half = DA_WIDTH // 2
    row8 = lax.broadcasted_iota(jnp.int32, (tile, half), 0)

    @pl.when(i == 0)
    def _():
        carry_ref[...] = jnp.zeros_like(carry_ref)

    def gdn_part(c):
        cols = slice(c * half, (c + 1) * half)
        u = proj(base + c * half, half, lhs=hb_perm)
        prev = carry_ref[:, cols]
        prev = jnp.where(first, jnp.zeros_like(prev), prev)
        for d in range(1, GDN_CONV):
            cur_t = u[tm - d * tile:tm - (d - 1) * tile, :]
            prev_t = prev[halo - d * tile:halo - (d - 1) * tile, :]
            cbuf_ref[halo - d * tile:halo - (d - 1) * tile, cols] = jnp.where(
                row8 == 0, pltpu.roll(prev_t, 1, 0), pltpu.roll(cur_t, 1, 0))
        cbuf_ref[halo:halo + tm, cols] = u
        carry_ref[:, cols] = u[tm - halo:tm, :]

    def q_part(c):
        qk_ref[:, c * half:(c + 1) * half] = (proj(c * half, half) * (DA_HEAD_DIM ** -0.5 * LOG2E)).astype(BF16)

    def k_part(c):
        qk_ref[:, DA_WIDTH + c * half:DA_WIDTH + (c + 1) * half] = proj(DA_WIDTH + c * half, half).astype(BF16)

    def v_part(c):
        v = proj(2 * DA_WIDTH + c * half, half)
        ones_col = jnp.ones((tm, LANES), BF16)
        for hh in range(2):
            h = 2 * c + hh
            va_ref[:, 2 * h * LANES:(2 * h + 1) * LANES] = v[:, hh * DA_V_DIM:(hh + 1) * DA_V_DIM].astype(BF16)
            va_ref[:, (2 * h + 1) * LANES:(2 * h + 2) * LANES] = ones_col

    def z_part(c):
        gz_ref[:, c * half:(c + 1) * half] = proj(c * half, half, wz_ref)

    def gate_part(_):
        gab = proj(0, LANES, wab_ref)
        sp_in = gab + dtb_ref[...]
        softplus = jnp.maximum(sp_in, 0.0) + jnp.log(1.0 + jnp.exp(-jnp.abs(sp_in)))
        gate_ref[...] = jnp.where(lane < GDN_HEADS, -jnp.exp(alog_ref[...]) * softplus,
                                  1.0 / (1.0 + jnp.exp(-gab)))

    def conv_group(c):
        cols = slice(c * GDN_HEAD_DIM, (c + 1) * GDN_HEAD_DIM)
        cw = cw_ref[:, cols]
        xc = cw[GDN_CONV - 1:GDN_CONV] * cbuf_ref[halo:halo + tm, cols]
        for d in range(1, GDN_CONV):
            xc = xc + cw[GDN_CONV - 1 - d:GDN_CONV - d] * cbuf_ref[halo - d * tile:halo - d * tile + tm, cols]
        xc = _silu(xc)
        if c < GDN_HEADS:
            xc = xc * (lax.rsqrt(jnp.sum(xc * xc, axis=-1, keepdims=True) + EPS) * GDN_HEAD_DIM ** -0.5)
        elif c < 2 * GDN_HEADS:
            xc = xc * lax.rsqrt(jnp.sum(xc * xc, axis=-1, keepdims=True) + EPS)
        _unpermute_rows(perm_ref, xc, gqkv_ref, col0=c * GDN_HEAD_DIM, slab0=c % (D_MODEL // LANES))

    matmul_tasks = ([(gdn_part, c) for c in range(6)] + [(q_part, 0), (q_part, 1), (k_part, 0), (k_part, 1),
                    (v_part, 0), (v_part, 1), (z_part, 0), (z_part, 1), (gate_part, 0)])
    ngroups = 3 * GDN_HEADS
    issued = 0
    for c in range(ngroups):
        want = min(len(matmul_tasks), 4 + 2 * (c // 2))
        while issued < want:
            fn, arg = matmul_tasks[issued]
            fn(arg)
            issued += 1
        conv_group(c)
    for fn, arg in matmul_tasks[issued:]:
        fn(arg)


def _in_proj(x2, g, w_main, w_z, w_ab, conv_w, alog_row, dtb_row, tm, tiles_per_seq):
    m = x2.shape[0]
    row = lambda i: (i, 0)
    fixed = lambda i: (0, 0)
    return pl.pallas_call(
        functools.partial(_in_proj_kernel, tm=tm, tiles_per_seq=tiles_per_seq),
        grid=(m // tm,),
        in_specs=[pl.BlockSpec((tm, D_MODEL), row),
                  pl.BlockSpec((1, D_MODEL), fixed),
                  pl.BlockSpec((D_MODEL, W_MAIN_COLS), fixed),
                  pl.BlockSpec((D_MODEL, GDN_WIDTH), fixed),
                  pl.BlockSpec((D_MODEL, LANES), fixed),
                  pl.BlockSpec((GDN_CONV, 3 * GDN_WIDTH), fixed),
                  pl.BlockSpec((1, LANES), fixed),
                  pl.BlockSpec((1, LANES), fixed)],
        out_specs=[pl.BlockSpec((tm, 2 * DA_WIDTH), row),
                   pl.BlockSpec((tm, 2 * DA_WIDTH), row),
                   pl.BlockSpec((tm, 3 * GDN_WIDTH), row),
                   pl.BlockSpec((tm, GDN_WIDTH), row),
                   pl.BlockSpec((tm, LANES), row)],
        out_shape=[jax.ShapeDtypeStruct((m, 2 * DA_WIDTH), BF16),
                   jax.ShapeDtypeStruct((m, 2 * DA_WIDTH), BF16),
                   jax.ShapeDtypeStruct((m, 3 * GDN_WIDTH), F32),
                   jax.ShapeDtypeStruct((m, GDN_WIDTH), F32),
                   jax.ShapeDtypeStruct((m, LANES), F32)],
        scratch_shapes=[pltpu.VMEM((tm + (GDN_CONV - 1) * SUBLANES, 3 * GDN_WIDTH), F32),
                        pltpu.VMEM(((GDN_CONV - 1) * SUBLANES, 3 * GDN_WIDTH), F32),
                        pltpu.VMEM((D_MODEL // LANES, SUBLANES * PERM_PITCH, LANES), F32)],
        compiler_params=pltpu.CompilerParams(dimension_semantics=("arbitrary",),
                                             vmem_limit_bytes=VMEM_LIMIT),
        name="in_proj",
    )(x2, g, w_main, w_z, w_ab, conv_w, alog_row, dtb_row)


ATTN_HEADS_PER_STEP = 4


def _attn_kernel(lam_ref, sg_ref, q_ref, k_ref, v_ref, o_ref, acc_ref, m_ref, *, t, hp):
    hg = pl.program_id(1)
    qi = pl.program_id(2)

    lane = lax.broadcasted_iota(jnp.int32, (t, LANES), 1)
    col_f = lax.broadcasted_iota(jnp.int32, (1, t), 1).astype(F32)
    row_i = lax.broadcasted_iota(jnp.int32, (t, t), 0)
    col_i = lax.broadcasted_iota(jnp.int32, (t, t), 1)

    qm, slope_row = [], []
    for j in range(hp):
        q = q_ref[0, :, j * LANES:(j + 1) * LANES]
        zero = jnp.zeros_like(q)
        qm += [jnp.where(lane < DA_HEAD_DIM, q, zero), jnp.where(lane >= DA_HEAD_DIM, q, zero)]
        hf = (hg * hp + j + 1).astype(F32)
        slope_row.append(jnp.exp2(jnp.full((1, t), -8.0 / DA_HEADS, F32) * hf) * LOG2E)

    acc_ref[...] = jnp.zeros_like(acc_ref)

    nchain = 2 * hp

    m_ref[...] = jnp.full(m_ref.shape, NEG_BIG, F32)
    lane_tiles = lambda x, n: jnp.concatenate([x] * n, axis=1)

    def step(blocks):
        ms = [m_ref[idx] for idx in range(nchain)]
        work = [(ki, masked, idx) for ki, masked in blocks for idx in range(nchain)]

        def logits(ki, masked, idx):
            j = idx // 2
            k = k_ref[0, pl.ds(pl.multiple_of(ki * t, t), t), j * LANES:(j + 1) * LANES]
            pos = ((ki - qi) * t).astype(F32) + col_f
            s = (lax.dot_general(qm[idx], k, (((1,), (1,)), ((), ())), preferred_element_type=F32)
                 + pos * slope_row[j])
            return jnp.where(col_i <= row_i, s, NEG_BIG) if masked else s

        def probs(idx, s):
            m_new = jnp.maximum(ms[idx], jnp.max(s, axis=-1, keepdims=True))
            return m_new, jnp.exp2(s - lane_tiles(m_new, t // LANES)).astype(BF16)

        def accumulate(ki, idx, m_new, p):
            j = idx // 2
            v = v_ref[0, pl.ds(pl.multiple_of(ki * t, t), t), 2 * j * LANES:2 * (j + 1) * LANES]
            acc_ref[idx] = (lane_tiles(jnp.exp2(ms[idx] - m_new), 2) * acc_ref[idx]
                            + jnp.dot(p, v, preferred_element_type=F32))
            ms[idx] = m_new

        s_prev, mp_prev = None, None
        for slot in range(len(work) + 2):
            s_cur = logits(*work[slot]) if slot < len(work) else None
            mp_cur = probs(work[slot - 1][2], s_prev) if 1 <= slot <= len(work) else None
            if slot >= 2:
                accumulate(work[slot - 2][0], work[slot - 2][2], *mp_prev)
            s_prev, mp_prev = s_cur, mp_cur
        for idx in range(nchain):
            m_ref[idx] = ms[idx]

    npair = qi // 2
    odd = qi - 2 * npair

    def run(trips, blocks_of):
        def body(j, carry):
            step(blocks_of(j))
            return carry
        lax.fori_loop(0, trips, body, 0)

    run(npair, lambda j: [(2 * j, False), (2 * j + 1, False)])
    run(odd, lambda j: [(qi - 1, False), (qi, True)])
    run(1 - odd, lambda j: [(qi, True)])

    lam_p = lam_ref[...]
    lam = (jnp.exp(jnp.sum(lam_p[0:1] * lam_p[1:2], axis=-1, keepdims=True))
           - jnp.exp(jnp.sum(lam_p[2:3] * lam_p[3:4], axis=-1, keepdims=True)) + LAM_INIT)
    ones_sq = jnp.ones((DA_V_DIM, DA_V_DIM), BF16)
    for j in range(hp):
        a1 = acc_ref[2 * j]
        a2 = acc_ref[2 * j + 1]
        o = a1[:, :DA_V_DIM] / a1[:, DA_V_DIM:] - lam * (a2[:, :DA_V_DIM] / a2[:, DA_V_DIM:])
        o_ref[0, :, j * DA_V_DIM:(j + 1) * DA_V_DIM] = (
            o * lax.rsqrt(_sum3(ones_sq, o * o, left=False) * (1.0 / DA_V_DIM) + EPS)
            * sg_ref[...] * (1.0 - LAM_INIT)).astype(BF16)


def _diff_attn(lam_p, subln_g, qk3, va3, t):
    b, s, _ = qk3.shape
    hp = ATTN_HEADS_PER_STEP
    ngroups = DA_HEADS // hp
    return pl.pallas_call(
        functools.partial(_attn_kernel, t=t, hp=hp),
        grid=(b, ngroups, s // t),
        in_specs=[pl.BlockSpec((4, DA_HEAD_DIM), lambda bi, hg, qi: (0, 0)),
                  pl.BlockSpec((1, DA_V_DIM), lambda bi, hg, qi: (0, 0)),
                  pl.BlockSpec((1, t, hp * LANES), lambda bi, hg, qi: (bi, qi, hg)),
                  pl.BlockSpec((1, s, hp * LANES), lambda bi, hg, qi: (bi, 0, ngroups + hg),
                               pipeline_mode=pl.Buffered(1)),
                  pl.BlockSpec((1, s, 2 * hp * LANES), lambda bi, hg, qi: (bi, 0, hg),
                               pipeline_mode=pl.Buffered(1))],
        out_specs=pl.BlockSpec((1, t, hp * DA_V_DIM), lambda bi, hg, qi: (bi, qi, hg)),
        out_shape=jax.ShapeDtypeStruct((b, s, DA_WIDTH), BF16),
        scratch_shapes=[pltpu.VMEM((2 * hp, t, 2 * LANES), F32),
                        pltpu.VMEM((2 * hp, t, LANES), F32)],
        compiler_params=pltpu.CompilerParams(dimension_semantics=("arbitrary", "arbitrary", "arbitrary"),
                                             vmem_limit_bytes=VMEM_LIMIT),
        name="diff_attn",
    )(lam_p, subln_g, qk3, qk3, va3)


GDN_GROUP = 2 * GDN_CHUNK
GDN_WQ_ROWS = 2 * GDN_CHUNK
GDN_KQ_ROWS = GDN_HEAD_DIM + GDN_CHUNK


def _gdn_prep_kernel(gqkv_ref, gate_ref, u_ref, wq_ref, kq_ref, egl_ref, *, rb):
    c = GDN_CHUNK
    r = GDN_GROUP
    ri = lax.broadcasted_iota(jnp.int32, (r, r), 0)
    ci = lax.broadcasted_iota(jnp.int32, (r, r), 1)
    rx = ri ^ ci
    same_chunk = rx < c
    lower = ci <= ri
    strict = ci < ri
    eye = jnp.where(ci == ri, 1.0, 0.0).astype(F32)
    tri_ones = jnp.where(same_chunk, jnp.where(lower, 1.0, 0.0), 0.0).astype(BF16)
    blk_ones = jnp.where(same_chunk, 1.0, 0.0).astype(BF16)

    ngroups = rb // r
    chains = [(gidx, h) for gidx in range(ngroups) for h in range(GDN_HEADS)]
    each = lambda f: [f(j) for j in range(len(chains))]
    rows_of = lambda j: slice(chains[j][0] * r, (chains[j][0] + 1) * r)
    head_of = lambda j: chains[j][1]

    gate = [gate_ref[gidx * r:(gidx + 1) * r, :] for gidx in range(ngroups)]
    gc_all = [_sum3(tri_ones, gt) for gt in gate]
    gl_all = [_sum3(blk_ones, gt) for gt in gate]
    gc_t = [jnp.transpose(x) for x in gc_all]
    for gidx in range(ngroups):
        egl_ref[gidx * r:(gidx + 1) * r, :] = jnp.exp(gl_all[gidx])

    def hslice(j, part):
        lo = part * GDN_WIDTH + head_of(j) * GDN_HEAD_DIM
        return gqkv_ref[rows_of(j), lo:lo + GDN_HEAD_DIM]

    qh = each(lambda j: hslice(j, 0))
    kh = each(lambda j: hslice(j, 1))
    vh = each(lambda j: hslice(j, 2))
    beta = each(lambda j: gate[chains[j][0]][:, GDN_HEADS + head_of(j):GDN_HEADS + head_of(j) + 1])
    gc = each(lambda j: gc_all[chains[j][0]][:, head_of(j):head_of(j) + 1])
    gl = each(lambda j: gl_all[chains[j][0]][:, head_of(j):head_of(j) + 1])
    gc_row = each(lambda j: gc_t[chains[j][0]][head_of(j):head_of(j) + 1, :])

    decay = each(lambda j: jnp.exp(jnp.where(same_chunk, jnp.where(lower, gc[j] - gc_row[j], -jnp.inf), -jnp.inf)))
    kb = each(lambda j: kh[j] * beta[j])
    vb = each(lambda j: vh[j] * beta[j])
    kbf = each(lambda j: kh[j].astype(BF16))
    kk = each(lambda j: _mm_nt(kb[j], kbf[j]))
    qk = each(lambda j: _mm_nt(qh[j], kbf[j]))
    lmat = each(lambda j: jnp.where(strict, kk[j] * decay[j], 0.0))
    qk = each(lambda j: (qk[j] * decay[j]).astype(BF16))

    x1 = each(lambda j: jnp.where(rx < 16, -lmat[j], 0.0))
    p = each(lambda j: eye + x1[j])
    x2 = each(lambda j: _mm(x1[j], x1[j]))
    x2p = each(lambda j: _mm(x2[j], p[j]))
    x4 = each(lambda j: _mm(x2[j], x2[j]))
    p = each(lambda j: p[j] + x2p[j])
    x4p = each(lambda j: _mm(x4[j], p[j]))
    x8 = each(lambda j: _mm(x4[j], x4[j]))
    p = each(lambda j: p[j] + x4p[j])
    x8p = each(lambda j: _mm(x8[j], p[j]))
    p = each(lambda j: p[j] + x8p[j])
    b32 = each(lambda j: jnp.where(rx < 32, jnp.where(rx >= 16, lmat[j], 0.0), 0.0))
    t1 = each(lambda j: _mm(b32[j], p[j]))
    t2 = each(lambda j: _mm(p[j], t1[j]))
    p = each(lambda j: p[j] - t2[j])
    b64 = each(lambda j: jnp.where(rx >= 32, lmat[j], 0.0))
    t1 = each(lambda j: _mm(b64[j], p[j]))
    t2 = each(lambda j: _mm(p[j], t1[j]))
    tmat = each(lambda j: p[j] - t2[j])

    eg = each(lambda j: jnp.exp(gc[j]))
    uw = each(lambda j: _mm(tmat[j], jnp.concatenate([vb[j], kb[j] * eg[j]], axis=1)))
    qdec = each(lambda j: (qh[j] * eg[j]).astype(BF16))
    kdec_t = each(lambda j: jnp.transpose(kh[j] * jnp.exp(gl[j] - gc[j])).astype(BF16))
    for j, (gidx, h) in enumerate(chains):
        u_ref[rows_of(j), h * GDN_HEAD_DIM:(h + 1) * GDN_HEAD_DIM] = uw[j][:, :GDN_HEAD_DIM]
        w = uw[j][:, GDN_HEAD_DIM:].astype(BF16)
        for n in range(r // c):
            ch = gidx * (r // c) + n
            cr = slice(n * c, (n + 1) * c)
            wq_ref[h, ch * GDN_WQ_ROWS:ch * GDN_WQ_ROWS + c, :] = w[cr]
            wq_ref[h, ch * GDN_WQ_ROWS + c:(ch + 1) * GDN_WQ_ROWS, :] = qdec[j][cr]
            kq_ref[h, ch * GDN_KQ_ROWS:ch * GDN_KQ_ROWS + GDN_HEAD_DIM, :] = kdec_t[j][:, cr]
            kq_ref[h, ch * GDN_KQ_ROWS + GDN_HEAD_DIM:(ch + 1) * GDN_KQ_ROWS, :] = qk[j][cr, cr]


def _gdn_prep(gqkv, gate, rb):
    m = gqkv.shape[0]
    nch = rb // GDN_CHUNK
    row = lambda i: (i, 0)
    hrow = lambda i: (0, i, 0)
    return pl.pallas_call(
        functools.partial(_gdn_prep_kernel, rb=rb),
        grid=(m // rb,),
        in_specs=[pl.BlockSpec((rb, 3 * GDN_WIDTH), row),
                  pl.BlockSpec((rb, LANES), row)],
        out_specs=[pl.BlockSpec((rb, GDN_WIDTH), row),
                   pl.BlockSpec((GDN_HEADS, nch * GDN_WQ_ROWS, GDN_HEAD_DIM), hrow),
                   pl.BlockSpec((GDN_HEADS, nch * GDN_KQ_ROWS, GDN_CHUNK), hrow),
                   pl.BlockSpec((rb, LANES), row)],
        out_shape=[jax.ShapeDtypeStruct((m, GDN_WIDTH), F32),
                   jax.ShapeDtypeStruct((GDN_HEADS, m // GDN_CHUNK * GDN_WQ_ROWS, GDN_HEAD_DIM), BF16),
                   jax.ShapeDtypeStruct((GDN_HEADS, m // GDN_CHUNK * GDN_KQ_ROWS, GDN_CHUNK), BF16),
                   jax.ShapeDtypeStruct((m, LANES), F32)],
        compiler_params=pltpu.CompilerParams(dimension_semantics=("arbitrary",),
                                             vmem_limit_bytes=VMEM_LIMIT),
        name="gdn_prep",
    )(gqkv, gate)


def _gdn_scan_kernel(egl_ref, u_ref, wq_ref, kq_ref, gz_ref, ng_ref, og_ref, state_ref, *, nb, g):
    i = pl.program_id(0)
    c = GDN_CHUNK

    @pl.when(i == 0)
    def _():
        state_ref[...] = jnp.zeros_like(state_ref)

    chains = [(b, h) for b in range(nb) for h in range(GDN_HEADS)]
    each = lambda f: [f(j, *chains[j]) for j in range(len(chains))]
    hcols = lambda h: slice(h * GDN_HEAD_DIM, (h + 1) * GDN_HEAD_DIM)
    states = each(lambda j, b, h: state_ref[j])
    for n in range(g):
        rows = slice(n * c, (n + 1) * c)
        ws_qs = each(lambda j, b, h: jnp.dot(wq_ref[h, b, n * GDN_WQ_ROWS:(n + 1) * GDN_WQ_ROWS, :],
                                             states[j].astype(BF16),
                                             preferred_element_type=F32))
        v_new = each(lambda j, b, h: (u_ref[b, rows, hcols(h)] - ws_qs[j][:c]).astype(BF16))
        kv_qv = each(lambda j, b, h: jnp.dot(kq_ref[h, b, n * GDN_KQ_ROWS:(n + 1) * GDN_KQ_ROWS, :], v_new[j],
                                             preferred_element_type=F32))
        states = each(lambda j, b, h: states[j] * egl_ref[b, n * c:n * c + 1, h:h + 1] + kv_qv[j][:GDN_HEAD_DIM])
        for j, (b, h) in enumerate(chains):
            o = _rms(ws_qs[j][c:] + kv_qv[j][GDN_HEAD_DIM:], ng_ref[...])
            og_ref[b, rows, hcols(h)] = (o * _silu(gz_ref[b, rows, hcols(h)])).astype(BF16)
    for j in range(len(chains)):
        state_ref[j] = states[j]


def _gdn_scan(egl3, u3, wq4, kq4, gz3, norm_g, g):
    nb, s, _ = u3.shape
    nchunks = s // GDN_CHUNK
    blk = lambda i: (0, i, 0)
    hblk = lambda i: (0, 0, i, 0)
    return pl.pallas_call(
        functools.partial(_gdn_scan_kernel, nb=nb, g=g),
        grid=(nchunks // g,),
        in_specs=[pl.BlockSpec((nb, g * GDN_CHUNK, LANES), blk),
                  pl.BlockSpec((nb, g * GDN_CHUNK, GDN_WIDTH), blk),
                  pl.BlockSpec((GDN_HEADS, nb, g * GDN_WQ_ROWS, GDN_HEAD_DIM), hblk),
                  pl.BlockSpec((GDN_HEADS, nb, g * GDN_KQ_ROWS, GDN_CHUNK), hblk),
                  pl.BlockSpec((nb, g * GDN_CHUNK, GDN_WIDTH), blk),
                  pl.BlockSpec((1, GDN_HEAD_DIM), lambda i: (0, 0))],
        out_specs=pl.BlockSpec((nb, g * GDN_CHUNK, GDN_WIDTH), blk),
        out_shape=jax.ShapeDtypeStruct((nb, s, GDN_WIDTH), BF16),
        scratch_shapes=[pltpu.VMEM((nb * GDN_HEADS, GDN_HEAD_DIM, GDN_HEAD_DIM), F32)],
        compiler_params=pltpu.CompilerParams(dimension_semantics=("arbitrary",),
                                             vmem_limit_bytes=VMEM_LIMIT),
        name="gdn_scan",
    )(egl3, u3, wq4, kq4, gz3, norm_g)


FFN_FC = 256


def _ffn_kernel(x_ref, oa_ref, og_ref, wa_ref, wg_ref, gffn_ref, wup_ref, cw_ref, cb_ref, wd_ref, g_ref, out_ref,
                ubuf_ref, carry_ref, acc_ref, perm_ref, *, tm, tiles_per_seq):
    i = pl.program_id(0)
    tile = SUBLANES
    halo = (FFN_CONV - 1) * tile
    first = (i % tiles_per_seq) == 0
    x1 = _permute_rows(perm_ref, x_ref[...] + jnp.dot(oa_ref[...], wa_ref[...], preferred_element_type=F32)
                       + jnp.dot(og_ref[...], wg_ref[...], preferred_element_type=F32))
    acc_ref[...] = x1
    h2 = _rms(x1, gffn_ref[...]).astype(BF16)

    @pl.when(i == 0)
    def _():
        carry_ref[...] = jnp.zeros_like(carry_ref)

    nslab = D_FF // FFN_FC
    col_of = lambda cidx, part: part * D_FF + cidx * FFN_FC
    slot_of = lambda cidx, part: 2 * (cidx % 2) + part
    row8 = lax.broadcasted_iota(jnp.int32, (tile, FFN_FC), 0)

    def up(cidx):
        for part in range(2):
            lo, slot = col_of(cidx, part), slot_of(cidx, part)
            u = jnp.dot(h2, wup_ref[:, lo:lo + FFN_FC], preferred_element_type=F32)
            prev = carry_ref[:, lo:lo + FFN_FC]
            prev = jnp.where(first, jnp.zeros_like(prev), prev)
            for d in range(1, FFN_CONV):
                cur_t = u[tm - d * tile:tm - (d - 1) * tile, :]
                prev_t = prev[halo - d * tile:halo - (d - 1) * tile, :]
                ubuf_ref[slot, halo - d * tile:halo - (d - 1) * tile, :] = jnp.where(
                    row8 == 0, pltpu.roll(prev_t, 1, 0), pltpu.roll(cur_t, 1, 0))
            ubuf_ref[slot, halo:halo + tm, :] = u
            carry_ref[:, lo:lo + FFN_FC] = u[tm - halo:tm, :]

    def conv(cidx, part):
        lo, slot = col_of(cidx, part), slot_of(cidx, part)
        cw = cw_ref[:, lo:lo + FFN_FC]
        y = cb_ref[:, lo:lo + FFN_FC] + cw[FFN_CONV - 1:FFN_CONV] * ubuf_ref[slot, halo:halo + tm, :]
        for d in range(1, FFN_CONV):
            y = y + cw[FFN_CONV - 1 - d:FFN_CONV - d] * ubuf_ref[slot, halo - d * tile:halo - d * tile + tm, :]
        return y

    def down(cidx, act):
        acc_ref[...] += jnp.dot(act, wd_ref[cidx * FFN_FC:(cidx + 1) * FFN_FC, :], preferred_element_type=F32)

    act_prev = None
    for slot in range(nslab + 2):
        if slot < nslab:
            up(slot)
        act_cur = (_silu(conv(slot - 1, 0)) * conv(slot - 1, 1)).astype(BF16) if 1 <= slot <= nslab else None
        if slot >= 2:
            down(slot - 2, act_prev)
        act_prev = act_cur

    _unpermute_rows(perm_ref, _rms(acc_ref[...], g_ref[...]), out_ref)


def _ffn(x2, oa2, og2, wo_a, wo_g, g_ffn, w_up, conv_w, conv_b, w_down, g, tm, tiles_per_seq):
    m = x2.shape[0]
    row = lambda i: (i, 0)
    fixed = lambda i: (0, 0)
    once = pl.Buffered(1)
    return pl.pallas_call(
        functools.partial(_ffn_kernel, tm=tm, tiles_per_seq=tiles_per_seq),
        grid=(m // tm,),
        in_specs=[pl.BlockSpec((tm, D_MODEL), row),
                  pl.BlockSpec((tm, DA_WIDTH), row),
                  pl.BlockSpec((tm, GDN_WIDTH), row),
                  pl.BlockSpec((DA_WIDTH, D_MODEL), fixed, pipeline_mode=once),
                  pl.BlockSpec((GDN_WIDTH, D_MODEL), fixed, pipeline_mode=once),
                  pl.BlockSpec((1, D_MODEL), fixed),
                  pl.BlockSpec((D_MODEL, 2 * D_FF), fixed, pipeline_mode=once),
                  pl.BlockSpec((FFN_CONV, 2 * D_FF), fixed),
                  pl.BlockSpec((1, 2 * D_FF), fixed),
                  pl.BlockSpec((D_FF, D_MODEL), fixed, pipeline_mode=once),
                  pl.BlockSpec((1, D_MODEL), fixed)],
        out_specs=pl.BlockSpec((tm, D_MODEL), row),
        out_shape=jax.ShapeDtypeStruct((m, D_MODEL), F32),
        scratch_shapes=[pltpu.VMEM((4, tm + (FFN_CONV - 1) * SUBLANES, FFN_FC), F32),
                        pltpu.VMEM(((FFN_CONV - 1) * SUBLANES, 2 * D_FF), F32),
                        pltpu.VMEM((tm, D_MODEL), F32),
                        pltpu.VMEM((D_MODEL // LANES, SUBLANES * PERM_PITCH, LANES), F32)],
        compiler_params=pltpu.CompilerParams(dimension_semantics=("arbitrary",),
                                             vmem_limit_bytes=VMEM_LIMIT),
        name="ffn",
    )(x2, oa2, og2, wo_a, wo_g, g_ffn, w_up, conv_w, conv_b, w_down, g)


def _pad_lanes(v):
    return jnp.zeros((1, LANES), F32).at[0, :v.shape[0]].set(v.astype(F32))


def kernel(x, attn_norm_g, w_in, da_lambda_q1, da_lambda_k1, da_lambda_q2, da_lambda_k2, da_subln_g,
           gdn_conv_w, gdn_a_log, gdn_dt_bias, gdn_norm_g, w_out, ffn_norm_g, w_up, ffn_conv_w,
           ffn_conv_b, w_down, final_norm_g):
    b, s, d = x.shape
    m = b * s
    tm = 512
    assert d == D_MODEL and s % tm == 0 and w_in.shape[0] == 1, (x.shape, w_in.shape)
    l = 0
    x2 = x.reshape(m, d)

    wi = w_in[l]
    w_main = wi[:, :W_MAIN_COLS].astype(BF16)
    w_z = wi[:, W_MAIN_COLS + 2 * GDN_HEADS:].astype(BF16)
    w_ab = jnp.pad(wi[:, W_MAIN_COLS:W_MAIN_COLS + 2 * GDN_HEADS], ((0, 0), (0, LANES - 2 * GDN_HEADS))).astype(BF16)
    lam_p = jnp.stack([da_lambda_q1[l], da_lambda_k1[l], da_lambda_q2[l], da_lambda_k2[l]]).astype(F32)

    qk, va, gqkv, gz, gate = _in_proj(x2, attn_norm_g[l].reshape(1, d).astype(F32), w_main, w_z, w_ab,
                                      gdn_conv_w[l].astype(F32), _pad_lanes(gdn_a_log[l]),
                                      _pad_lanes(gdn_dt_bias[l]), tm, s // tm)

    oa = _diff_attn(lam_p, da_subln_g[l].reshape(1, DA_V_DIM).astype(F32),
                    qk.reshape(b, s, 2 * DA_WIDTH), va.reshape(b, s, 2 * DA_WIDTH), 512)

    u, wq, kq, egl = _gdn_prep(gqkv, gate, 512)
    nchunks = s // GDN_CHUNK
    og = _gdn_scan(egl.reshape(b, s, LANES), u.reshape(b, s, GDN_WIDTH),
                   wq.reshape(GDN_HEADS, b, nchunks * GDN_WQ_ROWS, GDN_HEAD_DIM),
                   kq.reshape(GDN_HEADS, b, nchunks * GDN_KQ_ROWS, GDN_CHUNK),
                   gz.reshape(b, s, GDN_WIDTH), gdn_norm_g[l].reshape(1, GDN_HEAD_DIM).astype(F32), 8)

    wo = w_out[l].astype(BF16)
    out = _ffn(x2, oa.reshape(m, DA_WIDTH), og.reshape(m, GDN_WIDTH), wo[:DA_WIDTH], wo[DA_WIDTH:],
               ffn_norm_g[l].reshape(1, d).astype(F32), w_up[l].astype(BF16), ffn_conv_w[l].astype(F32),
               ffn_conv_b[l].reshape(1, 2 * D_FF).astype(F32), w_down[l].astype(BF16),
               final_norm_g.reshape(1, d).astype(F32), tm, s // tm)
    return out.reshape(b, s, d)
```

```python
import functools
import math

import jax
import jax.numpy as jnp
from jax import lax
from jax.experimental import pallas as pl
from jax.experimental.pallas import tpu as pltpu

F32 = jnp.float32
BF16 = jnp.bfloat16

EPS = 1e-6
LOG2E = 1.4426950408889634
NEG_BIG = -1e30

D_MODEL = 1024
DA_HEADS = 4
DA_HEAD_DIM = 64
DA_V_DIM = 128
DA_WIDTH = DA_HEADS * DA_V_DIM
GDN_HEADS = 4
GDN_HEAD_DIM = 128
GDN_WIDTH = GDN_HEADS * GDN_HEAD_DIM
GDN_CONV = 4
GDN_CHUNK = 64
D_FF = 2816
FFN_CONV = 3
LAM_INIT = 0.8 - 0.6 * math.exp(-0.3 * 0)

LANES = 128
SUBLANES = 8
VMEM_LIMIT = 56 * 1024 * 1024

W_MAIN_COLS = 3 * DA_WIDTH + 3 * GDN_WIDTH


def _mm(a, b):
    return jnp.dot(a.astype(BF16), b.astype(BF16), preferred_element_type=F32)


def _mm_nt(a, b):
    return lax.dot_general(a.astype(BF16), b.astype(BF16), (((1,), (1,)), ((), ())),
                           preferred_element_type=F32)


def _sum3(ones_bf16, x, left=True):
    hi = x.astype(BF16)
    r1 = x - hi.astype(F32)
    mid = r1.astype(BF16)
    lo = (r1 - mid.astype(F32)).astype(BF16)
    if left:
        dot = lambda t: jnp.dot(ones_bf16, t, preferred_element_type=F32)
    else:
        dot = lambda t: jnp.dot(t, ones_bf16, preferred_element_type=F32)
    return dot(hi) + dot(mid) + dot(lo)


def _silu(x):
    hx = 0.5 * x
    return hx + hx * jnp.tanh(hx)


def _rms(x, g):
    return x * lax.rsqrt(jnp.mean(x * x, axis=-1, keepdims=True) + EPS) * g


PERM_PITCH = 72


def _permute_rows(scr_ref, a, nseg=SUBLANES):
    tm, ncols = a.shape
    seg = tm // nseg
    slabs = []
    for k in range(ncols // LANES):
        for s in range(nseg):
            scr_ref[k, s * PERM_PITCH:s * PERM_PITCH + seg, :] = a[s * seg:(s + 1) * seg, k * LANES:(k + 1) * LANES]
        slabs.append(jnp.concatenate(
            [scr_ref[k, pl.ds(j, nseg, stride=PERM_PITCH), :] for j in range(seg)], axis=0))
    return jnp.concatenate(slabs, axis=1)


def _unpermute_rows(scr_ref, ap, out_ref, col0=0, slab0=0, nseg=SUBLANES):
    tm, ncols = ap.shape
    seg = tm // nseg
    for k in range(ncols // LANES):
        for j in range(seg):
            scr_ref[slab0 + k, pl.ds(j, nseg, stride=PERM_PITCH), :] = (
                ap[j * nseg:(j + 1) * nseg, k * LANES:(k + 1) * LANES])
        for s in range(nseg):
            out_ref[s * seg:(s + 1) * seg, col0 + k * LANES:col0 + (k + 1) * LANES] = (
                scr_ref[slab0 + k, s * PERM_PITCH:s * PERM_PITCH + seg, :])


def _in_proj_kernel(x_ref, g_ref, w_ref, wz_ref, wab_ref, cw_ref, alog_ref, dtb_ref,
                    qk_ref, va_ref, gqkv_ref, gz_ref, gate_ref, cbuf_ref, carry_ref, perm_ref,
                    *, tm, tiles_per_seq):
    i = pl.program_id(0)
    tile = SUBLANES
    halo = (GDN_CONV - 1) * tile
    first = (i % tiles_per_seq) == 0
    h = _rms(x_ref[...], g_ref[...])
    hb = h.astype(BF16)
    hb_perm = _permute_rows(perm_ref, h).astype(BF16)

    def proj(lo, width, ref=w_ref, lhs=hb):
        return jnp.dot(lhs, ref[:, lo:lo + width], preferred_element_type=F32)

    lane = lax.broadcasted_iota(jnp.int32, (tm, LANES), 1)
    base = 3 * DA_WIDTH
    half = DA_WIDTH // 2
    row8 = lax.broadcasted_iota(jnp.int32, (tile, half), 0)

    @pl.when(i == 0)
    def _():
        carry_ref[...] = jnp.zeros_like(carry_ref)

    def gdn_part(c):
        cols = slice(c * half, (c + 1) * half)
        u = proj(base + c * half, half, lhs=hb_perm)
        prev = carry_ref[:, cols]
        prev = jnp.where(first, jnp.zeros_like(prev), prev)
        for d in range(1, GDN_CONV):
            cur_t = u[tm - d * tile:tm - (d - 1) * tile, :]
            prev_t = prev[halo - d * tile:halo - (d - 1) * tile, :]
            cbuf_ref[halo - d * tile:halo - (d - 1) * tile, cols] = jnp.where(
                row8 == 0, pltpu.roll(prev_t, 1, 0), pltpu.roll(cur_t, 1, 0))
        cbuf_ref[halo:halo + tm, cols] = u
        carry_ref[:, cols] = u[tm - halo:tm, :]

    def q_part(c):
        qk_ref[:, c * half:(c + 1) * half] = (proj(c * half, half) * (DA_HEAD_DIM ** -0.5 * LOG2E)).astype(BF16)

    def k_part(c):
        qk_ref[:, DA_WIDTH + c * half:DA_WIDTH + (c + 1) * half] = proj(DA_WIDTH + c * half, half).astype(BF16)

    def v_part(c):
        v = proj(2 * DA_WIDTH + c * half, half)
        ones_col = jnp.ones((tm, LANES), BF16)
        for hh in range(2):
            h = 2 * c + hh
            va_ref[:, 2 * h * LANES:(2 * h + 1) * LANES] = v[:, hh * DA_V_DIM:(hh + 1) * DA_V_DIM].astype(BF16)
            va_ref[:, (2 * h + 1) * LANES:(2 * h + 2) * LANES] = ones_col

    def z_part(c):
        gz_ref[:, c * half:(c + 1) * half] = proj(c * half, half, wz_ref)

    def gate_part(_):
        gab = proj(0, LANES, wab_ref)
        sp_in = gab + dtb_ref[...]
        softplus = jnp.maximum(sp_in, 0.0) + jnp.log(1.0 + jnp.exp(-jnp.abs(sp_in)))
        gate_ref[...] = jnp.where(lane < GDN_HEADS, -jnp.exp(alog_ref[...]) * softplus,
                                  1.0 / (1.0 + jnp.exp(-gab)))

    def conv_group(c):
        cols = slice(c * GDN_HEAD_DIM, (c + 1) * GDN_HEAD_DIM)
        cw = cw_ref[:, cols]
        xc = cw[GDN_CONV - 1:GDN_CONV] * cbuf_ref[halo:halo + tm, cols]
        for d in range(1, GDN_CONV):
            xc = xc + cw[GDN_CONV - 1 - d:GDN_CONV - d] * cbuf_ref[halo - d * tile:halo - d * tile + tm, cols]
        xc = _silu(xc)
        if c < GDN_HEADS:
            xc = xc * (lax.rsqrt(jnp.sum(xc * xc, axis=-1, keepdims=True) + EPS) * GDN_HEAD_DIM ** -0.5)
        elif c < 2 * GDN_HEADS:
            xc = xc * lax.rsqrt(jnp.sum(xc * xc, axis=-1, keepdims=True) + EPS)
        _unpermute_rows(perm_ref, xc, gqkv_ref, col0=c * GDN_HEAD_DIM, slab0=c % (D_MODEL // LANES))

    matmul_tasks = ([(gdn_part, c) for c in range(6)] + [(q_part, 0), (q_part, 1), (k_part, 0), (k_part, 1),
                    (v_part, 0), (v_part, 1), (z_part, 0), (z_part, 1), (gate_part, 0)])
    ngroups = 3 * GDN_HEADS
    issued = 0
    for c in range(ngroups):
        want = min(len(matmul_tasks), 4 + 2 * (c // 2))
        while issued < want:
            fn, arg = matmul_tasks[issued]
            fn(arg)
            issued += 1
        conv_group(c)
    for fn, arg in matmul_tasks[issued:]:
        fn(arg)


def _in_proj(x2, g, w_main, w_z, w_ab, conv_w, alog_row, dtb_row, tm, tiles_per_seq):
    m = x2.shape[0]
    row = lambda i: (i, 0)
    fixed = lambda i: (0, 0)
    return pl.pallas_call(
        functools.partial(_in_proj_kernel, tm=tm, tiles_per_seq=tiles_per_seq),
        grid=(m // tm,),
        in_specs=[pl.BlockSpec((tm, D_MODEL), row),
                  pl.BlockSpec((1, D_MODEL), fixed),
                  pl.BlockSpec((D_MODEL, W_MAIN_COLS), fixed),
                  pl.BlockSpec((D_MODEL, GDN_WIDTH), fixed),
                  pl.BlockSpec((D_MODEL, LANES), fixed),
                  pl.BlockSpec((GDN_CONV, 3 * GDN_WIDTH), fixed),
                  pl.BlockSpec((1, LANES), fixed),
                  pl.BlockSpec((1, LANES), fixed)],
        out_specs=[pl.BlockSpec((tm, 2 * DA_WIDTH), row),
                   pl.BlockSpec((tm, 2 * DA_WIDTH), row),
                   pl.BlockSpec((tm, 3 * GDN_WIDTH), row),
                   pl.BlockSpec((tm, GDN_WIDTH), row),
                   pl.BlockSpec((tm, LANES), row)],
        out_shape=[jax.ShapeDtypeStruct((m, 2 * DA_WIDTH), BF16),
                   jax.ShapeDtypeStruct((m, 2 * DA_WIDTH), BF16),
                   jax.ShapeDtypeStruct((m, 3 * GDN_WIDTH), F32),
                   jax.ShapeDtypeStruct((m, GDN_WIDTH), F32),
                   jax.ShapeDtypeStruct((m, LANES), F32)],
        scratch_shapes=[pltpu.VMEM((tm + (GDN_CONV - 1) * SUBLANES, 3 * GDN_WIDTH), F32),
                        pltpu.VMEM(((GDN_CONV - 1) * SUBLANES, 3 * GDN_WIDTH), F32),
                        pltpu.VMEM((D_MODEL // LANES, SUBLANES * PERM_PITCH, LANES), F32)],
        compiler_params=pltpu.CompilerParams(dimension_semantics=("arbitrary",),
                                             vmem_limit_bytes=VMEM_LIMIT),
        name="in_proj",
    )(x2, g, w_main, w_z, w_ab, conv_w, alog_row, dtb_row)


ATTN_HEADS_PER_STEP = 4


def _attn_kernel(lam_ref, sg_ref, q_ref, k_ref, v_ref, o_ref, acc_ref, m_ref, *, t, hp):
    hg = pl.program_id(1)
    qi = pl.program_id(2)

    lane = lax.broadcasted_iota(jnp.int32, (t, LANES), 1)
    col_f = lax.broadcasted_iota(jnp.int32, (1, t), 1).astype(F32)
    row_i = lax.broadcasted_iota(jnp.int32, (t, t), 0)
    col_i = lax.broadcasted_iota(jnp.int32, (t, t), 1)

    qm, slope_row = [], []
    for j in range(hp):
        q = q_ref[0, :, j * LANES:(j + 1) * LANES]
        zero = jnp.zeros_like(q)
        qm += [jnp.where(lane < DA_HEAD_DIM, q, zero), jnp.where(lane >= DA_HEAD_DIM, q, zero)]
        hf = (hg * hp + j + 1).astype(F32)
        slope_row.append(jnp.exp2(jnp.full((1, t), -8.0 / DA_HEADS, F32) * hf) * LOG2E)

    acc_ref[...] = jnp.zeros_like(acc_ref)

    nchain = 2 * hp

    m_ref[...] = jnp.full(m_ref.shape, NEG_BIG, F32)
    lane_tiles = lambda x, n: jnp.concatenate([x] * n, axis=1)

    def step(blocks):
        ms = [m_ref[idx] for idx in range(nchain)]
        work = [(ki, masked, idx) for ki, masked in blocks for idx in range(nchain)]

        def logits(ki, masked, idx):
            j = idx // 2
            k = k_ref[0, pl.ds(pl.multiple_of(ki * t, t), t), j * LANES:(j + 1) * LANES]
            pos = ((ki - qi) * t).astype(F32) + col_f
            s = (lax.dot_general(qm[idx], k, (((1,), (1,)), ((), ())), preferred_element_type=F32)
                 + pos * slope_row[j])
            return jnp.where(col_i <= row_i, s, NEG_BIG) if masked else s

        def probs(idx, s):
            m_new = jnp.maximum(ms[idx], jnp.max(s, axis=-1, keepdims=True))
            return m_new, jnp.exp2(s - lane_tiles(m_new, t // LANES)).astype(BF16)

        def accumulate(ki, idx, m_new, p):
            j = idx // 2
            v = v_ref[0, pl.ds(pl.multiple_of(ki * t, t), t), 2 * j * LANES:2 * (j + 1) * LANES]
            acc_ref[idx] = (lane_tiles(jnp.exp2(ms[idx] - m_new), 2) * acc_ref[idx]
                            + jnp.dot(p, v, preferred_element_type=F32))
            ms[idx] = m_new

        s_prev, mp_prev = None, None
        for slot in range(len(work) + 2):
            s_cur = logits(*work[slot]) if slot < len(work) else None
            mp_cur = probs(work[slot - 1][2], s_prev) if 1 <= slot <= len(work) else None
            if slot >= 2:
                accumulate(work[slot - 2][0], work[slot - 2][2], *mp_prev)
            s_prev, mp_prev = s_cur, mp_cur
        for idx in range(nchain):
            m_ref[idx] = ms[idx]

    nquad = qi // 4
    rem = qi - 4 * nquad
    odd = rem % 2

    def run(trips, blocks_of):
        def body(j, carry):
            step(blocks_of(j))
            return carry
        lax.fori_loop(0, trips, body, 0)

    run(nquad, lambda j: [(4 * j + r, False) for r in range(4)])
    run(rem // 2, lambda j: [(4 * nquad, False), (4 * nquad + 1, False)])
    run(odd, lambda j: [(qi - 1, False), (qi, True)])
    run(1 - odd, lambda j: [(qi, True)])

    lam_p = lam_ref[...]
    lam = (jnp.exp(jnp.sum(lam_p[0:1] * lam_p[1:2], axis=-1, keepdims=True))
           - jnp.exp(jnp.sum(lam_p[2:3] * lam_p[3:4], axis=-1, keepdims=True)) + LAM_INIT)
    ones_sq = jnp.ones((DA_V_DIM, DA_V_DIM), BF16)
    for j in range(hp):
        a1 = acc_ref[2 * j]
        a2 = acc_ref[2 * j + 1]
        o = a1[:, :DA_V_DIM] / a1[:, DA_V_DIM:] - lam * (a2[:, :DA_V_DIM] / a2[:, DA_V_DIM:])
        o_ref[0, :, j * DA_V_DIM:(j + 1) * DA_V_DIM] = (
            o * lax.rsqrt(_sum3(ones_sq, o * o, left=False) * (1.0 / DA_V_DIM) + EPS)
            * sg_ref[...] * (1.0 - LAM_INIT)).astype(BF16)


def _diff_attn(lam_p, subln_g, qk3, va3, t):
    b, s, _ = qk3.shape
    hp = ATTN_HEADS_PER_STEP
    ngroups = DA_HEADS // hp
    return pl.pallas_call(
        functools.partial(_attn_kernel, t=t, hp=hp),
        grid=(b, ngroups, s // t),
        in_specs=[pl.BlockSpec((4, DA_HEAD_DIM), lambda bi, hg, qi: (0, 0)),
                  pl.BlockSpec((1, DA_V_DIM), lambda bi, hg, qi: (0, 0)),
                  pl.BlockSpec((1, t, hp * LANES), lambda bi, hg, qi: (bi, qi, hg)),
                  pl.BlockSpec((1, s, hp * LANES), lambda bi, hg, qi: (bi, 0, ngroups + hg),
                               pipeline_mode=pl.Buffered(1)),
                  pl.BlockSpec((1, s, 2 * hp * LANES), lambda bi, hg, qi: (bi, 0, hg),
                               pipeline_mode=pl.Buffered(1))],
        out_specs=pl.BlockSpec((1, t, hp * DA_V_DIM), lambda bi, hg, qi: (bi, qi, hg)),
        out_shape=jax.ShapeDtypeStruct((b, s, DA_WIDTH), BF16),
        scratch_shapes=[pltpu.VMEM((2 * hp, t, 2 * LANES), F32),
                        pltpu.VMEM((2 * hp, t, LANES), F32)],
        compiler_params=pltpu.CompilerParams(dimension_semantics=("arbitrary", "arbitrary", "arbitrary"),
                                             vmem_limit_bytes=VMEM_LIMIT),
        name="diff_attn",
    )(lam_p, subln_g, qk3, qk3, va3)


GDN_GROUP = 2 * GDN_CHUNK
GDN_WQ_ROWS = 2 * GDN_CHUNK
GDN_KQ_ROWS = GDN_HEAD_DIM + GDN_CHUNK


def _gdn_prep_kernel(gqkv_ref, gate_ref, u_ref, wq_ref, kq_ref, egl_ref, *, rb):
    c = GDN_CHUNK
    r = GDN_GROUP
    ri = lax.broadcasted_iota(jnp.int32, (r, r), 0)
    ci = lax.broadcasted_iota(jnp.int32, (r, r), 1)
    rx = ri ^ ci
    same_chunk = rx < c
    lower = ci <= ri
    strict = ci < ri
    eye = jnp.where(ci == ri, 1.0, 0.0).astype(F32)
    tri_ones = jnp.where(same_chunk, jnp.where(lower, 1.0, 0.0), 0.0).astype(BF16)
    blk_ones = jnp.where(same_chunk, 1.0, 0.0).astype(BF16)

    ngroups = rb // r
    chains = [(gidx, h) for gidx in range(ngroups) for h in range(GDN_HEADS)]
    each = lambda f: [f(j) for j in range(len(chains))]
    rows_of = lambda j: slice(chains[j][0] * r, (chains[j][0] + 1) * r)
    head_of = lambda j: chains[j][1]

    gate = [gate_ref[gidx * r:(gidx + 1) * r, :] for gidx in range(ngroups)]
    gc_all = [_sum3(tri_ones, gt) for gt in gate]
    gl_all = [_sum3(blk_ones, gt) for gt in gate]
    gc_t = [jnp.transpose(x) for x in gc_all]
    for gidx in range(ngroups):
        egl_ref[gidx * r:(gidx + 1) * r, :] = jnp.exp(gl_all[gidx])

    def hslice(j, part):
        lo = part * GDN_WIDTH + head_of(j) * GDN_HEAD_DIM
        return gqkv_ref[rows_of(j), lo:lo + GDN_HEAD_DIM]

    qh = each(lambda j: hslice(j, 0))
    kh = each(lambda j: hslice(j, 1))
    vh = each(lambda j: hslice(j, 2))
    beta = each(lambda j: gate[chains[j][0]][:, GDN_HEADS + head_of(j):GDN_HEADS + head_of(j) + 1])
    gc = each(lambda j: gc_all[chains[j][0]][:, head_of(j):head_of(j) + 1])
    gl = each(lambda j: gl_all[chains[j][0]][:, head_of(j):head_of(j) + 1])
    gc_row = each(lambda j: gc_t[chains[j][0]][head_of(j):head_of(j) + 1, :])

    decay = each(lambda j: jnp.exp(jnp.where(same_chunk, jnp.where(lower, gc[j] - gc_row[j], -jnp.inf), -jnp.inf)))
    kb = each(lambda j: kh[j] * beta[j])
    vb = each(lambda j: vh[j] * beta[j])
    kbf = each(lambda j: kh[j].astype(BF16))
    kk = each(lambda j: _mm_nt(kb[j], kbf[j]))
    qk = each(lambda j: _mm_nt(qh[j], kbf[j]))
    lmat = each(lambda j: jnp.where(strict, kk[j] * decay[j], 0.0))
    qk = each(lambda j: (qk[j] * decay[j]).astype(BF16))

    x1 = each(lambda j: jnp.where(rx < 16, -lmat[j], 0.0))
    p = each(lambda j: eye + x1[j])
    x2 = each(lambda j: _mm(x1[j], x1[j]))
    x2p = each(lambda j: _mm(x2[j], p[j]))
    x4 = each(lambda j: _mm(x2[j], x2[j]))
    p = each(lambda j: p[j] + x2p[j])
    x4p = each(lambda j: _mm(x4[j], p[j]))
    x8 = each(lambda j: _mm(x4[j], x4[j]))
    p = each(lambda j: p[j] + x4p[j])
    x8p = each(lambda j: _mm(x8[j], p[j]))
    p = each(lambda j: p[j] + x8p[j])
    b32 = each(lambda j: jnp.where(rx < 32, jnp.where(rx >= 16, lmat[j], 0.0), 0.0))
    t1 = each(lambda j: _mm(b32[j], p[j]))
    t2 = each(lambda j: _mm(p[j], t1[j]))
    p = each(lambda j: p[j] - t2[j])
    b64 = each(lambda j: jnp.where(rx >= 32, lmat[j], 0.0))
    t1 = each(lambda j: _mm(b64[j], p[j]))
    t2 = each(lambda j: _mm(p[j], t1[j]))
    tmat = each(lambda j: p[j] - t2[j])

    eg = each(lambda j: jnp.exp(gc[j]))
    uw = each(lambda j: _mm(tmat[j], jnp.concatenate([vb[j], kb[j] * eg[j]], axis=1)))
    qdec = each(lambda j: (qh[j] * eg[j]).astype(BF16))
    kdec_t = each(lambda j: jnp.transpose(kh[j] * jnp.exp(gl[j] - gc[j])).astype(BF16))
    for j, (gidx, h) in enumerate(chains):
        u_ref[rows_of(j), h * GDN_HEAD_DIM:(h + 1) * GDN_HEAD_DIM] = uw[j][:, :GDN_HEAD_DIM]
        w = uw[j][:, GDN_HEAD_DIM:].astype(BF16)
        for n in range(r // c):
            ch = gidx * (r // c) + n
            cr = slice(n * c, (n + 1) * c)
            wq_ref[h, ch * GDN_WQ_ROWS:ch * GDN_WQ_ROWS + c, :] = w[cr]
            wq_ref[h, ch * GDN_WQ_ROWS + c:(ch + 1) * GDN_WQ_ROWS, :] = qdec[j][cr]
            kq_ref[h, ch * GDN_KQ_ROWS:ch * GDN_KQ_ROWS + GDN_HEAD_DIM, :] = kdec_t[j][:, cr]
            kq_ref[h, ch * GDN_KQ_ROWS + GDN_HEAD_DIM:(ch + 1) * GDN_KQ_ROWS, :] = qk[j][cr, cr]


def _gdn_prep(gqkv, gate, rb):
    m = gqkv.shape[0]
    nch = rb // GDN_CHUNK
    row = lambda i: (i, 0)
    hrow = lambda i: (0, i, 0)
    return pl.pallas_call(
        functools.partial(_gdn_prep_kernel, rb=rb),
        grid=(m // rb,),
        in_specs=[pl.BlockSpec((rb, 3 * GDN_WIDTH), row),
                  pl.BlockSpec((rb, LANES), row)],
        out_specs=[pl.BlockSpec((rb, GDN_WIDTH), row),
                   pl.BlockSpec((GDN_HEADS, nch * GDN_WQ_ROWS, GDN_HEAD_DIM), hrow),
                   pl.BlockSpec((GDN_HEADS, nch * GDN_KQ_ROWS, GDN_CHUNK), hrow),
                   pl.BlockSpec((rb, LANES), row)],
        out_shape=[jax.ShapeDtypeStruct((m, GDN_WIDTH), F32),
                   jax.ShapeDtypeStruct((GDN_HEADS, m // GDN_CHUNK * GDN_WQ_ROWS, GDN_HEAD_DIM), BF16),
                   jax.ShapeDtypeStruct((GDN_HEADS, m // GDN_CHUNK * GDN_KQ_ROWS, GDN_CHUNK), BF16),
                   jax.ShapeDtypeStruct((m, LANES), F32)],
        compiler_params=pltpu.CompilerParams(dimension_semantics=("arbitrary",),
                                             vmem_limit_bytes=VMEM_LIMIT),
        name="gdn_prep",
    )(gqkv, gate)


def _gdn_scan_kernel(egl_ref, u_ref, wq_ref, kq_ref, gz_ref, ng_ref, og_ref, state_ref, *, nb, g):
    i = pl.program_id(0)
    c = GDN_CHUNK

    @pl.when(i == 0)
    def _():
        state_ref[...] = jnp.zeros_like(state_ref)

    chains = [(b, h) for b in range(nb) for h in range(GDN_HEADS)]
    each = lambda f: [f(j, *chains[j]) for j in range(len(chains))]
    hcols = lambda h: slice(h * GDN_HEAD_DIM, (h + 1) * GDN_HEAD_DIM)
    states = each(lambda j, b, h: state_ref[j])
    for n in range(g):
        rows = slice(n * c, (n + 1) * c)
        ws_qs = each(lambda j, b, h: jnp.dot(wq_ref[h, b, n * GDN_WQ_ROWS:(n + 1) * GDN_WQ_ROWS, :],
                                             states[j].astype(BF16),
                                             preferred_element_type=F32))
        v_new = each(lambda j, b, h: (u_ref[b, rows, hcols(h)] - ws_qs[j][:c]).astype(BF16))
        kv_qv = each(lambda j, b, h: jnp.dot(kq_ref[h, b, n * GDN_KQ_ROWS:(n + 1) * GDN_KQ_ROWS, :], v_new[j],
                                             preferred_element_type=F32))
        states = each(lambda j, b, h: states[j] * egl_ref[b, n * c:n * c + 1, h:h + 1] + kv_qv[j][:GDN_HEAD_DIM])
        for j, (b, h) in enumerate(chains):
            o = _rms(ws_qs[j][c:] + kv_qv[j][GDN_HEAD_DIM:], ng_ref[...])
            og_ref[b, rows, hcols(h)] = (o * _silu(gz_ref[b, rows, hcols(h)])).astype(BF16)
    for j in range(len(chains)):
        state_ref[j] = states[j]


def _gdn_scan(egl3, u3, wq4, kq4, gz3, norm_g, g):
    nb, s, _ = u3.shape
    nchunks = s // GDN_CHUNK
    blk = lambda i: (0, i, 0)
    hblk = lambda i: (0, 0, i, 0)
    return pl.pallas_call(
        functools.partial(_gdn_scan_kernel, nb=nb, g=g),
        grid=(nchunks // g,),
        in_specs=[pl.BlockSpec((nb, g * GDN_CHUNK, LANES), blk),
                  pl.BlockSpec((nb, g * GDN_CHUNK, GDN_WIDTH), blk),
                  pl.BlockSpec((GDN_HEADS, nb, g * GDN_WQ_ROWS, GDN_HEAD_DIM), hblk),
                  pl.BlockSpec((GDN_HEADS, nb, g * GDN_KQ_ROWS, GDN_CHUNK), hblk),
                  pl.BlockSpec((nb, g * GDN_CHUNK, GDN_WIDTH), blk),
                  pl.BlockSpec((1, GDN_HEAD_DIM), lambda i: (0, 0))],
        out_specs=pl.BlockSpec((nb, g * GDN_CHUNK, GDN_WIDTH), blk),
        out_shape=jax.ShapeDtypeStruct((nb, s, GDN_WIDTH), BF16),
        scratch_shapes=[pltpu.VMEM((nb * GDN_HEADS, GDN_HEAD_DIM, GDN_HEAD_DIM), F32)],
        compiler_params=pltpu.CompilerParams(dimension_semantics=("arbitrary",),
                                             vmem_limit_bytes=VMEM_LIMIT),
        name="gdn_scan",
    )(egl3, u3, wq4, kq4, gz3, norm_g)


FFN_FC = 256


def _ffn_kernel(x_ref, oa_ref, og_ref, wa_ref, wg_ref, gffn_ref, wup_ref, cw_ref, cb_ref, wd_ref, g_ref, out_ref,
                ubuf_ref, carry_ref, acc_ref, perm_ref, *, tm, tiles_per_seq):
    i = pl.program_id(0)
    tile = SUBLANES
    halo = (FFN_CONV - 1) * tile
    first = (i % tiles_per_seq) == 0
    x1 = _permute_rows(perm_ref, x_ref[...] + jnp.dot(oa_ref[...], wa_ref[...], preferred_element_type=F32)
                       + jnp.dot(og_ref[...], wg_ref[...], preferred_element_type=F32))
    acc_ref[...] = x1
    h2 = _rms(x1, gffn_ref[...]).astype(BF16)

    @pl.when(i == 0)
    def _():
        carry_ref[...] = jnp.zeros_like(carry_ref)

    nslab = D_FF // FFN_FC
    col_of = lambda cidx, part: part * D_FF + cidx * FFN_FC
    slot_of = lambda cidx, part: 2 * (cidx % 2) + part
    row8 = lax.broadcasted_iota(jnp.int32, (tile, FFN_FC), 0)

    def up(cidx):
        for part in range(2):
            lo, slot = col_of(cidx, part), slot_of(cidx, part)
            u = jnp.dot(h2, wup_ref[:, lo:lo + FFN_FC], preferred_element_type=F32)
            prev = carry_ref[:, lo:lo + FFN_FC]
            prev = jnp.where(first, jnp.zeros_like(prev), prev)
            for d in range(1, FFN_CONV):
                cur_t = u[tm - d * tile:tm - (d - 1) * tile, :]
                prev_t = prev[halo - d * tile:halo - (d - 1) * tile, :]
                ubuf_ref[slot, halo - d * tile:halo - (d - 1) * tile, :] = jnp.where(
                    row8 == 0, pltpu.roll(prev_t, 1, 0), pltpu.roll(cur_t, 1, 0))
            ubuf_ref[slot, halo:halo + tm, :] = u
            carry_ref[:, lo:lo + FFN_FC] = u[tm - halo:tm, :]

    def conv(cidx, part):
        lo, slot = col_of(cidx, part), slot_of(cidx, part)
        cw = cw_ref[:, lo:lo + FFN_FC]
        y = cb_ref[:, lo:lo + FFN_FC] + cw[FFN_CONV - 1:FFN_CONV] * ubuf_ref[slot, halo:halo + tm, :]
        for d in range(1, FFN_CONV):
            y = y + cw[FFN_CONV - 1 - d:FFN_CONV - d] * ubuf_ref[slot, halo - d * tile:halo - d * tile + tm, :]
        return y

    def down(cidx, act):
        acc_ref[...] += jnp.dot(act, wd_ref[cidx * FFN_FC:(cidx + 1) * FFN_FC, :], preferred_element_type=F32)

    act_prev = None
    for slot in range(nslab + 2):
        if slot < nslab:
            up(slot)
        act_cur = (_silu(conv(slot - 1, 0)) * conv(slot - 1, 1)).astype(BF16) if 1 <= slot <= nslab else None
        if slot >= 2:
            down(slot - 2, act_prev)
        act_prev = act_cur

    _unpermute_rows(perm_ref, _rms(acc_ref[...], g_ref[...]), out_ref)


def _ffn(x2, oa2, og2, wo_a, wo_g, g_ffn, w_up, conv_w, conv_b, w_down, g, tm, tiles_per_seq):
    m = x2.shape[0]
    row = lambda i: (i, 0)
    fixed = lambda i: (0, 0)
    once = pl.Buffered(1)
    return pl.pallas_call(
        functools.partial(_ffn_kernel, tm=tm, tiles_per_seq=tiles_per_seq),
        grid=(m // tm,),
        in_specs=[pl.BlockSpec((tm, D_MODEL), row),
                  pl.BlockSpec((tm, DA_WIDTH), row),
                  pl.BlockSpec((tm, GDN_WIDTH), row),
                  pl.BlockSpec((DA_WIDTH, D_MODEL), fixed, pipeline_mode=once),
                  pl.BlockSpec((GDN_WIDTH, D_MODEL), fixed, pipeline_mode=once),
                  pl.BlockSpec((1, D_MODEL), fixed),
                  pl.BlockSpec((D_MODEL, 2 * D_FF), fixed, pipeline_mode=once),
                  pl.BlockSpec((FFN_CONV, 2 * D_FF), fixed),
                  pl.BlockSpec((1, 2 * D_FF), fixed),
                  pl.BlockSpec((D_FF, D_MODEL), fixed, pipeline_mode=once),
                  pl.BlockSpec((1, D_MODEL), fixed)],
        out_specs=pl.BlockSpec((tm, D_MODEL), row),
        out_shape=jax.ShapeDtypeStruct((m, D_MODEL), F32),
        scratch_shapes=[pltpu.VMEM((4, tm + (FFN_CONV - 1) * SUBLANES, FFN_FC), F32),
                        pltpu.VMEM(((FFN_CONV - 1) * SUBLANES, 2 * D_FF), F32),
                        pltpu.VMEM((tm, D_MODEL), F32),
                        pltpu.VMEM((D_MODEL // LANES, SUBLANES * PERM_PITCH, LANES), F32)],
        compiler_params=pltpu.CompilerParams(dimension_semantics=("arbitrary",),
                                             vmem_limit_bytes=VMEM_LIMIT),
        name="ffn",
    )(x2, oa2, og2, wo_a, wo_g, g_ffn, w_up, conv_w, conv_b, w_down, g)


def _pad_lanes(v):
    return jnp.zeros((1, LANES), F32).at[0, :v.shape[0]].set(v.astype(F32))


def kernel(x, attn_norm_g, w_in, da_lambda_q1, da_lambda_k1, da_lambda_q2, da_lambda_k2, da_subln_g,
           gdn_conv_w, gdn_a_log, gdn_dt_bias, gdn_norm_g, w_out, ffn_norm_g, w_up, ffn_conv_w,
           ffn_conv_b, w_down, final_norm_g):
    b, s, d = x.shape
    m = b * s
    tm = 512
    assert d == D_MODEL and s % tm == 0 and w_in.shape[0] == 1, (x.shape, w_in.shape)
    l = 0
    x2 = x.reshape(m, d)

    wi = w_in[l]
    w_main = wi[:, :W_MAIN_COLS].astype(BF16)
    w_z = wi[:, W_MAIN_COLS + 2 * GDN_HEADS:].astype(BF16)
    w_ab = jnp.pad(wi[:, W_MAIN_COLS:W_MAIN_COLS + 2 * GDN_HEADS], ((0, 0), (0, LANES - 2 * GDN_HEADS))).astype(BF16)
    lam_p = jnp.stack([da_lambda_q1[l], da_lambda_k1[l], da_lambda_q2[l], da_lambda_k2[l]]).astype(F32)

    qk, va, gqkv, gz, gate = _in_proj(x2, attn_norm_g[l].reshape(1, d).astype(F32), w_main, w_z, w_ab,
                                      gdn_conv_w[l].astype(F32), _pad_lanes(gdn_a_log[l]),
                                      _pad_lanes(gdn_dt_bias[l]), tm, s // tm)

    oa = _diff_attn(lam_p, da_subln_g[l].reshape(1, DA_V_DIM).astype(F32),
                    qk.reshape(b, s, 2 * DA_WIDTH), va.reshape(b, s, 2 * DA_WIDTH), 512)

    u, wq, kq, egl = _gdn_prep(gqkv, gate, 512)
    nchunks = s // GDN_CHUNK
    og = _gdn_scan(egl.reshape(b, s, LANES), u.reshape(b, s, GDN_WIDTH),
                   wq.reshape(GDN_HEADS, b, nchunks * GDN_WQ_ROWS, GDN_HEAD_DIM),
                   kq.reshape(GDN_HEADS, b, nchunks * GDN_KQ_ROWS, GDN_CHUNK),
                   gz.reshape(b, s, GDN_WIDTH), gdn_norm_g[l].reshape(1, GDN_HEAD_DIM).astype(F32), 8)

    wo = w_out[l].astype(BF16)
    out = _ffn(x2, oa.reshape(m, DA_WIDTH), og.reshape(m, GDN_WIDTH), wo[:DA_WIDTH], wo[DA_WIDTH:],
               ffn_norm_g[l].reshape(1, d).astype(F32), w_up[l].astype(BF16), ffn_conv_w[l].astype(F32),
               ffn_conv_b[l].reshape(1, 2 * D_FF).astype(F32), w_down[l].astype(BF16),
               final_norm_g.reshape(1, d).astype(F32), tm, s // tm)
    return out.reshape(b, s, d)
```

```python
import functools
import math

import jax
import jax.numpy as jnp
from jax import lax
from jax.experimental import pallas as pl
from jax.experimental.pallas import tpu as pltpu

F32 = jnp.float32
BF16 = jnp.bfloat16

EPS = 1e-6
LOG2E = 1.4426950408889634
NEG_BIG = -1e30

D_MODEL = 1024
DA_HEADS = 4
DA_HEAD_DIM = 64
DA_V_DIM = 128
DA_WIDTH = DA_HEADS * DA_V_DIM
GDN_HEADS = 4
GDN_HEAD_DIM = 128
GDN_WIDTH = GDN_HEADS * GDN_HEAD_DIM
GDN_CONV = 4
GDN_CHUNK = 64
D_FF = 2816
FFN_CONV = 3
LAM_INIT = 0.8 - 0.6 * math.exp(-0.3 * 0)

LANES = 128
SUBLANES = 8
VMEM_LIMIT = 56 * 1024 * 1024

W_MAIN_COLS = 3 * DA_WIDTH + 3 * GDN_WIDTH


def _mm(a, b):
    return jnp.dot(a.astype(BF16), b.astype(BF16), preferred_element_type=F32)


def _mm_nt(a, b):
    return lax.dot_general(a.astype(BF16), b.astype(BF16), (((1,), (1,)), ((), ())),
                           preferred_element_type=F32)


def _sum3(ones_bf16, x, left=True):
    hi = x.astype(BF16)
    r1 = x - hi.astype(F32)
    mid = r1.astype(BF16)
    lo = (r1 - mid.astype(F32)).astype(BF16)
    if left:
        dot = lambda t: jnp.dot(ones_bf16, t, preferred_element_type=F32)
    else:
        dot = lambda t: jnp.dot(t, ones_bf16, preferred_element_type=F32)
    return dot(hi) + dot(mid) + dot(lo)


def _silu(x):
    hx = 0.5 * x
    return hx + hx * jnp.tanh(hx)


def _rms(x, g):
    return x * lax.rsqrt(jnp.mean(x * x, axis=-1, keepdims=True) + EPS) * g


PERM_PITCH = 72


def _permute_rows(scr_ref, a, slab0=0, nseg=SUBLANES):
    tm, ncols = a.shape
    seg = tm // nseg
    slabs = []
    for k in range(ncols // LANES):
        for s in range(nseg):
            scr_ref[slab0 + k, s * PERM_PITCH:s * PERM_PITCH + seg, :] = (
                a[s * seg:(s + 1) * seg, k * LANES:(k + 1) * LANES])
        slabs.append(jnp.concatenate(
            [scr_ref[slab0 + k, pl.ds(j, nseg, stride=PERM_PITCH), :] for j in range(seg)], axis=0))
    return jnp.concatenate(slabs, axis=1)


def _unpermute_rows(scr_ref, ap, out_ref, col0=0, slab0=0, nseg=SUBLANES):
    tm, ncols = ap.shape
    seg = tm // nseg
    for k in range(ncols // LANES):
        for j in range(seg):
            scr_ref[slab0 + k, pl.ds(j, nseg, stride=PERM_PITCH), :] = (
                ap[j * nseg:(j + 1) * nseg, k * LANES:(k + 1) * LANES])
        for s in range(nseg):
            out_ref[s * seg:(s + 1) * seg, col0 + k * LANES:col0 + (k + 1) * LANES] = (
                scr_ref[slab0 + k, s * PERM_PITCH:s * PERM_PITCH + seg, :])


def _in_proj_kernel(x_ref, g_ref, w_ref, wz_ref, wab_ref, cw_ref, alog_ref, dtb_ref,
                    qk_ref, va_ref, gqkv_ref, gz_ref, gate_ref, cbuf_ref, carry_ref, perm_ref,
                    *, tm, tiles_per_seq):
    i = pl.program_id(0)
    tile = SUBLANES
    halo = (GDN_CONV - 1) * tile
    first = (i % tiles_per_seq) == 0
    h = _rms(x_ref[...], g_ref[...])
    hb = h.astype(BF16)
    hb_perm = _permute_rows(perm_ref, h).astype(BF16)

    def proj(lo, width, ref=w_ref, lhs=hb):
        return jnp.dot(lhs, ref[:, lo:lo + width], preferred_element_type=F32)

    lane = lax.broadcasted_iota(jnp.int32, (tm, LANES), 1)
    base = 3 * DA_WIDTH
    half = DA_WIDTH // 2
    row8 = lax.broadcasted_iota(jnp.int32, (tile, half), 0)

    @pl.when(i == 0)
    def _():
        carry_ref[...] = jnp.zeros_like(carry_ref)

    def gdn_part(c):
        cols = slice(c * half, (c + 1) * half)
        u = proj(base + c * half, half, lhs=hb_perm)
        prev = carry_ref[:, cols]
        prev = jnp.where(first, jnp.zeros_like(prev), prev)
        for d in range(1, GDN_CONV):
            cur_t = u[tm - d * tile:tm - (d - 1) * tile, :]
            prev_t = prev[halo - d * tile:halo - (d - 1) * tile, :]
            cbuf_ref[halo - d * tile:halo - (d - 1) * tile, cols] = jnp.where(
                row8 == 0, pltpu.roll(prev_t, 1, 0), pltpu.roll(cur_t, 1, 0))
        cbuf_ref[halo:halo + tm, cols] = u
        carry_ref[:, cols] = u[tm - halo:tm, :]

    def q_part(c):
        qk_ref[:, c * half:(c + 1) * half] = (proj(c * half, half) * (DA_HEAD_DIM ** -0.5 * LOG2E)).astype(BF16)

    def k_part(c):
        qk_ref[:, DA_WIDTH + c * half:DA_WIDTH + (c + 1) * half] = proj(DA_WIDTH + c * half, half).astype(BF16)

    def v_part(c):
        v = proj(2 * DA_WIDTH + c * half, half)
        ones_col = jnp.ones((tm, LANES), BF16)
        for hh in range(2):
            h = 2 * c + hh
            va_ref[:, 2 * h * LANES:(2 * h + 1) * LANES] = v[:, hh * DA_V_DIM:(hh + 1) * DA_V_DIM].astype(BF16)
            va_ref[:, (2 * h + 1) * LANES:(2 * h + 2) * LANES] = ones_col

    def z_part(c):
        gz_ref[:, c * half:(c + 1) * half] = proj(c * half, half, wz_ref).astype(BF16)

    def gate_part(_):
        gab = proj(0, LANES, wab_ref)
        sp_in = gab + dtb_ref[...]
        softplus = jnp.maximum(sp_in, 0.0) + jnp.log(1.0 + jnp.exp(-jnp.abs(sp_in)))
        gate_ref[...] = jnp.where(lane < GDN_HEADS, -jnp.exp(alog_ref[...]) * softplus,
                                  1.0 / (1.0 + jnp.exp(-gab)))

    def conv_group(c):
        cols = slice(c * GDN_HEAD_DIM, (c + 1) * GDN_HEAD_DIM)
        cw = cw_ref[:, cols]
        xc = cw[GDN_CONV - 1:GDN_CONV] * cbuf_ref[halo:halo + tm, cols]
        for d in range(1, GDN_CONV):
            xc = xc + cw[GDN_CONV - 1 - d:GDN_CONV - d] * cbuf_ref[halo - d * tile:halo - d * tile + tm, cols]
        xc = _silu(xc)
        if c < GDN_HEADS:
            xc = xc * (lax.rsqrt(jnp.sum(xc * xc, axis=-1, keepdims=True) + EPS) * GDN_HEAD_DIM ** -0.5)
        elif c < 2 * GDN_HEADS:
            xc = xc * lax.rsqrt(jnp.sum(xc * xc, axis=-1, keepdims=True) + EPS)
        _unpermute_rows(perm_ref, xc, gqkv_ref, col0=c * GDN_HEAD_DIM, slab0=c % (D_MODEL // LANES))

    matmul_tasks = ([(gdn_part, c) for c in range(6)] + [(q_part, 0), (q_part, 1), (k_part, 0), (k_part, 1),
                    (v_part, 0), (v_part, 1), (z_part, 0), (z_part, 1), (gate_part, 0)])
    ngroups = 3 * GDN_HEADS
    issued = 0
    for c in range(ngroups):
        want = min(len(matmul_tasks), 4 + 2 * (c // 2))
        while issued < want:
            fn, arg = matmul_tasks[issued]
            fn(arg)
            issued += 1
        conv_group(c)
    for fn, arg in matmul_tasks[issued:]:
        fn(arg)


def _in_proj(x2, g, w_main, w_z, w_ab, conv_w, alog_row, dtb_row, tm, tiles_per_seq):
    m = x2.shape[0]
    row = lambda i: (i, 0)
    fixed = lambda i: (0, 0)
    return pl.pallas_call(
        functools.partial(_in_proj_kernel, tm=tm, tiles_per_seq=tiles_per_seq),
        grid=(m // tm,),
        in_specs=[pl.BlockSpec((tm, D_MODEL), row),
                  pl.BlockSpec((1, D_MODEL), fixed),
                  pl.BlockSpec((D_MODEL, W_MAIN_COLS), fixed),
                  pl.BlockSpec((D_MODEL, GDN_WIDTH), fixed),
                  pl.BlockSpec((D_MODEL, LANES), fixed),
                  pl.BlockSpec((GDN_CONV, 3 * GDN_WIDTH), fixed),
                  pl.BlockSpec((1, LANES), fixed),
                  pl.BlockSpec((1, LANES), fixed)],
        out_specs=[pl.BlockSpec((tm, 2 * DA_WIDTH), row),
                   pl.BlockSpec((tm, 2 * DA_WIDTH), row),
                   pl.BlockSpec((tm, 3 * GDN_WIDTH), row),
                   pl.BlockSpec((tm, GDN_WIDTH), row),
                   pl.BlockSpec((tm, LANES), row)],
        out_shape=[jax.ShapeDtypeStruct((m, 2 * DA_WIDTH), BF16),
                   jax.ShapeDtypeStruct((m, 2 * DA_WIDTH), BF16),
                   jax.ShapeDtypeStruct((m, 3 * GDN_WIDTH), F32),
                   jax.ShapeDtypeStruct((m, GDN_WIDTH), BF16),
                   jax.ShapeDtypeStruct((m, LANES), F32)],
        scratch_shapes=[pltpu.VMEM((tm + (GDN_CONV - 1) * SUBLANES, 3 * GDN_WIDTH), F32),
                        pltpu.VMEM(((GDN_CONV - 1) * SUBLANES, 3 * GDN_WIDTH), F32),
                        pltpu.VMEM((D_MODEL // LANES, SUBLANES * PERM_PITCH, LANES), F32)],
        compiler_params=pltpu.CompilerParams(dimension_semantics=("arbitrary",),
                                             vmem_limit_bytes=VMEM_LIMIT),
        name="in_proj",
    )(x2, g, w_main, w_z, w_ab, conv_w, alog_row, dtb_row)


ATTN_HEADS_PER_STEP = 4


def _attn_kernel(lam_ref, sg_ref, q_ref, k_ref, v_ref, o_ref, acc_ref, m_ref, *, t, hp):
    hg = pl.program_id(1)
    qi = pl.program_id(2)

    lane = lax.broadcasted_iota(jnp.int32, (t, LANES), 1)
    col_f = lax.broadcasted_iota(jnp.int32, (1, t), 1).astype(F32)
    row_i = lax.broadcasted_iota(jnp.int32, (t, t), 0)
    col_i = lax.broadcasted_iota(jnp.int32, (t, t), 1)

    qm, slope_row = [], []
    for j in range(hp):
        q = q_ref[0, :, j * LANES:(j + 1) * LANES]
        zero = jnp.zeros_like(q)
        qm += [jnp.where(lane < DA_HEAD_DIM, q, zero), jnp.where(lane >= DA_HEAD_DIM, q, zero)]
        hf = (hg * hp + j + 1).astype(F32)
        slope_row.append(jnp.exp2(jnp.full((1, t), -8.0 / DA_HEADS, F32) * hf) * LOG2E)

    acc_ref[...] = jnp.zeros_like(acc_ref)

    nchain = 2 * hp

    m_ref[...] = jnp.full(m_ref.shape, NEG_BIG, F32)
    lane_tiles = lambda x, n: jnp.concatenate([x] * n, axis=1)

    def step(blocks):
        ms = [m_ref[idx] for idx in range(nchain)]
        work = [(ki, masked, idx) for ki, masked in blocks for idx in range(nchain)]

        def logits(ki, masked, idx):
            j = idx // 2
            k = k_ref[0, pl.ds(pl.multiple_of(ki * t, t), t), j * LANES:(j + 1) * LANES]
            pos = ((ki - qi) * t).astype(F32) + col_f
            s = (lax.dot_general(qm[idx], k, (((1,), (1,)), ((), ())), preferred_element_type=F32)
                 + pos * slope_row[j])
            return jnp.where(col_i <= row_i, s, NEG_BIG) if masked else s

        def probs(idx, s):
            m_new = jnp.maximum(ms[idx], jnp.max(s, axis=-1, keepdims=True))
            return m_new, jnp.exp2(s - lane_tiles(m_new, t // LANES)).astype(BF16)

        def accumulate(ki, idx, m_new, p):
            j = idx // 2
            v = v_ref[0, pl.ds(pl.multiple_of(ki * t, t), t), 2 * j * LANES:2 * (j + 1) * LANES]
            acc_ref[idx] = (lane_tiles(jnp.exp2(ms[idx] - m_new), 2) * acc_ref[idx]
                            + jnp.dot(p, v, preferred_element_type=F32))
            ms[idx] = m_new

        s_prev, mp_prev = None, None
        for slot in range(len(work) + 2):
            s_cur = logits(*work[slot]) if slot < len(work) else None
            mp_cur = probs(work[slot - 1][2], s_prev) if 1 <= slot <= len(work) else None
            if slot >= 2:
                accumulate(work[slot - 2][0], work[slot - 2][2], *mp_prev)
            s_prev, mp_prev = s_cur, mp_cur
        for idx in range(nchain):
            m_ref[idx] = ms[idx]

    npair = qi // 2
    odd = qi - 2 * npair

    def run(trips, blocks_of):
        def body(j, carry):
            step(blocks_of(j))
            return carry
        lax.fori_loop(0, trips, body, 0)

    run(npair, lambda j: [(2 * j, False), (2 * j + 1, False)])
    run(odd, lambda j: [(qi - 1, False), (qi, True)])
    run(1 - odd, lambda j: [(qi, True)])

    lam_p = lam_ref[...]
    lam = (jnp.exp(jnp.sum(lam_p[0:1] * lam_p[1:2], axis=-1, keepdims=True))
           - jnp.exp(jnp.sum(lam_p[2:3] * lam_p[3:4], axis=-1, keepdims=True)) + LAM_INIT)
    ones_sq = jnp.ones((DA_V_DIM, DA_V_DIM), BF16)
    for j in range(hp):
        a1 = acc_ref[2 * j]
        a2 = acc_ref[2 * j + 1]
        o = a1[:, :DA_V_DIM] / a1[:, DA_V_DIM:] - lam * (a2[:, :DA_V_DIM] / a2[:, DA_V_DIM:])
        o_ref[0, :, j * DA_V_DIM:(j + 1) * DA_V_DIM] = (
            o * lax.rsqrt(_sum3(ones_sq, o * o, left=False) * (1.0 / DA_V_DIM) + EPS)
            * sg_ref[...] * (1.0 - LAM_INIT)).astype(BF16)


def _diff_attn(lam_p, subln_g, qk3, va3, t):
    b, s, _ = qk3.shape
    hp = ATTN_HEADS_PER_STEP
    ngroups = DA_HEADS // hp
    return pl.pallas_call(
        functools.partial(_attn_kernel, t=t, hp=hp),
        grid=(b, ngroups, s // t),
        in_specs=[pl.BlockSpec((4, DA_HEAD_DIM), lambda bi, hg, qi: (0, 0)),
                  pl.BlockSpec((1, DA_V_DIM), lambda bi, hg, qi: (0, 0)),
                  pl.BlockSpec((1, t, hp * LANES), lambda bi, hg, qi: (bi, qi, hg)),
                  pl.BlockSpec((1, s, hp * LANES), lambda bi, hg, qi: (bi, 0, ngroups + hg),
                               pipeline_mode=pl.Buffered(1)),
                  pl.BlockSpec((1, s, 2 * hp * LANES), lambda bi, hg, qi: (bi, 0, hg),
                               pipeline_mode=pl.Buffered(1))],
        out_specs=pl.BlockSpec((1, t, hp * DA_V_DIM), lambda bi, hg, qi: (bi, qi, hg)),
        out_shape=jax.ShapeDtypeStruct((b, s, DA_WIDTH), BF16),
        scratch_shapes=[pltpu.VMEM((2 * hp, t, 2 * LANES), F32),
                        pltpu.VMEM((2 * hp, t, LANES), F32)],
        compiler_params=pltpu.CompilerParams(dimension_semantics=("arbitrary", "arbitrary", "arbitrary"),
                                             vmem_limit_bytes=VMEM_LIMIT),
        name="diff_attn",
    )(lam_p, subln_g, qk3, qk3, va3)


GDN_GROUP = 2 * GDN_CHUNK
GDN_WQ_ROWS = 2 * GDN_CHUNK
GDN_KQ_ROWS = GDN_HEAD_DIM + GDN_CHUNK


def _gdn_prep_kernel(gqkv_ref, gate_ref, u_ref, wq_ref, kq_ref, egl_ref, *, rb):
    c = GDN_CHUNK
    r = GDN_GROUP
    ri = lax.broadcasted_iota(jnp.int32, (r, r), 0)
    ci = lax.broadcasted_iota(jnp.int32, (r, r), 1)
    rx = ri ^ ci
    same_chunk = rx < c
    lower = ci <= ri
    strict = ci < ri
    eye = jnp.where(ci == ri, 1.0, 0.0).astype(F32)
    tri_ones = jnp.where(same_chunk, jnp.where(lower, 1.0, 0.0), 0.0).astype(BF16)
    blk_ones = jnp.where(same_chunk, 1.0, 0.0).astype(BF16)

    ngroups = rb // r
    chains = [(gidx, h) for gidx in range(ngroups) for h in range(GDN_HEADS)]
    each = lambda f: [f(j) for j in range(len(chains))]
    rows_of = lambda j: slice(chains[j][0] * r, (chains[j][0] + 1) * r)
    head_of = lambda j: chains[j][1]

    gate = [gate_ref[gidx * r:(gidx + 1) * r, :] for gidx in range(ngroups)]
    gc_all = [_sum3(tri_ones, gt) for gt in gate]
    gl_all = [_sum3(blk_ones, gt) for gt in gate]
    gc_t = [jnp.transpose(x) for x in gc_all]
    for gidx in range(ngroups):
        egl = jnp.exp(gl_all[gidx])
        for n in range(r // c):
            ch = gidx * (r // c) + n
            egl_ref[ch:ch + 1, :] = egl[n * c:n * c + 1, :]

    def hslice(j, part):
        lo = part * GDN_WIDTH + head_of(j) * GDN_HEAD_DIM
        return gqkv_ref[rows_of(j), lo:lo + GDN_HEAD_DIM]

    qh = each(lambda j: hslice(j, 0))
    kh = each(lambda j: hslice(j, 1))
    vh = each(lambda j: hslice(j, 2))
    beta = each(lambda j: gate[chains[j][0]][:, GDN_HEADS + head_of(j):GDN_HEADS + head_of(j) + 1])
    gc = each(lambda j: gc_all[chains[j][0]][:, head_of(j):head_of(j) + 1])
    gl = each(lambda j: gl_all[chains[j][0]][:, head_of(j):head_of(j) + 1])
    gc_row = each(lambda j: gc_t[chains[j][0]][head_of(j):head_of(j) + 1, :])

    decay = each(lambda j: jnp.exp(jnp.where(same_chunk, jnp.where(lower, gc[j] - gc_row[j], -jnp.inf), -jnp.inf)))
    kb = each(lambda j: kh[j] * beta[j])
    vb = each(lambda j: vh[j] * beta[j])
    kbf = each(lambda j: kh[j].astype(BF16))
    kk = each(lambda j: _mm_nt(kb[j], kbf[j]))
    qk = each(lambda j: _mm_nt(qh[j], kbf[j]))
    lmat = each(lambda j: jnp.where(strict, kk[j] * decay[j], 0.0))
    qk = each(lambda j: (qk[j] * decay[j]).astype(BF16))

    x1 = each(lambda j: jnp.where(rx < 16, -lmat[j], 0.0))
    p = each(lambda j: eye + x1[j])
    x2 = each(lambda j: _mm(x1[j], x1[j]))
    x2p = each(lambda j: _mm(x2[j], p[j]))
    x4 = each(lambda j: _mm(x2[j], x2[j]))
    p = each(lambda j: p[j] + x2p[j])
    x4p = each(lambda j: _mm(x4[j], p[j]))
    x8 = each(lambda j: _mm(x4[j], x4[j]))
    p = each(lambda j: p[j] + x4p[j])
    x8p = each(lambda j: _mm(x8[j], p[j]))
    p = each(lambda j: p[j] + x8p[j])
    b32 = each(lambda j: jnp.where(rx < 32, jnp.where(rx >= 16, lmat[j], 0.0), 0.0))
    t1 = each(lambda j: _mm(b32[j], p[j]))
    t2 = each(lambda j: _mm(p[j], t1[j]))
    p = each(lambda j: p[j] - t2[j])
    b64 = each(lambda j: jnp.where(rx >= 32, lmat[j], 0.0))
    t1 = each(lambda j: _mm(b64[j], p[j]))
    t2 = each(lambda j: _mm(p[j], t1[j]))
    tmat = each(lambda j: p[j] - t2[j])

    eg = each(lambda j: jnp.exp(gc[j]))
    uw = each(lambda j: _mm(tmat[j], jnp.concatenate([vb[j], kb[j] * eg[j]], axis=1)))
    qdec = each(lambda j: (qh[j] * eg[j]).astype(BF16))
    kdec_t = each(lambda j: jnp.transpose(kh[j] * jnp.exp(gl[j] - gc[j])).astype(BF16))
    for j, (gidx, h) in enumerate(chains):
        u_ref[rows_of(j), h * GDN_HEAD_DIM:(h + 1) * GDN_HEAD_DIM] = uw[j][:, :GDN_HEAD_DIM].astype(BF16)
        w = uw[j][:, GDN_HEAD_DIM:].astype(BF16)
        for n in range(r // c):
            ch = gidx * (r // c) + n
            cr = slice(n * c, (n + 1) * c)
            wq_ref[h, ch * GDN_WQ_ROWS:ch * GDN_WQ_ROWS + c, :] = w[cr]
            wq_ref[h, ch * GDN_WQ_ROWS + c:(ch + 1) * GDN_WQ_ROWS, :] = qdec[j][cr]
            kq_ref[h, ch * GDN_KQ_ROWS:ch * GDN_KQ_ROWS + GDN_HEAD_DIM, :] = kdec_t[j][:, cr]
            kq_ref[h, ch * GDN_KQ_ROWS + GDN_HEAD_DIM:(ch + 1) * GDN_KQ_ROWS, :] = qk[j][cr, cr]


def _gdn_prep(gqkv, gate, rb):
    m = gqkv.shape[0]
    nch = rb // GDN_CHUNK
    row = lambda i: (i, 0)
    hrow = lambda i: (0, i, 0)
    return pl.pallas_call(
        functools.partial(_gdn_prep_kernel, rb=rb),
        grid=(m // rb,),
        in_specs=[pl.BlockSpec((rb, 3 * GDN_WIDTH), row),
                  pl.BlockSpec((rb, LANES), row)],
        out_specs=[pl.BlockSpec((rb, GDN_WIDTH), row),
                   pl.BlockSpec((GDN_HEADS, nch * GDN_WQ_ROWS, GDN_HEAD_DIM), hrow),
                   pl.BlockSpec((GDN_HEADS, nch * GDN_KQ_ROWS, GDN_CHUNK), hrow),
                   pl.BlockSpec((nch, LANES), row)],
        out_shape=[jax.ShapeDtypeStruct((m, GDN_WIDTH), BF16),
                   jax.ShapeDtypeStruct((GDN_HEADS, m // GDN_CHUNK * GDN_WQ_ROWS, GDN_HEAD_DIM), BF16),
                   jax.ShapeDtypeStruct((GDN_HEADS, m // GDN_CHUNK * GDN_KQ_ROWS, GDN_CHUNK), BF16),
                   jax.ShapeDtypeStruct((m // GDN_CHUNK, LANES), F32)],
        compiler_params=pltpu.CompilerParams(dimension_semantics=("arbitrary",),
                                             vmem_limit_bytes=VMEM_LIMIT),
        name="gdn_prep",
    )(gqkv, gate)


def _gdn_scan_kernel(egl_ref, u_ref, wq_ref, kq_ref, gz_ref, ng_ref, og_ref, state_ref, *, nb, g):
    i = pl.program_id(0)
    c = GDN_CHUNK

    @pl.when(i == 0)
    def _():
        state_ref[...] = jnp.zeros_like(state_ref)

    chains = [(b, h) for b in range(nb) for h in range(GDN_HEADS)]
    each = lambda f: [f(j, *chains[j]) for j in range(len(chains))]
    hcols = lambda h: slice(h * GDN_HEAD_DIM, (h + 1) * GDN_HEAD_DIM)
    states = each(lambda j, b, h: state_ref[j])
    for n in range(g):
        rows = slice(n * c, (n + 1) * c)
        ws_qs = each(lambda j, b, h: jnp.dot(wq_ref[h, b, n * GDN_WQ_ROWS:(n + 1) * GDN_WQ_ROWS, :],
                                             states[j].astype(BF16),
                                             preferred_element_type=F32))
        v_new = each(lambda j, b, h: (u_ref[b, rows, hcols(h)].astype(F32) - ws_qs[j][:c]).astype(BF16))
        kv_qv = each(lambda j, b, h: jnp.dot(kq_ref[h, b, n * GDN_KQ_ROWS:(n + 1) * GDN_KQ_ROWS, :], v_new[j],
                                             preferred_element_type=F32))
        states = each(lambda j, b, h: states[j] * egl_ref[b, n:n + 1, h:h + 1] + kv_qv[j][:GDN_HEAD_DIM])
        for j, (b, h) in enumerate(chains):
            o = _rms(ws_qs[j][c:] + kv_qv[j][GDN_HEAD_DIM:], ng_ref[...])
            og_ref[b, rows, hcols(h)] = (o * _silu(gz_ref[b, rows, hcols(h)].astype(F32))).astype(BF16)
    for j in range(len(chains)):
        state_ref[j] = states[j]


def _gdn_scan(egl3, u3, wq4, kq4, gz3, norm_g, g):
    nb, s, _ = u3.shape
    nchunks = s // GDN_CHUNK
    blk = lambda i: (0, i, 0)
    hblk = lambda i: (0, 0, i, 0)
    return pl.pallas_call(
        functools.partial(_gdn_scan_kernel, nb=nb, g=g),
        grid=(nchunks // g,),
        in_specs=[pl.BlockSpec((nb, g, LANES), blk),
                  pl.BlockSpec((nb, g * GDN_CHUNK, GDN_WIDTH), blk),
                  pl.BlockSpec((GDN_HEADS, nb, g * GDN_WQ_ROWS, GDN_HEAD_DIM), hblk),
                  pl.BlockSpec((GDN_HEADS, nb, g * GDN_KQ_ROWS, GDN_CHUNK), hblk),
                  pl.BlockSpec((nb, g * GDN_CHUNK, GDN_WIDTH), blk),
                  pl.BlockSpec((1, GDN_HEAD_DIM), lambda i: (0, 0))],
        out_specs=pl.BlockSpec((nb, g * GDN_CHUNK, GDN_WIDTH), blk),
        out_shape=jax.ShapeDtypeStruct((nb, s, GDN_WIDTH), BF16),
        scratch_shapes=[pltpu.VMEM((nb * GDN_HEADS, GDN_HEAD_DIM, GDN_HEAD_DIM), F32)],
        compiler_params=pltpu.CompilerParams(dimension_semantics=("arbitrary",),
                                             vmem_limit_bytes=VMEM_LIMIT),
        name="gdn_scan",
    )(egl3, u3, wq4, kq4, gz3, norm_g)


FFN_FC = 256


def _ffn_kernel(x_ref, oa_ref, og_ref, wa_ref, wg_ref, gffn_ref, wup_ref, cw_ref, cb_ref, wd_ref, g_ref, out_ref,
                ubuf_ref, carry_ref, acc_ref, perm_ref, *, tm, tiles_per_seq):
    i = pl.program_id(0)
    tile = SUBLANES
    halo = (FFN_CONV - 1) * tile
    first = (i % tiles_per_seq) == 0
    x1 = _permute_rows(perm_ref, x_ref[...] + jnp.dot(oa_ref[...], wa_ref[...], preferred_element_type=F32)
                       + jnp.dot(og_ref[...], wg_ref[...], preferred_element_type=F32))
    acc_ref[...] = x1
    h2 = _rms(x1, gffn_ref[...]).astype(BF16)

    @pl.when(i == 0)
    def _():
        carry_ref[...] = jnp.zeros_like(carry_ref)

    nslab = D_FF // FFN_FC
    col_of = lambda cidx, part: part * D_FF + cidx * FFN_FC
    slot_of = lambda cidx, part: 2 * (cidx % 2) + part
    row8 = lax.broadcasted_iota(jnp.int32, (tile, FFN_FC), 0)

    def up(cidx):
        for part in range(2):
            lo, slot = col_of(cidx, part), slot_of(cidx, part)
            u = jnp.dot(h2, wup_ref[:, lo:lo + FFN_FC], preferred_element_type=F32)
            prev = carry_ref[:, lo:lo + FFN_FC]
            prev = jnp.where(first, jnp.zeros_like(prev), prev)
            for d in range(1, FFN_CONV):
                cur_t = u[tm - d * tile:tm - (d - 1) * tile, :]
                prev_t = prev[halo - d * tile:halo - (d - 1) * tile, :]
                ubuf_ref[slot, halo - d * tile:halo - (d - 1) * tile, :] = jnp.where(
                    row8 == 0, pltpu.roll(prev_t, 1, 0), pltpu.roll(cur_t, 1, 0))
            ubuf_ref[slot, halo:halo + tm, :] = u
            carry_ref[:, lo:lo + FFN_FC] = u[tm - halo:tm, :]

    def conv(cidx, part):
        lo, slot = col_of(cidx, part), slot_of(cidx, part)
        cw = cw_ref[:, lo:lo + FFN_FC]
        y = cb_ref[:, lo:lo + FFN_FC] + cw[FFN_CONV - 1:FFN_CONV] * ubuf_ref[slot, halo:halo + tm, :]
        for d in range(1, FFN_CONV):
            y = y + cw[FFN_CONV - 1 - d:FFN_CONV - d] * ubuf_ref[slot, halo - d * tile:halo - d * tile + tm, :]
        return y

    def down(cidx, act):
        acc_ref[...] += jnp.dot(act, wd_ref[cidx * FFN_FC:(cidx + 1) * FFN_FC, :], preferred_element_type=F32)

    act_prev = None
    for slot in range(nslab + 2):
        if slot < nslab:
            up(slot)
        act_cur = (_silu(conv(slot - 1, 0)) * conv(slot - 1, 1)).astype(BF16) if 1 <= slot <= nslab else None
        if slot >= 2:
            down(slot - 2, act_prev)
        act_prev = act_cur

    _unpermute_rows(perm_ref, _rms(acc_ref[...], g_ref[...]), out_ref)


def _ffn(x2, oa2, og2, wo_a, wo_g, g_ffn, w_up, conv_w, conv_b, w_down, g, tm, tiles_per_seq):
    m = x2.shape[0]
    row = lambda i: (i, 0)
    fixed = lambda i: (0, 0)
    once = pl.Buffered(1)
    return pl.pallas_call(
        functools.partial(_ffn_kernel, tm=tm, tiles_per_seq=tiles_per_seq),
        grid=(m // tm,),
        in_specs=[pl.BlockSpec((tm, D_MODEL), row),
                  pl.BlockSpec((tm, DA_WIDTH), row),
                  pl.BlockSpec((tm, GDN_WIDTH), row),
                  pl.BlockSpec((DA_WIDTH, D_MODEL), fixed, pipeline_mode=once),
                  pl.BlockSpec((GDN_WIDTH, D_MODEL), fixed, pipeline_mode=once),
                  pl.BlockSpec((1, D_MODEL), fixed),
                  pl.BlockSpec((D_MODEL, 2 * D_FF), fixed, pipeline_mode=once),
                  pl.BlockSpec((FFN_CONV, 2 * D_FF), fixed),
                  pl.BlockSpec((1, 2 * D_FF), fixed),
                  pl.BlockSpec((D_FF, D_MODEL), fixed, pipeline_mode=once),
                  pl.BlockSpec((1, D_MODEL), fixed)],
        out_specs=pl.BlockSpec((tm, D_MODEL), row),
        out_shape=jax.ShapeDtypeStruct((m, D_MODEL), F32),
        scratch_shapes=[pltpu.VMEM((4, tm + (FFN_CONV - 1) * SUBLANES, FFN_FC), F32),
                        pltpu.VMEM(((FFN_CONV - 1) * SUBLANES, 2 * D_FF), F32),
                        pltpu.VMEM((tm, D_MODEL), F32),
                        pltpu.VMEM((D_MODEL // LANES, SUBLANES * PERM_PITCH, LANES), F32)],
        compiler_params=pltpu.CompilerParams(dimension_semantics=("arbitrary",),
                                             vmem_limit_bytes=VMEM_LIMIT),
        name="ffn",
    )(x2, oa2, og2, wo_a, wo_g, g_ffn, w_up, conv_w, conv_b, w_down, g)


def _pad_lanes(v):
    return jnp.zeros((1, LANES), F32).at[0, :v.shape[0]].set(v.astype(F32))


def kernel(x, attn_norm_g, w_in, da_lambda_q1, da_lambda_k1, da_lambda_q2, da_lambda_k2, da_subln_g,
           gdn_conv_w, gdn_a_log, gdn_dt_bias, gdn_norm_g, w_out, ffn_norm_g, w_up, ffn_conv_w,
           ffn_conv_b, w_down, final_norm_g):
    b, s, d = x.shape
    m = b * s
    tm = 512
    assert d == D_MODEL and s % tm == 0 and w_in.shape[0] == 1, (x.shape, w_in.shape)
    l = 0
    x2 = x.reshape(m, d)

    wi = w_in[l]
    w_main = wi[:, :W_MAIN_COLS].astype(BF16)
    w_z = wi[:, W_MAIN_COLS + 2 * GDN_HEADS:].astype(BF16)
    w_ab = jnp.pad(wi[:, W_MAIN_COLS:W_MAIN_COLS + 2 * GDN_HEADS], ((0, 0), (0, LANES - 2 * GDN_HEADS))).astype(BF16)
    lam_p = jnp.stack([da_lambda_q1[l], da_lambda_k1[l], da_lambda_q2[l], da_lambda_k2[l]]).astype(F32)

    qk, va, gqkv, gz, gate = _in_proj(x2, attn_norm_g[l].reshape(1, d).astype(F32), w_main, w_z, w_ab,
                                      gdn_conv_w[l].astype(F32), _pad_lanes(gdn_a_log[l]),
                                      _pad_lanes(gdn_dt_bias[l]), tm, s // tm)

    oa = _diff_attn(lam_p, da_subln_g[l].reshape(1, DA_V_DIM).astype(F32),
                    qk.reshape(b, s, 2 * DA_WIDTH), va.reshape(b, s, 2 * DA_WIDTH), 512)

    u, wq, kq, egl = _gdn_prep(gqkv, gate, 512)
    nchunks = s // GDN_CHUNK
    og = _gdn_scan(egl.reshape(b, nchunks, LANES), u.reshape(b, s, GDN_WIDTH),
                   wq.reshape(GDN_HEADS, b, nchunks * GDN_WQ_ROWS, GDN_HEAD_DIM),
                   kq.reshape(GDN_HEADS, b, nchunks * GDN_KQ_ROWS, GDN_CHUNK),
                   gz.reshape(b, s, GDN_WIDTH), gdn_norm_g[l].reshape(1, GDN_HEAD_DIM).astype(F32), 8)

    wo = w_out[l].astype(BF16)
    out = _ffn(x2, oa.reshape(m, DA_WIDTH), og.reshape(m, GDN_WIDTH), wo[:DA_WIDTH], wo[DA_WIDTH:],
               ffn_norm_g[l].reshape(1, d).astype(F32), w_up[l].astype(BF16), ffn_conv_w[l].astype(F32),
               ffn_conv_b[l].reshape(1, 2 * D_FF).astype(F32), w_down[l].astype(BF16),
               final_norm_g.reshape(1, d).astype(F32), tm, s // tm)
    return out.reshape(b, s, d)
```

```python
import functools
import math

import jax
import jax.numpy as jnp
from jax import lax
from jax.experimental import pallas as pl
from jax.experimental.pallas import tpu as pltpu

F32 = jnp.float32
BF16 = jnp.bfloat16

EPS = 1e-6
LOG2E = 1.4426950408889634
NEG_BIG = -1e30

D_MODEL = 1024
DA_HEADS = 4
DA_HEAD_DIM = 64
DA_V_DIM = 128
DA_WIDTH = DA_HEADS * DA_V_DIM
GDN_HEADS = 4
GDN_HEAD_DIM = 128
GDN_WIDTH = GDN_HEADS * GDN_HEAD_DIM
GDN_CONV = 4
GDN_CHUNK = 64
D_FF = 2816
FFN_CONV = 3
LAM_INIT = 0.8 - 0.6 * math.exp(-0.3 * 0)

LANES = 128
SUBLANES = 8
VMEM_LIMIT = 56 * 1024 * 1024

W_MAIN_COLS = 3 * DA_WIDTH + 3 * GDN_WIDTH


def _mm(a, b):
    return jnp.dot(a.astype(BF16), b.astype(BF16), preferred_element_type=F32)


def _mm_nt(a, b):
    return lax.dot_general(a.astype(BF16), b.astype(BF16), (((1,), (1,)), ((), ())),
                           preferred_element_type=F32)


def _sum3(ones_bf16, x, left=True):
    hi = x.astype(BF16)
    r1 = x - hi.astype(F32)
    mid = r1.astype(BF16)
    lo = (r1 - mid.astype(F32)).astype(BF16)
    if left:
        dot = lambda t: jnp.dot(ones_bf16, t, preferred_element_type=F32)
    else:
        dot = lambda t: jnp.dot(t, ones_bf16, preferred_element_type=F32)
    return dot(hi) + dot(mid) + dot(lo)


def _silu(x):
    hx = 0.5 * x
    return hx + hx * jnp.tanh(hx)


def _rms(x, g):
    return x * lax.rsqrt(jnp.mean(x * x, axis=-1, keepdims=True) + EPS) * g


PERM_PITCH = 72


def _permute_rows(scr_ref, a, slab0=0, nseg=SUBLANES):
    tm, ncols = a.shape
    seg = tm // nseg
    slabs = []
    for k in range(ncols // LANES):
        for s in range(nseg):
            scr_ref[slab0 + k, s * PERM_PITCH:s * PERM_PITCH + seg, :] = (
                a[s * seg:(s + 1) * seg, k * LANES:(k + 1) * LANES])
        slabs.append(jnp.concatenate(
            [scr_ref[slab0 + k, pl.ds(j, nseg, stride=PERM_PITCH), :] for j in range(seg)], axis=0))
    return jnp.concatenate(slabs, axis=1)


def _unpermute_rows(scr_ref, ap, out_ref, row0=0, col0=0, slab0=0, nseg=SUBLANES):
    tm, ncols = ap.shape
    seg = tm // nseg
    for k in range(ncols // LANES):
        for j in range(seg):
            scr_ref[slab0 + k, pl.ds(j, nseg, stride=PERM_PITCH), :] = (
                ap[j * nseg:(j + 1) * nseg, k * LANES:(k + 1) * LANES])
        for s in range(nseg):
            out_ref[row0 + s * seg:row0 + (s + 1) * seg, col0 + k * LANES:col0 + (k + 1) * LANES] = (
                scr_ref[slab0 + k, s * PERM_PITCH:s * PERM_PITCH + seg, :])


def _in_proj_kernel(x_ref, g_ref, w_ref, wz_ref, wab_ref, cw_ref, alog_ref, dtb_ref,
                    qk_ref, va_ref, gqkv_ref, gz_ref, gate_ref, cbuf_ref, carry_ref, perm_ref,
                    *, tm, tiles_per_seq):
    i = pl.program_id(0)
    tile = SUBLANES
    halo = (GDN_CONV - 1) * tile
    first = (i % tiles_per_seq) == 0
    h = _rms(x_ref[...], g_ref[...])
    hb = h.astype(BF16)
    hb_perm = _permute_rows(perm_ref, h).astype(BF16)

    def proj(lo, width, ref=w_ref, lhs=hb):
        return jnp.dot(lhs, ref[:, lo:lo + width], preferred_element_type=F32)

    lane = lax.broadcasted_iota(jnp.int32, (tm, LANES), 1)
    base = 3 * DA_WIDTH
    half = DA_WIDTH // 2
    row8 = lax.broadcasted_iota(jnp.int32, (tile, half), 0)

    @pl.when(i == 0)
    def _():
        carry_ref[...] = jnp.zeros_like(carry_ref)

    def gdn_part(c):
        cols = slice(c * half, (c + 1) * half)
        u = proj(base + c * half, half, lhs=hb_perm)
        prev = carry_ref[:, cols]
        prev = jnp.where(first, jnp.zeros_like(prev), prev)
        for d in range(1, GDN_CONV):
            cur_t = u[tm - d * tile:tm - (d - 1) * tile, :]
            prev_t = prev[halo - d * tile:halo - (d - 1) * tile, :]
            cbuf_ref[halo - d * tile:halo - (d - 1) * tile, cols] = jnp.where(
                row8 == 0, pltpu.roll(prev_t, 1, 0), pltpu.roll(cur_t, 1, 0))
        cbuf_ref[halo:halo + tm, cols] = u
        carry_ref[:, cols] = u[tm - halo:tm, :]

    def q_part(c):
        qk_ref[:, c * half:(c + 1) * half] = (proj(c * half, half) * (DA_HEAD_DIM ** -0.5 * LOG2E)).astype(BF16)

    def k_part(c):
        qk_ref[:, DA_WIDTH + c * half:DA_WIDTH + (c + 1) * half] = proj(DA_WIDTH + c * half, half).astype(BF16)

    def v_part(c):
        v = proj(2 * DA_WIDTH + c * half, half)
        ones_col = jnp.ones((tm, LANES), BF16)
        for hh in range(2):
            h = 2 * c + hh
            va_ref[:, 2 * h * LANES:(2 * h + 1) * LANES] = v[:, hh * DA_V_DIM:(hh + 1) * DA_V_DIM].astype(BF16)
            va_ref[:, (2 * h + 1) * LANES:(2 * h + 2) * LANES] = ones_col

    def z_part(c):
        gz_ref[:, c * half:(c + 1) * half] = proj(c * half, half, wz_ref)

    def gate_part(_):
        gab = proj(0, LANES, wab_ref)
        sp_in = gab + dtb_ref[...]
        softplus = jnp.maximum(sp_in, 0.0) + jnp.log(1.0 + jnp.exp(-jnp.abs(sp_in)))
        gate_ref[...] = jnp.where(lane < GDN_HEADS, -jnp.exp(alog_ref[...]) * softplus,
                                  1.0 / (1.0 + jnp.exp(-gab)))

    def conv_group(c):
        cols = slice(c * GDN_HEAD_DIM, (c + 1) * GDN_HEAD_DIM)
        cw = cw_ref[:, cols]
        xc = cw[GDN_CONV - 1:GDN_CONV] * cbuf_ref[halo:halo + tm, cols]
        for d in range(1, GDN_CONV):
            xc = xc + cw[GDN_CONV - 1 - d:GDN_CONV - d] * cbuf_ref[halo - d * tile:halo - d * tile + tm, cols]
        xc = _silu(xc)
        if c < GDN_HEADS:
            xc = xc * (lax.rsqrt(jnp.sum(xc * xc, axis=-1, keepdims=True) + EPS) * GDN_HEAD_DIM ** -0.5)
        elif c < 2 * GDN_HEADS:
            xc = xc * lax.rsqrt(jnp.sum(xc * xc, axis=-1, keepdims=True) + EPS)
        _unpermute_rows(perm_ref, xc, gqkv_ref, col0=c * GDN_HEAD_DIM, slab0=c % (D_MODEL // LANES))

    matmul_tasks = ([(gdn_part, c) for c in range(6)] + [(q_part, 0), (q_part, 1), (k_part, 0), (k_part, 1),
                    (v_part, 0), (v_part, 1), (z_part, 0), (z_part, 1), (gate_part, 0)])
    ngroups = 3 * GDN_HEADS
    issued = 0
    for c in range(ngroups):
        want = min(len(matmul_tasks), 4 + 2 * (c // 2))
        while issued < want:
            fn, arg = matmul_tasks[issued]
            fn(arg)
            issued += 1
        conv_group(c)
    for fn, arg in matmul_tasks[issued:]:
        fn(arg)


def _in_proj(x2, g, w_main, w_z, w_ab, conv_w, alog_row, dtb_row, tm, tiles_per_seq):
    m = x2.shape[0]
    row = lambda i: (i, 0)
    fixed = lambda i: (0, 0)
    return pl.pallas_call(
        functools.partial(_in_proj_kernel, tm=tm, tiles_per_seq=tiles_per_seq),
        grid=(m // tm,),
        in_specs=[pl.BlockSpec((tm, D_MODEL), row),
                  pl.BlockSpec((1, D_MODEL), fixed),
                  pl.BlockSpec((D_MODEL, W_MAIN_COLS), fixed),
                  pl.BlockSpec((D_MODEL, GDN_WIDTH), fixed),
                  pl.BlockSpec((D_MODEL, LANES), fixed),
                  pl.BlockSpec((GDN_CONV, 3 * GDN_WIDTH), fixed),
                  pl.BlockSpec((1, LANES), fixed),
                  pl.BlockSpec((1, LANES), fixed)],
        out_specs=[pl.BlockSpec((tm, 2 * DA_WIDTH), row),
                   pl.BlockSpec((tm, 2 * DA_WIDTH), row),
                   pl.BlockSpec((tm, 3 * GDN_WIDTH), row),
                   pl.BlockSpec((tm, GDN_WIDTH), row),
                   pl.BlockSpec((tm, LANES), row)],
        out_shape=[jax.ShapeDtypeStruct((m, 2 * DA_WIDTH), BF16),
                   jax.ShapeDtypeStruct((m, 2 * DA_WIDTH), BF16),
                   jax.ShapeDtypeStruct((m, 3 * GDN_WIDTH), F32),
                   jax.ShapeDtypeStruct((m, GDN_WIDTH), F32),
                   jax.ShapeDtypeStruct((m, LANES), F32)],
        scratch_shapes=[pltpu.VMEM((tm + (GDN_CONV - 1) * SUBLANES, 3 * GDN_WIDTH), F32),
                        pltpu.VMEM(((GDN_CONV - 1) * SUBLANES, 3 * GDN_WIDTH), F32),
                        pltpu.VMEM((D_MODEL // LANES, SUBLANES * PERM_PITCH, LANES), F32)],
        compiler_params=pltpu.CompilerParams(dimension_semantics=("arbitrary",),
                                             vmem_limit_bytes=VMEM_LIMIT),
        name="in_proj",
    )(x2, g, w_main, w_z, w_ab, conv_w, alog_row, dtb_row)


ATTN_HEADS_PER_STEP = 4


def _attn_kernel(lam_ref, sg_ref, q_ref, k_ref, v_ref, o_ref, acc_ref, m_ref, *, t, hp):
    hg = pl.program_id(1)
    qi = pl.program_id(2)

    lane = lax.broadcasted_iota(jnp.int32, (t, LANES), 1)
    col_f = lax.broadcasted_iota(jnp.int32, (1, t), 1).astype(F32)
    row_i = lax.broadcasted_iota(jnp.int32, (t, t), 0)
    col_i = lax.broadcasted_iota(jnp.int32, (t, t), 1)

    qm, slope_row = [], []
    for j in range(hp):
        q = q_ref[0, :, j * LANES:(j + 1) * LANES]
        zero = jnp.zeros_like(q)
        qm += [jnp.where(lane < DA_HEAD_DIM, q, zero), jnp.where(lane >= DA_HEAD_DIM, q, zero)]
        hf = (hg * hp + j + 1).astype(F32)
        slope_row.append(jnp.exp2(jnp.full((1, t), -8.0 / DA_HEADS, F32) * hf) * LOG2E)

    acc_ref[...] = jnp.zeros_like(acc_ref)

    nchain = 2 * hp

    m_ref[...] = jnp.full(m_ref.shape, NEG_BIG, F32)
    lane_tiles = lambda x, n: jnp.concatenate([x] * n, axis=1)

    def step(blocks):
        ms = [m_ref[idx] for idx in range(nchain)]
        work = [(ki, masked, idx) for ki, masked in blocks for idx in range(nchain)]

        def logits(ki, masked, idx):
            j = idx // 2
            k = k_ref[0, pl.ds(pl.multiple_of(ki * t, t), t), j * LANES:(j + 1) * LANES]
            pos = ((ki - qi) * t).astype(F32) + col_f
            s = (lax.dot_general(qm[idx], k, (((1,), (1,)), ((), ())), preferred_element_type=F32)
                 + pos * slope_row[j])
            return jnp.where(col_i <= row_i, s, NEG_BIG) if masked else s

        def probs(idx, s):
            m_new = jnp.maximum(ms[idx], jnp.max(s, axis=-1, keepdims=True))
            return m_new, jnp.exp2(s - lane_tiles(m_new, t // LANES)).astype(BF16)

        def accumulate(ki, idx, m_new, p):
            j = idx // 2
            v = v_ref[0, pl.ds(pl.multiple_of(ki * t, t), t), 2 * j * LANES:2 * (j + 1) * LANES]
            acc_ref[idx] = (lane_tiles(jnp.exp2(ms[idx] - m_new), 2) * acc_ref[idx]
                            + jnp.dot(p, v, preferred_element_type=F32))
            ms[idx] = m_new

        s_prev, mp_prev = None, None
        for slot in range(len(work) + 2):
            s_cur = logits(*work[slot]) if slot < len(work) else None
            mp_cur = probs(work[slot - 1][2], s_prev) if 1 <= slot <= len(work) else None
            if slot >= 2:
                accumulate(work[slot - 2][0], work[slot - 2][2], *mp_prev)
            s_prev, mp_prev = s_cur, mp_cur
        for idx in range(nchain):
            m_ref[idx] = ms[idx]

    npair = qi // 2
    odd = qi - 2 * npair

    def run(trips, blocks_of):
        def body(j, carry):
            step(blocks_of(j))
            return carry
        lax.fori_loop(0, trips, body, 0)

    run(npair, lambda j: [(2 * j, False), (2 * j + 1, False)])
    run(odd, lambda j: [(qi - 1, False), (qi, True)])
    run(1 - odd, lambda j: [(qi, True)])

    lam_p = lam_ref[...]
    lam = (jnp.exp(jnp.sum(lam_p[0:1] * lam_p[1:2], axis=-1, keepdims=True))
           - jnp.exp(jnp.sum(lam_p[2:3] * lam_p[3:4], axis=-1, keepdims=True)) + LAM_INIT)
    ones_sq = jnp.ones((DA_V_DIM, DA_V_DIM), BF16)
    for j in range(hp):
        a1 = acc_ref[2 * j]
        a2 = acc_ref[2 * j + 1]
        o = a1[:, :DA_V_DIM] / a1[:, DA_V_DIM:] - lam * (a2[:, :DA_V_DIM] / a2[:, DA_V_DIM:])
        o_ref[0, :, j * DA_V_DIM:(j + 1) * DA_V_DIM] = (
            o * lax.rsqrt(_sum3(ones_sq, o * o, left=False) * (1.0 / DA_V_DIM) + EPS)
            * sg_ref[...] * (1.0 - LAM_INIT)).astype(BF16)


def _diff_attn(lam_p, subln_g, qk3, va3, t):
    b, s, _ = qk3.shape
    hp = ATTN_HEADS_PER_STEP
    ngroups = DA_HEADS // hp
    return pl.pallas_call(
        functools.partial(_attn_kernel, t=t, hp=hp),
        grid=(b, ngroups, s // t),
        in_specs=[pl.BlockSpec((4, DA_HEAD_DIM), lambda bi, hg, qi: (0, 0)),
                  pl.BlockSpec((1, DA_V_DIM), lambda bi, hg, qi: (0, 0)),
                  pl.BlockSpec((1, t, hp * LANES), lambda bi, hg, qi: (bi, qi, hg)),
                  pl.BlockSpec((1, s, hp * LANES), lambda bi, hg, qi: (bi, 0, ngroups + hg),
                               pipeline_mode=pl.Buffered(1)),
                  pl.BlockSpec((1, s, 2 * hp * LANES), lambda bi, hg, qi: (bi, 0, hg),
                               pipeline_mode=pl.Buffered(1))],
        out_specs=pl.BlockSpec((1, t, hp * DA_V_DIM), lambda bi, hg, qi: (bi, qi, hg)),
        out_shape=jax.ShapeDtypeStruct((b, s, DA_WIDTH), BF16),
        scratch_shapes=[pltpu.VMEM((2 * hp, t, 2 * LANES), F32),
                        pltpu.VMEM((2 * hp, t, LANES), F32)],
        compiler_params=pltpu.CompilerParams(dimension_semantics=("arbitrary", "arbitrary", "arbitrary"),
                                             vmem_limit_bytes=VMEM_LIMIT),
        name="diff_attn",
    )(lam_p, subln_g, qk3, qk3, va3)


GDN_GROUP = 2 * GDN_CHUNK
GDN_WQ_ROWS = 2 * GDN_CHUNK
GDN_KQ_ROWS = GDN_HEAD_DIM + GDN_CHUNK


def _gdn_prep_kernel(gqkv_ref, gate_ref, u_ref, wq_ref, kq_ref, egl_ref, *, rb):
    c = GDN_CHUNK
    r = GDN_GROUP
    ri = lax.broadcasted_iota(jnp.int32, (r, r), 0)
    ci = lax.broadcasted_iota(jnp.int32, (r, r), 1)
    rx = ri ^ ci
    same_chunk = rx < c
    lower = ci <= ri
    strict = ci < ri
    eye = jnp.where(ci == ri, 1.0, 0.0).astype(F32)
    tri_ones = jnp.where(same_chunk, jnp.where(lower, 1.0, 0.0), 0.0).astype(BF16)
    blk_ones = jnp.where(same_chunk, 1.0, 0.0).astype(BF16)

    ngroups = rb // r
    chains = [(gidx, h) for gidx in range(ngroups) for h in range(GDN_HEADS)]
    each = lambda f: [f(j) for j in range(len(chains))]
    rows_of = lambda j: slice(chains[j][0] * r, (chains[j][0] + 1) * r)
    head_of = lambda j: chains[j][1]

    gate = [gate_ref[gidx * r:(gidx + 1) * r, :] for gidx in range(ngroups)]
    gc_all = [_sum3(tri_ones, gt) for gt in gate]
    gl_all = [_sum3(blk_ones, gt) for gt in gate]
    gc_t = [jnp.transpose(x) for x in gc_all]
    for gidx in range(ngroups):
        egl = jnp.exp(gl_all[gidx])
        for n in range(r // c):
            ch = gidx * (r // c) + n
            egl_ref[ch:ch + 1, :] = egl[n * c:n * c + 1, :]

    def hslice(j, part):
        lo = part * GDN_WIDTH + head_of(j) * GDN_HEAD_DIM
        return gqkv_ref[rows_of(j), lo:lo + GDN_HEAD_DIM]

    qh = each(lambda j: hslice(j, 0))
    kh = each(lambda j: hslice(j, 1))
    vh = each(lambda j: hslice(j, 2))
    beta = each(lambda j: gate[chains[j][0]][:, GDN_HEADS + head_of(j):GDN_HEADS + head_of(j) + 1])
    gc = each(lambda j: gc_all[chains[j][0]][:, head_of(j):head_of(j) + 1])
    gl = each(lambda j: gl_all[chains[j][0]][:, head_of(j):head_of(j) + 1])
    gc_row = each(lambda j: gc_t[chains[j][0]][head_of(j):head_of(j) + 1, :])

    decay = each(lambda j: jnp.exp(jnp.where(same_chunk, jnp.where(lower, gc[j] - gc_row[j], -jnp.inf), -jnp.inf)))
    kb = each(lambda j: kh[j] * beta[j])
    vb = each(lambda j: vh[j] * beta[j])
    kbf = each(lambda j: kh[j].astype(BF16))
    kk = each(lambda j: _mm_nt(kb[j], kbf[j]))
    qk = each(lambda j: _mm_nt(qh[j], kbf[j]))
    lmat = each(lambda j: jnp.where(strict, kk[j] * decay[j], 0.0))
    qk = each(lambda j: (qk[j] * decay[j]).astype(BF16))

    x1 = each(lambda j: jnp.where(rx < 16, -lmat[j], 0.0))
    p = each(lambda j: eye + x1[j])
    x2 = each(lambda j: _mm(x1[j], x1[j]))
    x2p = each(lambda j: _mm(x2[j], p[j]))
    x4 = each(lambda j: _mm(x2[j], x2[j]))
    p = each(lambda j: p[j] + x2p[j])
    x4p = each(lambda j: _mm(x4[j], p[j]))
    x8 = each(lambda j: _mm(x4[j], x4[j]))
    p = each(lambda j: p[j] + x4p[j])
    x8p = each(lambda j: _mm(x8[j], p[j]))
    p = each(lambda j: p[j] + x8p[j])
    b32 = each(lambda j: jnp.where(rx < 32, jnp.where(rx >= 16, lmat[j], 0.0), 0.0))
    t1 = each(lambda j: _mm(b32[j], p[j]))
    t2 = each(lambda j: _mm(p[j], t1[j]))
    p = each(lambda j: p[j] - t2[j])
    b64 = each(lambda j: jnp.where(rx >= 32, lmat[j], 0.0))
    t1 = each(lambda j: _mm(b64[j], p[j]))
    t2 = each(lambda j: _mm(p[j], t1[j]))
    tmat = each(lambda j: p[j] - t2[j])

    eg = each(lambda j: jnp.exp(gc[j]))
    uw = each(lambda j: _mm(tmat[j], jnp.concatenate([vb[j], kb[j] * eg[j]], axis=1)))
    qdec = each(lambda j: (qh[j] * eg[j]).astype(BF16))
    kdec_t = each(lambda j: jnp.transpose(kh[j] * jnp.exp(gl[j] - gc[j])).astype(BF16))
    for j, (gidx, h) in enumerate(chains):
        u_ref[rows_of(j), h * GDN_HEAD_DIM:(h + 1) * GDN_HEAD_DIM] = uw[j][:, :GDN_HEAD_DIM]
        w = uw[j][:, GDN_HEAD_DIM:].astype(BF16)
        for n in range(r // c):
            ch = gidx * (r // c) + n
            cr = slice(n * c, (n + 1) * c)
            wq_ref[h, ch * GDN_WQ_ROWS:ch * GDN_WQ_ROWS + c, :] = w[cr]
            wq_ref[h, ch * GDN_WQ_ROWS + c:(ch + 1) * GDN_WQ_ROWS, :] = qdec[j][cr]
            kq_ref[h, ch * GDN_KQ_ROWS:ch * GDN_KQ_ROWS + GDN_HEAD_DIM, :] = kdec_t[j][:, cr]
            kq_ref[h, ch * GDN_KQ_ROWS + GDN_HEAD_DIM:(ch + 1) * GDN_KQ_ROWS, :] = qk[j][cr, cr]


def _gdn_prep(gqkv, gate, rb):
    m = gqkv.shape[0]
    nch = rb // GDN_CHUNK
    row = lambda i: (i, 0)
    hrow = lambda i: (0, i, 0)
    return pl.pallas_call(
        functools.partial(_gdn_prep_kernel, rb=rb),
        grid=(m // rb,),
        in_specs=[pl.BlockSpec((rb, 3 * GDN_WIDTH), row),
                  pl.BlockSpec((rb, LANES), row)],
        out_specs=[pl.BlockSpec((rb, GDN_WIDTH), row),
                   pl.BlockSpec((GDN_HEADS, nch * GDN_WQ_ROWS, GDN_HEAD_DIM), hrow),
                   pl.BlockSpec((GDN_HEADS, nch * GDN_KQ_ROWS, GDN_CHUNK), hrow),
                   pl.BlockSpec((nch, LANES), row)],
        out_shape=[jax.ShapeDtypeStruct((m, GDN_WIDTH), F32),
                   jax.ShapeDtypeStruct((GDN_HEADS, m // GDN_CHUNK * GDN_WQ_ROWS, GDN_HEAD_DIM), BF16),
                   jax.ShapeDtypeStruct((GDN_HEADS, m // GDN_CHUNK * GDN_KQ_ROWS, GDN_CHUNK), BF16),
                   jax.ShapeDtypeStruct((m // GDN_CHUNK, LANES), F32)],
        compiler_params=pltpu.CompilerParams(dimension_semantics=("arbitrary",),
                                             vmem_limit_bytes=VMEM_LIMIT),
        name="gdn_prep",
    )(gqkv, gate)


def _gdn_scan_kernel(egl_ref, u_ref, wq_ref, kq_ref, gz_ref, ng_ref, og_ref, state_ref, *, nb, g):
    i = pl.program_id(0)
    c = GDN_CHUNK

    @pl.when(i == 0)
    def _():
        state_ref[...] = jnp.zeros_like(state_ref)

    chains = [(b, h) for b in range(nb) for h in range(GDN_HEADS)]
    each = lambda f: [f(j, *chains[j]) for j in range(len(chains))]
    hcols = lambda h: slice(h * GDN_HEAD_DIM, (h + 1) * GDN_HEAD_DIM)
    states = each(lambda j, b, h: state_ref[j])
    for n in range(g):
        rows = slice(n * c, (n + 1) * c)
        ws_qs = each(lambda j, b, h: jnp.dot(wq_ref[h, b, n * GDN_WQ_ROWS:(n + 1) * GDN_WQ_ROWS, :],
                                             states[j].astype(BF16),
                                             preferred_element_type=F32))
        v_new = each(lambda j, b, h: (u_ref[b, rows, hcols(h)] - ws_qs[j][:c]).astype(BF16))
        kv_qv = each(lambda j, b, h: jnp.dot(kq_ref[h, b, n * GDN_KQ_ROWS:(n + 1) * GDN_KQ_ROWS, :], v_new[j],
                                             preferred_element_type=F32))
        states = each(lambda j, b, h: states[j] * egl_ref[b, n:n + 1, h:h + 1] + kv_qv[j][:GDN_HEAD_DIM])
        for j, (b, h) in enumerate(chains):
            o = _rms(ws_qs[j][c:] + kv_qv[j][GDN_HEAD_DIM:], ng_ref[...])
            og_ref[b, rows, hcols(h)] = (o * _silu(gz_ref[b, rows, hcols(h)])).astype(BF16)
    for j in range(len(chains)):
        state_ref[j] = states[j]


def _gdn_scan(egl3, u3, wq4, kq4, gz3, norm_g, g):
    nb, s, _ = u3.shape
    nchunks = s // GDN_CHUNK
    blk = lambda i: (0, i, 0)
    hblk = lambda i: (0, 0, i, 0)
    return pl.pallas_call(
        functools.partial(_gdn_scan_kernel, nb=nb, g=g),
        grid=(nchunks // g,),
        in_specs=[pl.BlockSpec((nb, g, LANES), blk),
                  pl.BlockSpec((nb, g * GDN_CHUNK, GDN_WIDTH), blk),
                  pl.BlockSpec((GDN_HEADS, nb, g * GDN_WQ_ROWS, GDN_HEAD_DIM), hblk),
                  pl.BlockSpec((GDN_HEADS, nb, g * GDN_KQ_ROWS, GDN_CHUNK), hblk),
                  pl.BlockSpec((nb, g * GDN_CHUNK, GDN_WIDTH), blk),
                  pl.BlockSpec((1, GDN_HEAD_DIM), lambda i: (0, 0))],
        out_specs=pl.BlockSpec((nb, g * GDN_CHUNK, GDN_WIDTH), blk),
        out_shape=jax.ShapeDtypeStruct((nb, s, GDN_WIDTH), BF16),
        scratch_shapes=[pltpu.VMEM((nb * GDN_HEADS, GDN_HEAD_DIM, GDN_HEAD_DIM), F32)],
        compiler_params=pltpu.CompilerParams(dimension_semantics=("arbitrary",),
                                             vmem_limit_bytes=VMEM_LIMIT),
        name="gdn_scan",
    )(egl3, u3, wq4, kq4, gz3, norm_g)


FFN_FC = 256


FFN_TILE = 512


def _ffn_kernel(x_ref, oa_ref, og_ref, wa_ref, wg_ref, gffn_ref, wup_ref, cw_ref, cb_ref, wd_ref, g_ref, out_ref,
                ubuf_ref, carry_ref, acc_ref, perm_ref, *, nsub, tiles_per_seq):
    i = pl.program_id(0)
    tm = FFN_TILE
    tile = SUBLANES
    halo = (FFN_CONV - 1) * tile

    @pl.when(i == 0)
    def _():
        carry_ref[...] = jnp.zeros_like(carry_ref)

    nslab = D_FF // FFN_FC
    col_of = lambda cidx, part: part * D_FF + cidx * FFN_FC
    slot_of = lambda cidx, part: 2 * (cidx % 2) + part
    row8 = lax.broadcasted_iota(jnp.int32, (tile, FFN_FC), 0)
    rows_of = lambda sub: slice(sub * tm, (sub + 1) * tm)

    def prologue(sub):
        rows = rows_of(sub)
        x1 = _permute_rows(perm_ref, x_ref[rows, :]
                           + jnp.dot(oa_ref[rows, :], wa_ref[...], preferred_element_type=F32)
                           + jnp.dot(og_ref[rows, :], wg_ref[...], preferred_element_type=F32))
        acc_ref[rows, :] = x1
        return _rms(x1, gffn_ref[...]).astype(BF16)

    def epilogue(sub):
        _unpermute_rows(perm_ref, _rms(acc_ref[rows_of(sub), :], g_ref[...]), out_ref, row0=sub * tm)

    def up(sub, h2, cidx):
        first = ((i * nsub + sub) % tiles_per_seq) == 0
        for part in range(2):
            lo, slot = col_of(cidx, part), slot_of(cidx, part)
            u = jnp.dot(h2, wup_ref[:, lo:lo + FFN_FC], preferred_element_type=F32)
            prev = carry_ref[:, lo:lo + FFN_FC]
            prev = jnp.where(first, jnp.zeros_like(prev), prev)
            for d in range(1, FFN_CONV):
                cur_t = u[tm - d * tile:tm - (d - 1) * tile, :]
                prev_t = prev[halo - d * tile:halo - (d - 1) * tile, :]
                ubuf_ref[slot, halo - d * tile:halo - (d - 1) * tile, :] = jnp.where(
                    row8 == 0, pltpu.roll(prev_t, 1, 0), pltpu.roll(cur_t, 1, 0))
            ubuf_ref[slot, halo:halo + tm, :] = u
            carry_ref[:, lo:lo + FFN_FC] = u[tm - halo:tm, :]

    def conv(cidx, part):
        lo, slot = col_of(cidx, part), slot_of(cidx, part)
        cw = cw_ref[:, lo:lo + FFN_FC]
        y = cb_ref[:, lo:lo + FFN_FC] + cw[FFN_CONV - 1:FFN_CONV] * ubuf_ref[slot, halo:halo + tm, :]
        for d in range(1, FFN_CONV):
            y = y + cw[FFN_CONV - 1 - d:FFN_CONV - d] * ubuf_ref[slot, halo - d * tile:halo - d * tile + tm, :]
        return y

    def down(sub, cidx, act):
        acc_ref[rows_of(sub), :] += jnp.dot(act, wd_ref[cidx * FFN_FC:(cidx + 1) * FFN_FC, :],
                                            preferred_element_type=F32)

    mid = nslab // 2
    h2 = prologue(0)
    for sub in range(nsub):
        h2_next, act_prev = None, None
        for slot in range(nslab + 2):
            if slot < nslab:
                up(sub, h2, slot)
            act_cur = (_silu(conv(slot - 1, 0)) * conv(slot - 1, 1)).astype(BF16) if 1 <= slot <= nslab else None
            if slot >= 2:
                down(sub, slot - 2, act_prev)
            act_prev = act_cur
            if slot == mid:
                if sub > 0:
                    epilogue(sub - 1)
                if sub + 1 < nsub:
                    h2_next = prologue(sub + 1)
        h2 = h2_next
    epilogue(nsub - 1)


def _ffn(x2, oa2, og2, wo_a, wo_g, g_ffn, w_up, conv_w, conv_b, w_down, g, nsub, seq_len):
    m = x2.shape[0]
    tm = nsub * FFN_TILE
    row = lambda i: (i, 0)
    fixed = lambda i: (0, 0)
    once = pl.Buffered(1)
    return pl.pallas_call(
        functools.partial(_ffn_kernel, nsub=nsub, tiles_per_seq=seq_len // FFN_TILE),
        grid=(m // tm,),
        in_specs=[pl.BlockSpec((tm, D_MODEL), row),
                  pl.BlockSpec((tm, DA_WIDTH), row),
                  pl.BlockSpec((tm, GDN_WIDTH), row),
                  pl.BlockSpec((DA_WIDTH, D_MODEL), fixed, pipeline_mode=once),
                  pl.BlockSpec((GDN_WIDTH, D_MODEL), fixed, pipeline_mode=once),
                  pl.BlockSpec((1, D_MODEL), fixed),
                  pl.BlockSpec((D_MODEL, 2 * D_FF), fixed, pipeline_mode=once),
                  pl.BlockSpec((FFN_CONV, 2 * D_FF), fixed),
                  pl.BlockSpec((1, 2 * D_FF), fixed),
                  pl.BlockSpec((D_FF, D_MODEL), fixed, pipeline_mode=once),
                  pl.BlockSpec((1, D_MODEL), fixed)],
        out_specs=pl.BlockSpec((tm, D_MODEL), row),
        out_shape=jax.ShapeDtypeStruct((m, D_MODEL), F32),
        scratch_shapes=[pltpu.VMEM((4, FFN_TILE + (FFN_CONV - 1) * SUBLANES, FFN_FC), F32),
                        pltpu.VMEM(((FFN_CONV - 1) * SUBLANES, 2 * D_FF), F32),
                        pltpu.VMEM((tm, D_MODEL), F32),
                        pltpu.VMEM((D_MODEL // LANES, SUBLANES * PERM_PITCH, LANES), F32)],
        compiler_params=pltpu.CompilerParams(dimension_semantics=("arbitrary",),
                                             vmem_limit_bytes=VMEM_LIMIT),
        name="ffn",
    )(x2, oa2, og2, wo_a, wo_g, g_ffn, w_up, conv_w, conv_b, w_down, g)


def _pad_lanes(v):
    return jnp.zeros((1, LANES), F32).at[0, :v.shape[0]].set(v.astype(F32))


def kernel(x, attn_norm_g, w_in, da_lambda_q1, da_lambda_k1, da_lambda_q2, da_lambda_k2, da_subln_g,
           gdn_conv_w, gdn_a_log, gdn_dt_bias, gdn_norm_g, w_out, ffn_norm_g, w_up, ffn_conv_w,
           ffn_conv_b, w_down, final_norm_g):
    b, s, d = x.shape
    m = b * s
    tm = 512
    assert d == D_MODEL and s % tm == 0 and w_in.shape[0] == 1, (x.shape, w_in.shape)
    l = 0
    x2 = x.reshape(m, d)

    wi = w_in[l]
    w_main = wi[:, :W_MAIN_COLS].astype(BF16)
    w_z = wi[:, W_MAIN_COLS + 2 * GDN_HEADS:].astype(BF16)
    w_ab = jnp.pad(wi[:, W_MAIN_COLS:W_MAIN_COLS + 2 * GDN_HEADS], ((0, 0), (0, LANES - 2 * GDN_HEADS))).astype(BF16)
    lam_p = jnp.stack([da_lambda_q1[l], da_lambda_k1[l], da_lambda_q2[l], da_lambda_k2[l]]).astype(F32)

    qk, va, gqkv, gz, gate = _in_proj(x2, attn_norm_g[l].reshape(1, d).astype(F32), w_main, w_z, w_ab,
                                      gdn_conv_w[l].astype(F32), _pad_lanes(gdn_a_log[l]),
                                      _pad_lanes(gdn_dt_bias[l]), tm, s // tm)

    oa = _diff_attn(lam_p, da_subln_g[l].reshape(1, DA_V_DIM).astype(F32),
                    qk.reshape(b, s, 2 * DA_WIDTH), va.reshape(b, s, 2 * DA_WIDTH), 512)

    u, wq, kq, egl = _gdn_prep(gqkv, gate, 512)
    nchunks = s // GDN_CHUNK
    og = _gdn_scan(egl.reshape(b, nchunks, LANES), u.reshape(b, s, GDN_WIDTH),
                   wq.reshape(GDN_HEADS, b, nchunks * GDN_WQ_ROWS, GDN_HEAD_DIM),
                   kq.reshape(GDN_HEADS, b, nchunks * GDN_KQ_ROWS, GDN_CHUNK),
                   gz.reshape(b, s, GDN_WIDTH), gdn_norm_g[l].reshape(1, GDN_HEAD_DIM).astype(F32), 8)

    wo = w_out[l].astype(BF16)
    out = _ffn(x2, oa.reshape(m, DA_WIDTH), og.reshape(m, GDN_WIDTH), wo[:DA_WIDTH], wo[DA_WIDTH:],
               ffn_norm_g[l].reshape(1, d).astype(F32), w_up[l].astype(BF16), ffn_conv_w[l].astype(F32),
               ffn_conv_b[l].reshape(1, 2 * D_FF).astype(F32), w_down[l].astype(BF16),
               final_norm_g.reshape(1, d).astype(F32), 2, s)
    return out.reshape(b, s, d)
```

```python
import functools
import math

import jax
import jax.numpy as jnp
from jax import lax
from jax.experimental import pallas as pl
from jax.experimental.pallas import tpu as pltpu

F32 = jnp.float32
BF16 = jnp.bfloat16

EPS = 1e-6
LOG2E = 1.4426950408889634
NEG_BIG = -1e30

D_MODEL = 1024
DA_HEADS = 4
DA_HEAD_DIM = 64
DA_V_DIM = 128
DA_WIDTH = DA_HEADS * DA_V_DIM
GDN_HEADS = 4
GDN_HEAD_DIM = 128
GDN_WIDTH = GDN_HEADS * GDN_HEAD_DIM
GDN_CONV = 4
GDN_CHUNK = 64
D_FF = 2816
FFN_CONV = 3
LAM_INIT = 0.8 - 0.6 * math.exp(-0.3 * 0)

LANES = 128
SUBLANES = 8
VMEM_LIMIT = 56 * 1024 * 1024

W_MAIN_COLS = 3 * DA_WIDTH + 3 * GDN_WIDTH


def _mm(a, b):
    return jnp.dot(a.astype(BF16), b.astype(BF16), preferred_element_type=F32)


def _mm_nt(a, b):
    return lax.dot_general(a.astype(BF16), b.astype(BF16), (((1,), (1,)), ((), ())),
                           preferred_element_type=F32)


def _sum3(ones_bf16, x, left=True):
    hi = x.astype(BF16)
    r1 = x - hi.astype(F32)
    mid = r1.astype(BF16)
    lo = (r1 - mid.astype(F32)).astype(BF16)
    if left:
        dot = lambda t: jnp.dot(ones_bf16, t, preferred_element_type=F32)
    else:
        dot = lambda t: jnp.dot(t, ones_bf16, preferred_element_type=F32)
    return dot(hi) + dot(mid) + dot(lo)


def _silu(x):
    hx = 0.5 * x
    return hx + hx * jnp.tanh(hx)


def _rms(x, g):
    return x * lax.rsqrt(jnp.mean(x * x, axis=-1, keepdims=True) + EPS) * g


PERM_PITCH = 72


def _permute_rows(scr_ref, a, slab0=0, nseg=SUBLANES):
    tm, ncols = a.shape
    seg = tm // nseg
    slabs = []
    for k in range(ncols // LANES):
        for s in range(nseg):
            scr_ref[slab0 + k, s * PERM_PITCH:s * PERM_PITCH + seg, :] = (
                a[s * seg:(s + 1) * seg, k * LANES:(k + 1) * LANES])
        slabs.append(jnp.concatenate(
            [scr_ref[slab0 + k, pl.ds(j, nseg, stride=PERM_PITCH), :] for j in range(seg)], axis=0))
    return jnp.concatenate(slabs, axis=1)


def _unpermute_rows(scr_ref, ap, out_ref, row0=0, col0=0, slab0=0, nseg=SUBLANES):
    tm, ncols = ap.shape
    seg = tm // nseg
    for k in range(ncols // LANES):
        for j in range(seg):
            scr_ref[slab0 + k, pl.ds(j, nseg, stride=PERM_PITCH), :] = (
                ap[j * nseg:(j + 1) * nseg, k * LANES:(k + 1) * LANES])
        for s in range(nseg):
            out_ref[row0 + s * seg:row0 + (s + 1) * seg, col0 + k * LANES:col0 + (k + 1) * LANES] = (
                scr_ref[slab0 + k, s * PERM_PITCH:s * PERM_PITCH + seg, :])


def _in_proj_kernel(x_ref, g_ref, w_ref, wz_ref, wab_ref, cw_ref, alog_ref, dtb_ref,
                    qk_ref, va_ref, gqkv_ref, gz_ref, gate_ref, cbuf_ref, carry_ref, perm_ref,
                    *, tm, tiles_per_seq):
    i = pl.program_id(0)
    tile = SUBLANES
    halo = (GDN_CONV - 1) * tile
    first = (i % tiles_per_seq) == 0
    h = _rms(x_ref[...], g_ref[...])
    hb = h.astype(BF16)
    hb_perm = _permute_rows(perm_ref, h).astype(BF16)

    def proj(lo, width, ref=w_ref, lhs=hb):
        return jnp.dot(lhs, ref[:, lo:lo + width], preferred_element_type=F32)

    lane = lax.broadcasted_iota(jnp.int32, (tm, LANES), 1)
    base = 3 * DA_WIDTH
    half = DA_WIDTH // 2
    row8 = lax.broadcasted_iota(jnp.int32, (tile, half), 0)

    @pl.when(i == 0)
    def _():
        carry_ref[...] = jnp.zeros_like(carry_ref)

    def gdn_part(c):
        cols = slice(c * half, (c + 1) * half)
        u = proj(base + c * half, half, lhs=hb_perm)
        prev = carry_ref[:, cols]
        prev = jnp.where(first, jnp.zeros_like(prev), prev)
        for d in range(1, GDN_CONV):
            cur_t = u[tm - d * tile:tm - (d - 1) * tile, :]
            prev_t = prev[halo - d * tile:halo - (d - 1) * tile, :]
            cbuf_ref[halo - d * tile:halo - (d - 1) * tile, cols] = jnp.where(
                row8 == 0, pltpu.roll(prev_t, 1, 0), pltpu.roll(cur_t, 1, 0))
        cbuf_ref[halo:halo + tm, cols] = u
        carry_ref[:, cols] = u[tm - halo:tm, :]

    def q_part(c):
        qk_ref[:, c * half:(c + 1) * half] = (proj(c * half, half) * (DA_HEAD_DIM ** -0.5 * LOG2E)).astype(BF16)

    def k_part(c):
        qk_ref[:, DA_WIDTH + c * half:DA_WIDTH + (c + 1) * half] = proj(DA_WIDTH + c * half, half).astype(BF16)

    def v_part(c):
        v = proj(2 * DA_WIDTH + c * half, half)
        ones_col = jnp.ones((tm, LANES), BF16)
        for hh in range(2):
            h = 2 * c + hh
            va_ref[:, 2 * h * LANES:(2 * h + 1) * LANES] = v[:, hh * DA_V_DIM:(hh + 1) * DA_V_DIM].astype(BF16)
            va_ref[:, (2 * h + 1) * LANES:(2 * h + 2) * LANES] = ones_col

    def z_part(c):
        gz_ref[:, c * half:(c + 1) * half] = proj(c * half, half, wz_ref)

    def gate_part(_):
        gab = proj(0, LANES, wab_ref)
        sp_in = gab + dtb_ref[...]
        softplus = jnp.maximum(sp_in, 0.0) + jnp.log(1.0 + jnp.exp(-jnp.abs(sp_in)))
        gate_ref[...] = jnp.where(lane < GDN_HEADS, -jnp.exp(alog_ref[...]) * softplus,
                                  1.0 / (1.0 + jnp.exp(-gab)))

    def conv_group(c):
        cols = slice(c * GDN_HEAD_DIM, (c + 1) * GDN_HEAD_DIM)
        cw = cw_ref[:, cols]
        xc = cw[GDN_CONV - 1:GDN_CONV] * cbuf_ref[halo:halo + tm, cols]
        for d in range(1, GDN_CONV):
            xc = xc + cw[GDN_CONV - 1 - d:GDN_CONV - d] * cbuf_ref[halo - d * tile:halo - d * tile + tm, cols]
        xc = _silu(xc)
        if c < GDN_HEADS:
            xc = xc * (lax.rsqrt(jnp.sum(xc * xc, axis=-1, keepdims=True) + EPS) * GDN_HEAD_DIM ** -0.5)
        elif c < 2 * GDN_HEADS:
            xc = xc * lax.rsqrt(jnp.sum(xc * xc, axis=-1, keepdims=True) + EPS)
        _unpermute_rows(perm_ref, xc, gqkv_ref, col0=c * GDN_HEAD_DIM, slab0=c % (D_MODEL // LANES))

    matmul_tasks = ([(gdn_part, c) for c in range(6)] + [(q_part, 0), (q_part, 1), (k_part, 0), (k_part, 1),
                    (v_part, 0), (v_part, 1), (z_part, 0), (z_part, 1), (gate_part, 0)])
    ngroups = 3 * GDN_HEADS
    issued = 0
    for c in range(ngroups):
        want = min(len(matmul_tasks), 4 + 2 * (c // 2))
        while issued < want:
            fn, arg = matmul_tasks[issued]
            fn(arg)
            issued += 1
        conv_group(c)
    for fn, arg in matmul_tasks[issued:]:
        fn(arg)


def _in_proj(x2, g, w_main, w_z, w_ab, conv_w, alog_row, dtb_row, tm, tiles_per_seq):
    m = x2.shape[0]
    row = lambda i: (i, 0)
    fixed = lambda i: (0, 0)
    return pl.pallas_call(
        functools.partial(_in_proj_kernel, tm=tm, tiles_per_seq=tiles_per_seq),
        grid=(m // tm,),
        in_specs=[pl.BlockSpec((tm, D_MODEL), row),
                  pl.BlockSpec((1, D_MODEL), fixed),
                  pl.BlockSpec((D_MODEL, W_MAIN_COLS), fixed),
                  pl.BlockSpec((D_MODEL, GDN_WIDTH), fixed),
                  pl.BlockSpec((D_MODEL, LANES), fixed),
                  pl.BlockSpec((GDN_CONV, 3 * GDN_WIDTH), fixed),
                  pl.BlockSpec((1, LANES), fixed),
                  pl.BlockSpec((1, LANES), fixed)],
        out_specs=[pl.BlockSpec((tm, 2 * DA_WIDTH), row),
                   pl.BlockSpec((tm, 2 * DA_WIDTH), row),
                   pl.BlockSpec((tm, 3 * GDN_WIDTH), row),
                   pl.BlockSpec((tm, GDN_WIDTH), row),
                   pl.BlockSpec((tm, LANES), row)],
        out_shape=[jax.ShapeDtypeStruct((m, 2 * DA_WIDTH), BF16),
                   jax.ShapeDtypeStruct((m, 2 * DA_WIDTH), BF16),
                   jax.ShapeDtypeStruct((m, 3 * GDN_WIDTH), F32),
                   jax.ShapeDtypeStruct((m, GDN_WIDTH), F32),
                   jax.ShapeDtypeStruct((m, LANES), F32)],
        scratch_shapes=[pltpu.VMEM((tm + (GDN_CONV - 1) * SUBLANES, 3 * GDN_WIDTH), F32),
                        pltpu.VMEM(((GDN_CONV - 1) * SUBLANES, 3 * GDN_WIDTH), F32),
                        pltpu.VMEM((D_MODEL // LANES, SUBLANES * PERM_PITCH, LANES), F32)],
        compiler_params=pltpu.CompilerParams(dimension_semantics=("arbitrary",),
                                             vmem_limit_bytes=VMEM_LIMIT),
        name="in_proj",
    )(x2, g, w_main, w_z, w_ab, conv_w, alog_row, dtb_row)


ATTN_HEADS_PER_STEP = 4


def _attn_kernel(lam_ref, sg_ref, q_ref, k_ref, v_ref, o_ref, acc_ref, m_ref, *, t, hp):
    hg = pl.program_id(1)
    qi = pl.program_id(2)

    lane = lax.broadcasted_iota(jnp.int32, (t, LANES), 1)
    col_f = lax.broadcasted_iota(jnp.int32, (1, t), 1).astype(F32)
    row_i = lax.broadcasted_iota(jnp.int32, (t, t), 0)
    col_i = lax.broadcasted_iota(jnp.int32, (t, t), 1)

    qm, slope_row = [], []
    for j in range(hp):
        q = q_ref[0, :, j * LANES:(j + 1) * LANES]
        zero = jnp.zeros_like(q)
        qm += [jnp.where(lane < DA_HEAD_DIM, q, zero), jnp.where(lane >= DA_HEAD_DIM, q, zero)]
        hf = (hg * hp + j + 1).astype(F32)
        slope_row.append(jnp.exp2(jnp.full((1, t), -8.0 / DA_HEADS, F32) * hf) * LOG2E)

    acc_ref[...] = jnp.zeros_like(acc_ref)

    nchain = 2 * hp

    m_ref[...] = jnp.full(m_ref.shape, NEG_BIG, F32)
    lane_tiles = lambda x, n: jnp.concatenate([x] * n, axis=1)

    def step(blocks):
        ms = [m_ref[idx] for idx in range(nchain)]
        work = [(ki, masked, idx) for ki, masked in blocks for idx in range(nchain)]

        def logits(ki, masked, idx):
            j = idx // 2
            k = k_ref[0, pl.ds(pl.multiple_of(ki * t, t), t), j * LANES:(j + 1) * LANES]
            pos = ((ki - qi) * t).astype(F32) + col_f
            s = (lax.dot_general(qm[idx], k, (((1,), (1,)), ((), ())), preferred_element_type=F32)
                 + pos * slope_row[j])
            return jnp.where(col_i <= row_i, s, NEG_BIG) if masked else s

        def probs(idx, s):
            m_new = jnp.maximum(ms[idx], jnp.max(s, axis=-1, keepdims=True))
            return m_new, jnp.exp2(s - lane_tiles(m_new, t // LANES)).astype(BF16)

        def accumulate(ki, idx, m_new, p):
            j = idx // 2
            v = v_ref[0, pl.ds(pl.multiple_of(ki * t, t), t), 2 * j * LANES:2 * (j + 1) * LANES]
            acc_ref[idx] = (lane_tiles(jnp.exp2(ms[idx] - m_new), 2) * acc_ref[idx]
                            + jnp.dot(p, v, preferred_element_type=F32))
            ms[idx] = m_new

        s_prev, mp_prev = None, None
        for slot in range(len(work) + 2):
            s_cur = logits(*work[slot]) if slot < len(work) else None
            mp_cur = probs(work[slot - 1][2], s_prev) if 1 <= slot <= len(work) else None
            if slot >= 2:
                accumulate(work[slot - 2][0], work[slot - 2][2], *mp_prev)
            s_prev, mp_prev = s_cur, mp_cur
        for idx in range(nchain):
            m_ref[idx] = ms[idx]

    npair = qi // 2
    odd = qi - 2 * npair

    def run(trips, blocks_of):
        def body(j, carry):
            step(blocks_of(j))
            return carry
        lax.fori_loop(0, trips, body, 0)

    run(npair, lambda j: [(2 * j, False), (2 * j + 1, False)])
    run(odd, lambda j: [(qi - 1, False), (qi, True)])
    run(1 - odd, lambda j: [(qi, True)])

    lam_p = lam_ref[...]
    lam = (jnp.exp(jnp.sum(lam_p[0:1] * lam_p[1:2], axis=-1, keepdims=True))
           - jnp.exp(jnp.sum(lam_p[2:3] * lam_p[3:4], axis=-1, keepdims=True)) + LAM_INIT)
    ones_sq = jnp.ones((DA_V_DIM, DA_V_DIM), BF16)
    for j in range(hp):
        a1 = acc_ref[2 * j]
        a2 = acc_ref[2 * j + 1]
        o = a1[:, :DA_V_DIM] / a1[:, DA_V_DIM:] - lam * (a2[:, :DA_V_DIM] / a2[:, DA_V_DIM:])
        o_ref[0, :, j * DA_V_DIM:(j + 1) * DA_V_DIM] = (
            o * lax.rsqrt(_sum3(ones_sq, o * o, left=False) * (1.0 / DA_V_DIM) + EPS)
            * sg_ref[...] * (1.0 - LAM_INIT)).astype(BF16)


def _diff_attn(lam_p, subln_g, qk3, va3, t):
    b, s, _ = qk3.shape
    hp = ATTN_HEADS_PER_STEP
    ngroups = DA_HEADS // hp
    return pl.pallas_call(
        functools.partial(_attn_kernel, t=t, hp=hp),
        grid=(b, ngroups, s // t),
        in_specs=[pl.BlockSpec((4, DA_HEAD_DIM), lambda bi, hg, qi: (0, 0)),
                  pl.BlockSpec((1, DA_V_DIM), lambda bi, hg, qi: (0, 0)),
                  pl.BlockSpec((1, t, hp * LANES), lambda bi, hg, qi: (bi, qi, hg)),
                  pl.BlockSpec((1, s, hp * LANES), lambda bi, hg, qi: (bi, 0, ngroups + hg),
                               pipeline_mode=pl.Buffered(1)),
                  pl.BlockSpec((1, s, 2 * hp * LANES), lambda bi, hg, qi: (bi, 0, hg),
                               pipeline_mode=pl.Buffered(1))],
        out_specs=pl.BlockSpec((1, t, hp * DA_V_DIM), lambda bi, hg, qi: (bi, qi, hg)),
        out_shape=jax.ShapeDtypeStruct((b, s, DA_WIDTH), BF16),
        scratch_shapes=[pltpu.VMEM((2 * hp, t, 2 * LANES), F32),
                        pltpu.VMEM((2 * hp, t, LANES), F32)],
        compiler_params=pltpu.CompilerParams(dimension_semantics=("arbitrary", "arbitrary", "arbitrary"),
                                             vmem_limit_bytes=VMEM_LIMIT),
        name="diff_attn",
    )(lam_p, subln_g, qk3, qk3, va3)


GDN_GROUP = 2 * GDN_CHUNK
GDN_WQ_ROWS = 2 * GDN_CHUNK
GDN_KQ_ROWS = GDN_HEAD_DIM + GDN_CHUNK


def _gdn_prep_kernel(gqkv_ref, gate_ref, u_ref, wq_ref, kq_ref, egl_ref, *, rb):
    c = GDN_CHUNK
    r = GDN_GROUP
    ri = lax.broadcasted_iota(jnp.int32, (r, r), 0)
    ci = lax.broadcasted_iota(jnp.int32, (r, r), 1)
    rx = ri ^ ci
    same_chunk = rx < c
    lower = ci <= ri
    strict = ci < ri
    eye = jnp.where(ci == ri, 1.0, 0.0).astype(F32)
    tri_ones = jnp.where(same_chunk, jnp.where(lower, 1.0, 0.0), 0.0).astype(BF16)
    blk_ones = jnp.where(same_chunk, 1.0, 0.0).astype(BF16)

    ngroups = rb // r
    chains = [(gidx, h) for gidx in range(ngroups) for h in range(GDN_HEADS)]
    each = lambda f: [f(j) for j in range(len(chains))]
    rows_of = lambda j: slice(chains[j][0] * r, (chains[j][0] + 1) * r)
    head_of = lambda j: chains[j][1]

    gate = [gate_ref[gidx * r:(gidx + 1) * r, :] for gidx in range(ngroups)]
    gc_all = [_sum3(tri_ones, gt) for gt in gate]
    gl_all = [_sum3(blk_ones, gt) for gt in gate]
    gc_t = [jnp.transpose(x) for x in gc_all]
    for gidx in range(ngroups):
        egl = jnp.exp(gl_all[gidx])
        for n in range(r // c):
            ch = gidx * (r // c) + n
            egl_ref[ch:ch + 1, :] = egl[n * c:n * c + 1, :]

    def hslice(j, part):
        lo = part * GDN_WIDTH + head_of(j) * GDN_HEAD_DIM
        return gqkv_ref[rows_of(j), lo:lo + GDN_HEAD_DIM]

    qh = each(lambda j: hslice(j, 0))
    kh = each(lambda j: hslice(j, 1))
    vh = each(lambda j: hslice(j, 2))
    wide = lambda col: jnp.broadcast_to(col, (r, GDN_HEAD_DIM))
    beta = each(lambda j: wide(gate[chains[j][0]][:, GDN_HEADS + head_of(j):GDN_HEADS + head_of(j) + 1]))
    gc = each(lambda j: wide(gc_all[chains[j][0]][:, head_of(j):head_of(j) + 1]))
    gl = each(lambda j: gl_all[chains[j][0]][:, head_of(j):head_of(j) + 1])
    gc_row = each(lambda j: gc_t[chains[j][0]][head_of(j):head_of(j) + 1, :])

    decay = each(lambda j: jnp.exp(jnp.where(same_chunk, jnp.where(lower, gc[j] - gc_row[j], -jnp.inf), -jnp.inf)))
    kb = each(lambda j: kh[j] * beta[j])
    vb = each(lambda j: vh[j] * beta[j])
    kbf = each(lambda j: kh[j].astype(BF16))
    kk = each(lambda j: _mm_nt(kb[j], kbf[j]))
    qk = each(lambda j: _mm_nt(qh[j], kbf[j]))
    lmat = each(lambda j: jnp.where(strict, kk[j] * decay[j], 0.0))
    qk = each(lambda j: (qk[j] * decay[j]).astype(BF16))

    x1 = each(lambda j: jnp.where(rx < 16, -lmat[j], 0.0))
    p = each(lambda j: eye + x1[j])
    x2 = each(lambda j: _mm(x1[j], x1[j]))
    x2p = each(lambda j: _mm(x2[j], p[j]))
    x4 = each(lambda j: _mm(x2[j], x2[j]))
    p = each(lambda j: p[j] + x2p[j])
    x4p = each(lambda j: _mm(x4[j], p[j]))
    x8 = each(lambda j: _mm(x4[j], x4[j]))
    p = each(lambda j: p[j] + x4p[j])
    x8p = each(lambda j: _mm(x8[j], p[j]))
    p = each(lambda j: p[j] + x8p[j])
    b32 = each(lambda j: jnp.where(rx < 32, jnp.where(rx >= 16, lmat[j], 0.0), 0.0))
    t1 = each(lambda j: _mm(b32[j], p[j]))
    t2 = each(lambda j: _mm(p[j], t1[j]))
    p = each(lambda j: p[j] - t2[j])
    b64 = each(lambda j: jnp.where(rx >= 32, lmat[j], 0.0))
    t1 = each(lambda j: _mm(b64[j], p[j]))
    t2 = each(lambda j: _mm(p[j], t1[j]))
    tmat = each(lambda j: p[j] - t2[j])

    eg = each(lambda j: jnp.exp(gc[j]))
    uw = each(lambda j: _mm(tmat[j], jnp.concatenate([vb[j], kb[j] * eg[j]], axis=1)))
    qdec = each(lambda j: (qh[j] * eg[j]).astype(BF16))
    kdec_t = each(lambda j: jnp.transpose(kh[j] * jnp.exp(wide(gl[j]) - gc[j])).astype(BF16))
    for j, (gidx, h) in enumerate(chains):
        u_ref[rows_of(j), h * GDN_HEAD_DIM:(h + 1) * GDN_HEAD_DIM] = uw[j][:, :GDN_HEAD_DIM]
        w = uw[j][:, GDN_HEAD_DIM:].astype(BF16)
        for n in range(r // c):
            ch = gidx * (r // c) + n
            cr = slice(n * c, (n + 1) * c)
            wq_ref[h, ch * GDN_WQ_ROWS:ch * GDN_WQ_ROWS + c, :] = w[cr]
            wq_ref[h, ch * GDN_WQ_ROWS + c:(ch + 1) * GDN_WQ_ROWS, :] = qdec[j][cr]
            kq_ref[h, ch * GDN_KQ_ROWS:ch * GDN_KQ_ROWS + GDN_HEAD_DIM, :] = kdec_t[j][:, cr]
            kq_ref[h, ch * GDN_KQ_ROWS + GDN_HEAD_DIM:(ch + 1) * GDN_KQ_ROWS, :] = qk[j][cr, cr]


def _gdn_prep(gqkv, gate, rb):
    m = gqkv.shape[0]
    nch = rb // GDN_CHUNK
    row = lambda i: (i, 0)
    hrow = lambda i: (0, i, 0)
    return pl.pallas_call(
        functools.partial(_gdn_prep_kernel, rb=rb),
        grid=(m // rb,),
        in_specs=[pl.BlockSpec((rb, 3 * GDN_WIDTH), row),
                  pl.BlockSpec((rb, LANES), row)],
        out_specs=[pl.BlockSpec((rb, GDN_WIDTH), row),
                   pl.BlockSpec((GDN_HEADS, nch * GDN_WQ_ROWS, GDN_HEAD_DIM), hrow),
                   pl.BlockSpec((GDN_HEADS, nch * GDN_KQ_ROWS, GDN_CHUNK), hrow),
                   pl.BlockSpec((nch, LANES), row)],
        out_shape=[jax.ShapeDtypeStruct((m, GDN_WIDTH), F32),
                   jax.ShapeDtypeStruct((GDN_HEADS, m // GDN_CHUNK * GDN_WQ_ROWS, GDN_HEAD_DIM), BF16),
                   jax.ShapeDtypeStruct((GDN_HEADS, m // GDN_CHUNK * GDN_KQ_ROWS, GDN_CHUNK), BF16),
                   jax.ShapeDtypeStruct((m // GDN_CHUNK, LANES), F32)],
        compiler_params=pltpu.CompilerParams(dimension_semantics=("arbitrary",),
                                             vmem_limit_bytes=VMEM_LIMIT),
        name="gdn_prep",
    )(gqkv, gate)


def _gdn_scan_kernel(egl_ref, u_ref, wq_ref, kq_ref, gz_ref, ng_ref, og_ref, state_ref, *, nb, g):
    i = pl.program_id(0)
    c = GDN_CHUNK

    @pl.when(i == 0)
    def _():
        state_ref[...] = jnp.zeros_like(state_ref)

    chains = [(b, h) for b in range(nb) for h in range(GDN_HEADS)]
    each = lambda f: [f(j, *chains[j]) for j in range(len(chains))]
    hcols = lambda h: slice(h * GDN_HEAD_DIM, (h + 1) * GDN_HEAD_DIM)
    states = each(lambda j, b, h: state_ref[j])
    for n in range(g):
        rows = slice(n * c, (n + 1) * c)
        ws_qs = each(lambda j, b, h: jnp.dot(wq_ref[h, b, n * GDN_WQ_ROWS:(n + 1) * GDN_WQ_ROWS, :],
                                             states[j].astype(BF16),
                                             preferred_element_type=F32))
        v_new = each(lambda j, b, h: (u_ref[b, rows, hcols(h)] - ws_qs[j][:c]).astype(BF16))
        kv_qv = each(lambda j, b, h: jnp.dot(kq_ref[h, b, n * GDN_KQ_ROWS:(n + 1) * GDN_KQ_ROWS, :], v_new[j],
                                             preferred_element_type=F32))
        states = each(lambda j, b, h: states[j] * egl_ref[b, n:n + 1, h:h + 1] + kv_qv[j][:GDN_HEAD_DIM])
        for j, (b, h) in enumerate(chains):
            o = _rms(ws_qs[j][c:] + kv_qv[j][GDN_HEAD_DIM:], ng_ref[...])
            og_ref[b, rows, hcols(h)] = (o * _silu(gz_ref[b, rows, hcols(h)])).astype(BF16)
    for j in range(len(chains)):
        state_ref[j] = states[j]


def _gdn_scan(egl3, u3, wq4, kq4, gz3, norm_g, g):
    nb, s, _ = u3.shape
    nchunks = s // GDN_CHUNK
    blk = lambda i: (0, i, 0)
    hblk = lambda i: (0, 0, i, 0)
    return pl.pallas_call(
        functools.partial(_gdn_scan_kernel, nb=nb, g=g),
        grid=(nchunks // g,),
        in_specs=[pl.BlockSpec((nb, g, LANES), blk),
                  pl.BlockSpec((nb, g * GDN_CHUNK, GDN_WIDTH), blk),
                  pl.BlockSpec((GDN_HEADS, nb, g * GDN_WQ_ROWS, GDN_HEAD_DIM), hblk),
                  pl.BlockSpec((GDN_HEADS, nb, g * GDN_KQ_ROWS, GDN_CHUNK), hblk),
                  pl.BlockSpec((nb, g * GDN_CHUNK, GDN_WIDTH), blk),
                  pl.BlockSpec((1, GDN_HEAD_DIM), lambda i: (0, 0))],
        out_specs=pl.BlockSpec((nb, g * GDN_CHUNK, GDN_WIDTH), blk),
        out_shape=jax.ShapeDtypeStruct((nb, s, GDN_WIDTH), BF16),
        scratch_shapes=[pltpu.VMEM((nb * GDN_HEADS, GDN_HEAD_DIM, GDN_HEAD_DIM), F32)],
        compiler_params=pltpu.CompilerParams(dimension_semantics=("arbitrary",),
                                             vmem_limit_bytes=VMEM_LIMIT),
        name="gdn_scan",
    )(egl3, u3, wq4, kq4, gz3, norm_g)


FFN_FC = 256


FFN_TILE = 512


def _ffn_kernel(x_ref, oa_ref, og_ref, wa_ref, wg_ref, gffn_ref, wup_ref, cw_ref, cb_ref, wd_ref, g_ref, out_ref,
                ubuf_ref, carry_ref, acc_ref, perm_ref, *, nsub, tiles_per_seq):
    i = pl.program_id(0)
    tm = FFN_TILE
    tile = SUBLANES
    halo = (FFN_CONV - 1) * tile

    @pl.when(i == 0)
    def _():
        carry_ref[...] = jnp.zeros_like(carry_ref)

    nslab = D_FF // FFN_FC
    col_of = lambda cidx, part: part * D_FF + cidx * FFN_FC
    slot_of = lambda cidx, part: 2 * (cidx % 2) + part
    row8 = lax.broadcasted_iota(jnp.int32, (tile, FFN_FC), 0)
    rows_of = lambda sub: slice(sub * tm, (sub + 1) * tm)

    def prologue(sub):
        rows = rows_of(sub)
        x1 = _permute_rows(perm_ref, x_ref[rows, :]
                           + jnp.dot(oa_ref[rows, :], wa_ref[...], preferred_element_type=F32)
                           + jnp.dot(og_ref[rows, :], wg_ref[...], preferred_element_type=F32))
        acc_ref[rows, :] = x1
        return _rms(x1, gffn_ref[...]).astype(BF16)

    def epilogue(sub):
        _unpermute_rows(perm_ref, _rms(acc_ref[rows_of(sub), :], g_ref[...]), out_ref, row0=sub * tm)

    def up(sub, h2, cidx):
        first = ((i * nsub + sub) % tiles_per_seq) == 0
        for part in range(2):
            lo, slot = col_of(cidx, part), slot_of(cidx, part)
            u = jnp.dot(h2, wup_ref[:, lo:lo + FFN_FC], preferred_element_type=F32)
            prev = carry_ref[:, lo:lo + FFN_FC]
            prev = jnp.where(first, jnp.zeros_like(prev), prev)
            for d in range(1, FFN_CONV):
                cur_t = u[tm - d * tile:tm - (d - 1) * tile, :]
                prev_t = prev[halo - d * tile:halo - (d - 1) * tile, :]
                ubuf_ref[slot, halo - d * tile:halo - (d - 1) * tile, :] = jnp.where(
                    row8 == 0, pltpu.roll(prev_t, 1, 0), pltpu.roll(cur_t, 1, 0))
            ubuf_ref[slot, halo:halo + tm, :] = u
            carry_ref[:, lo:lo + FFN_FC] = u[tm - halo:tm, :]

    def conv(cidx, part):
        lo, slot = col_of(cidx, part), slot_of(cidx, part)
        cw = cw_ref[:, lo:lo + FFN_FC]
        y = cb_ref[:, lo:lo + FFN_FC] + cw[FFN_CONV - 1:FFN_CONV] * ubuf_ref[slot, halo:halo + tm, :]
        for d in range(1, FFN_CONV):
            y = y + cw[FFN_CONV - 1 - d:FFN_CONV - d] * ubuf_ref[slot, halo - d * tile:halo - d * tile + tm, :]
        return y

    def down(sub, cidx, act):
        acc_ref[rows_of(sub), :] += jnp.dot(act, wd_ref[cidx * FFN_FC:(cidx + 1) * FFN_FC, :],
                                            preferred_element_type=F32)

    mid = nslab // 2
    h2 = prologue(0)
    for sub in range(nsub):
        h2_next, act_prev = None, None
        for slot in range(nslab + 2):
            if slot < nslab:
                up(sub, h2, slot)
            act_cur = (_silu(conv(slot - 1, 0)) * conv(slot - 1, 1)).astype(BF16) if 1 <= slot <= nslab else None
            if slot >= 2:
                down(sub, slot - 2, act_prev)
            act_prev = act_cur
            if slot == mid:
                if sub > 0:
                    epilogue(sub - 1)
                if sub + 1 < nsub:
                    h2_next = prologue(sub + 1)
        h2 = h2_next
    epilogue(nsub - 1)


def _ffn(x2, oa2, og2, wo_a, wo_g, g_ffn, w_up, conv_w, conv_b, w_down, g, nsub, seq_len):
    m = x2.shape[0]
    tm = nsub * FFN_TILE
    row = lambda i: (i, 0)
    fixed = lambda i: (0, 0)
    once = pl.Buffered(1)
    return pl.pallas_call(
        functools.partial(_ffn_kernel, nsub=nsub, tiles_per_seq=seq_len // FFN_TILE),
        grid=(m // tm,),
        in_specs=[pl.BlockSpec((tm, D_MODEL), row),
                  pl.BlockSpec((tm, DA_WIDTH), row),
                  pl.BlockSpec((tm, GDN_WIDTH), row),
                  pl.BlockSpec((DA_WIDTH, D_MODEL), fixed, pipeline_mode=once),
                  pl.BlockSpec((GDN_WIDTH, D_MODEL), fixed, pipeline_mode=once),
                  pl.BlockSpec((1, D_MODEL), fixed),
                  pl.BlockSpec((D_MODEL, 2 * D_FF), fixed, pipeline_mode=once),
                  pl.BlockSpec((FFN_CONV, 2 * D_FF), fixed),
                  pl.BlockSpec((1, 2 * D_FF), fixed),
                  pl.BlockSpec((D_FF, D_MODEL), fixed, pipeline_mode=once),
                  pl.BlockSpec((1, D_MODEL), fixed)],
        out_specs=pl.BlockSpec((tm, D_MODEL), row),
        out_shape=jax.ShapeDtypeStruct((m, D_MODEL), F32),
        scratch_shapes=[pltpu.VMEM((4, FFN_TILE + (FFN_CONV - 1) * SUBLANES, FFN_FC), F32),
                        pltpu.VMEM(((FFN_CONV - 1) * SUBLANES, 2 * D_FF), F32),
                        pltpu.VMEM((tm, D_MODEL), F32),
                        pltpu.VMEM((D_MODEL // LANES, SUBLANES * PERM_PITCH, LANES), F32)],
        compiler_params=pltpu.CompilerParams(dimension_semantics=("arbitrary",),
                                             vmem_limit_bytes=VMEM_LIMIT),
        name="ffn",
    )(x2, oa2, og2, wo_a, wo_g, g_ffn, w_up, conv_w, conv_b, w_down, g)


def _pad_lanes(v):
    return jnp.zeros((1, LANES), F32).at[0, :v.shape[0]].set(v.astype(F32))


def kernel(x, attn_norm_g, w_in, da_lambda_q1, da_lambda_k1, da_lambda_q2, da_lambda_k2, da_subln_g,
           gdn_conv_w, gdn_a_log, gdn_dt_bias, gdn_norm_g, w_out, ffn_norm_g, w_up, ffn_conv_w,
           ffn_conv_b, w_down, final_norm_g):
    b, s, d = x.shape
    m = b * s
    tm = 512
    assert d == D_MODEL and s % tm == 0 and w_in.shape[0] == 1, (x.shape, w_in.shape)
    l = 0
    x2 = x.reshape(m, d)

    wi = w_in[l]
    w_main = wi[:, :W_MAIN_COLS].astype(BF16)
    w_z = wi[:, W_MAIN_COLS + 2 * GDN_HEADS:].astype(BF16)
    w_ab = jnp.pad(wi[:, W_MAIN_COLS:W_MAIN_COLS + 2 * GDN_HEADS], ((0, 0), (0, LANES - 2 * GDN_HEADS))).astype(BF16)
    lam_p = jnp.stack([da_lambda_q1[l], da_lambda_k1[l], da_lambda_q2[l], da_lambda_k2[l]]).astype(F32)

    qk, va, gqkv, gz, gate = _in_proj(x2, attn_norm_g[l].reshape(1, d).astype(F32), w_main, w_z, w_ab,
                                      gdn_conv_w[l].astype(F32), _pad_lanes(gdn_a_log[l]),
                                      _pad_lanes(gdn_dt_bias[l]), tm, s // tm)

    oa = _diff_attn(lam_p, da_subln_g[l].reshape(1, DA_V_DIM).astype(F32),
                    qk.reshape(b, s, 2 * DA_WIDTH), va.reshape(b, s, 2 * DA_WIDTH), 512)

    u, wq, kq, egl = _gdn_prep(gqkv, gate, 512)
    nchunks = s // GDN_CHUNK
    og = _gdn_scan(egl.reshape(b, nchunks, LANES), u.reshape(b, s, GDN_WIDTH),
                   wq.reshape(GDN_HEADS, b, nchunks * GDN_WQ_ROWS, GDN_HEAD_DIM),
                   kq.reshape(GDN_HEADS, b, nchunks * GDN_KQ_ROWS, GDN_CHUNK),
                   gz.reshape(b, s, GDN_WIDTH), gdn_norm_g[l].reshape(1, GDN_HEAD_DIM).astype(F32), 8)

    wo = w_out[l].astype(BF16)
    out = _ffn(x2, oa.reshape(m, DA_WIDTH), og.reshape(m, GDN_WIDTH), wo[:DA_WIDTH], wo[DA_WIDTH:],
               ffn_norm_g[l].reshape(1, d).astype(F32), w_up[l].astype(BF16), ffn_conv_w[l].astype(F32),
               ffn_conv_b[l].reshape(1, 2 * D_FF).astype(F32), w_down[l].astype(BF16),
               final_norm_g.reshape(1, d).astype(F32), 2, s)
    return out.reshape(b, s, d)
```
